```python
import math
import jax
import jax.numpy as jnp
from jax import lax
import numpy as np


D_MODEL = 1024
BATCH = 4
SEQ = 4096
DEPTH = 4

GRID_W = 64
CTX_LEN = 256
EPS = 1e-6
F32 = jnp.float32

HEAD_DIM = 64
ROPE_THETA = 10000.0
ROPE_PAIRS = HEAD_DIM // 4

DA_HEADS = 4
DA_VDIM = 2 * HEAD_DIM
DA_QBLOCK = 128

WG_HEADS = 8
WG_KV_HEADS = 2
WG_GROUP = WG_HEADS // WG_KV_HEADS
WINDOW = 128
WG_BLOCK = 128

GD_HEADS = 4
GD_DK = 128
GD_DV = 128
GD_CONV = 5
GD_CHUNK = 64
GD_QKV_W = GD_HEADS * (2 * GD_DK + GD_DV)

N_BRANCH = 3
BRANCH_W = DA_HEADS * DA_VDIM

SPLIT_SIZES = (
    DA_HEADS * 2 * HEAD_DIM,
    DA_HEADS * 2 * HEAD_DIM,
    DA_HEADS * DA_VDIM,
    WG_HEADS * HEAD_DIM,
    WG_KV_HEADS * HEAD_DIM,
    WG_KV_HEADS * HEAD_DIM,
    GD_QKV_W,
    GD_HEADS * GD_DV,
    2 * GD_HEADS,
    2 * GD_HEADS,
    N_BRANCH * D_MODEL,
)
IN_COLS = sum(SPLIT_SIZES)

PEER_HEADS = 8
PEER_DQ = 256
PEER_HALF = PEER_DQ // 2
N_KEYS = 128
N_EXPERTS = N_KEYS * N_KEYS
PEER_TOPK = 16
PEER_BLOCK = 128

kernel_name = 'hybrid_diffusion_gated_branches_peer'


def rms_norm(x, g):
    xf = x.astype(F32)
    y = xf * lax.rsqrt(jnp.mean(xf * xf, axis=-1, keepdims=True) + EPS)
    return (y * g.astype(F32)).astype(x.dtype)


def l2_norm(x):
    return x * lax.rsqrt(jnp.sum(x * x, axis=-1, keepdims=True) + EPS)


def rope_2d_tables(rows):
    row = jnp.repeat(jnp.arange(rows), GRID_W).astype(F32)
    col = jnp.tile(jnp.arange(GRID_W), rows).astype(F32)
    inv = jnp.power(ROPE_THETA, -jnp.arange(ROPE_PAIRS, dtype=F32) / ROPE_PAIRS)
    ar = row[:, None] * inv
    ac = col[:, None] * inv
    ang = jnp.concatenate([ar, ar, ac, ac], axis=-1)
    return jnp.cos(ang), jnp.sin(ang)


def apply_rope(x, cos, sin):
    shp = (1, cos.shape[0]) + (1,) * (x.ndim - 3) + (HEAD_DIM,)
    xr = x.reshape(x.shape[:-1] + (2, 2, ROPE_PAIRS))
    rot = jnp.stack([-xr[..., 1, :], xr[..., 0, :]], axis=-2).reshape(x.shape)
    return x * cos.reshape(shp).astype(x.dtype) + rot * sin.reshape(shp).astype(x.dtype)


def split_cols(p):
    idx = [int(v) for v in np.cumsum(SPLIT_SIZES)[:-1]]
    return jnp.split(p, idx, axis=-1)


def ada_mod(cvec, w, b):
    mod = jax.nn.silu(cvec) @ w + b
    return [t[:, None, :] for t in jnp.split(mod, 6, axis=-1)]


def modulate(x, g, shift, scale):
    return rms_norm(x, g) * (1.0 + scale) + shift


def diff_core(q, k, v, lam):
    s = jnp.einsum('bqhmd,bkhmd->bhmqk', q, k).astype(F32) * (HEAD_DIM ** -0.5)
    p = jax.nn.softmax(s, axis=-1)
    w = (p[:, :, 0] - lam * p[:, :, 1]).astype(v.dtype)
    return jnp.einsum('bhqk,bkhe->bqhe', w, v)


def diff_attention(q, k, v, qc, kc, vc, q_g, k_g, lam_vecs, subln_g, lam_init, cos, sin, need_ctx):
    bn, s_len, _ = q.shape
    l_len = kc.shape[1]
    heads = lambda t, n: t.reshape(bn, n, DA_HEADS, 2, HEAD_DIM)
    q = apply_rope(rms_norm(heads(q, s_len), q_g), cos, sin)
    k = apply_rope(rms_norm(heads(k, s_len), k_g), cos, sin)
    v = v.reshape(bn, s_len, DA_HEADS, DA_VDIM)
    kc = rms_norm(heads(kc, l_len), k_g)
    vc = vc.reshape(bn, l_len, DA_HEADS, DA_VDIM)
    lv = lam_vecs.astype(F32)
    lam = jnp.exp(jnp.sum(lv[0, 0] * lv[0, 1])) - jnp.exp(jnp.sum(lv[1, 0] * lv[1, 1])) + lam_init
    k_all = jnp.concatenate([kc, k], axis=1)
    v_all = jnp.concatenate([vc, v], axis=1)
    nb = s_len // DA_QBLOCK
    qb = jnp.moveaxis(q.reshape(bn, nb, DA_QBLOCK, DA_HEADS, 2, HEAD_DIM), 1, 0)
    o = lax.map(lambda qq: diff_core(qq, k_all, v_all, lam), qb)
    o = jnp.moveaxis(o, 0, 1).reshape(bn, s_len, DA_HEADS, DA_VDIM)

    def post(t):
        return (rms_norm(t, subln_g) * (1.0 - lam_init)).reshape(t.shape[0], t.shape[1], BRANCH_W)

    out = post(o)
    out_c = post(diff_core(rms_norm(heads(qc, l_len), q_g), kc, vc, lam)) if need_ctx else None
    return out, out_c


def window_gqa(q, k, v, qc, kc, vc, q_g, k_g, sink, cos, sin, need_ctx):
    bn, s_len, _ = q.shape
    l_len = kc.shape[1]
    nb = s_len // WG_BLOCK
    scale = HEAD_DIM ** -0.5
    q = apply_rope(rms_norm(q.reshape(bn, s_len, WG_KV_HEADS, WG_GROUP, HEAD_DIM), q_g), cos, sin)
    k = apply_rope(rms_norm(k.reshape(bn, s_len, WG_KV_HEADS, HEAD_DIM), k_g), cos, sin)
    v = v.reshape(bn, s_len, WG_KV_HEADS, HEAD_DIM)
    kc = rms_norm(kc.reshape(bn, l_len, WG_KV_HEADS, HEAD_DIM), k_g)
    vc = vc.reshape(bn, l_len, WG_KV_HEADS, HEAD_DIM)
    sink_l = sink.astype(F32).reshape(WG_KV_HEADS, WG_GROUP)

    pad = ((0, 0), (WG_BLOCK, WG_BLOCK), (0, 0), (0, 0))
    kp = jnp.pad(k, pad)
    vp = jnp.pad(v, pad)
    span = 3 * WG_BLOCK
    gidx = jnp.arange(nb)[:, None] * WG_BLOCK + jnp.arange(span)[None, :]
    kw = kp[:, gidx]
    vw = vp[:, gidx]
    qb = q.reshape(bn, nb, WG_BLOCK, WG_KV_HEADS, WG_GROUP, HEAD_DIM)
    s_ctx = jnp.einsum('bnihgd,bjhd->bnhgij', qb, kc).astype(F32) * scale
    s_win = jnp.einsum('bnihgd,bnjhd->bnhgij', qb, kw).astype(F32) * scale
    rel = jnp.arange(span)[None, :] - WG_BLOCK - jnp.arange(WG_BLOCK)[:, None]
    kpos = gidx - WG_BLOCK
    mask = (jnp.abs(rel) <= WINDOW)[None] & ((kpos >= 0) & (kpos < s_len))[:, None, :]
    s_win = jnp.where(mask[None, :, None, None], s_win, -jnp.inf)
    sink_col = jnp.broadcast_to(sink_l[None, None, :, :, None, None], s_ctx.shape[:-1] + (1,))
    p = jax.nn.softmax(jnp.concatenate([sink_col, s_ctx, s_win], axis=-1), axis=-1)
    p_ctx = p[..., 1:1 + l_len].astype(v.dtype)
    p_win = p[..., 1 + l_len:].astype(v.dtype)
    o = jnp.einsum('bnhgij,bjhd->bnihgd', p_ctx, vc) + jnp.einsum('bnhgij,bnjhd->bnihgd', p_win, vw)
    o = o.reshape(bn, s_len, BRANCH_W)

    oc = None
    if need_ctx:
        qcc = rms_norm(qc.reshape(bn, l_len, WG_KV_HEADS, WG_GROUP, HEAD_DIM), q_g)
        sc = jnp.einsum('bihgd,bjhd->bhgij', qcc, kc).astype(F32) * scale
        sink_c = jnp.broadcast_to(sink_l[None, :, :, None, None], sc.shape[:-1] + (1,))
        pc = jax.nn.softmax(jnp.concatenate([sink_c, sc], axis=-1), axis=-1)[..., 1:].astype(vc.dtype)
        oc = jnp.einsum('bhgij,bjhd->bihgd', pc, vc).reshape(bn, l_len, BRANCH_W)
    return o, oc


def depthwise_conv(x, w):
    return lax.conv_general_dilated(x, w[:, None, :], window_strides=(1,), padding='SAME',
                                    dimension_numbers=('NWC', 'WIO', 'NWC'),
                                    feature_group_count=x.shape[-1])


def gdn_chunk_scan(q, k, v, g, beta, state):
    bn, t_len, h = q.shape[:3]
    n = t_len // GD_CHUNK
    c = GD_CHUNK

    def chunks(a):
        a = a.reshape((bn, n, c, h) + a.shape[3:])
        return jnp.moveaxis(a, (1, 3), (0, 2))

    qc, kc, vc = chunks(q), chunks(k), chunks(v)
    gc = jnp.cumsum(chunks(g), axis=-1)
    bc = chunks(beta)
    incl = jnp.tril(jnp.ones((c, c), bool))
    strict = jnp.tril(jnp.ones((c, c), bool), -1)
    decay = jnp.exp(jnp.where(incl, gc[..., :, None] - gc[..., None, :], -jnp.inf))
    kb = kc * bc[..., None]
    a_mat = jnp.where(strict, jnp.einsum('nbhid,nbhjd->nbhij', kb, kc) * decay, 0.0)
    rhs = jnp.concatenate([vc * bc[..., None], kb * jnp.exp(gc)[..., None]], axis=-1)
    sol = lax.linalg.triangular_solve(a_mat + jnp.eye(c, dtype=F32), rhs, left_side=True, lower=True)
    u, w = sol[..., :GD_DV], sol[..., GD_DV:]
    qk = jnp.einsum('nbhid,nbhjd->nbhij', qc, kc) * decay
    qg = qc * jnp.exp(gc)[..., None]
    g_last = gc[..., -1]
    kd = kc * jnp.exp(g_last[..., None] - gc)[..., None]

    def step(s, xs):
        qg_i, qk_i, u_i, w_i, kd_i, gl_i = xs
        v_new = u_i - jnp.einsum('bhck,bhkv->bhcv', w_i, s)
        o = jnp.einsum('bhck,bhkv->bhcv', qg_i, s) + jnp.einsum('bhij,bhjv->bhiv', qk_i, v_new)
        s = s * jnp.exp(gl_i)[..., None, None] + jnp.einsum('bhck,bhcv->bhkv', kd_i, v_new)
        return s, o

    s_fin, o = lax.scan(step, state, (qg, qk, u, w, kd, g_last))
    o = jnp.moveaxis(o, (0, 2), (1, 3)).reshape(bn, t_len, h, GD_DV)
    return s_fin, o


def gated_deltanet(qkv, z, a, b, qkv_c, z_c, a_c, b_c, conv_w, a_log, dt_bias, norm_g, need_ctx):
    def prep(qkv_s, a_s, b_s):
        bn, t_len = qkv_s.shape[:2]
        y = jax.nn.silu(depthwise_conv(qkv_s, conv_w)).astype(F32)
        qq, kk, vv = jnp.split(y, [GD_HEADS * GD_DK, 2 * GD_HEADS * GD_DK], axis=-1)
        qq = l2_norm(qq.reshape(bn, t_len, GD_HEADS, GD_DK)) * (GD_DK ** -0.5)
        kk = l2_norm(kk.reshape(bn, t_len, GD_HEADS, GD_DK))
        vv = vv.reshape(bn, t_len, GD_HEADS, GD_DV)
        aa = a_s.astype(F32).reshape(bn, t_len, 2, GD_HEADS)
        gg = -jnp.exp(a_log.astype(F32)) * jax.nn.softplus(aa + dt_bias.astype(F32))
        bb = jax.nn.sigmoid(b_s.astype(F32).reshape(bn, t_len, 2, GD_HEADS))
        return qq, kk, vv, gg, bb

    ql, kl, vl, gl, bl = prep(qkv, a, b)
    qc, kc, vc, gc, bc = prep(qkv_c, a_c, b_c)
    zeros = jnp.zeros((qc.shape[0], GD_HEADS, GD_DK, GD_DV), F32)
    o_lat, o_ctx = 0.0, 0.0
    for d in range(2):
        f = (lambda t: jnp.flip(t, axis=1)) if d == 1 else (lambda t: t)
        s_c, oc_d = gdn_chunk_scan(f(qc), f(kc), f(vc), f(gc[:, :, d]), f(bc[:, :, d]), zeros)
        _, ol_d = gdn_chunk_scan(f(ql), f(kl), f(vl), f(gl[:, :, d]), f(bl[:, :, d]), s_c)
        o_lat = o_lat + f(ol_d)
        o_ctx = o_ctx + f(oc_d)

    def gated_out(o, zz):
        bn, t_len = o.shape[:2]
        zf = zz.reshape(bn, t_len, GD_HEADS, GD_DV).astype(F32)
        return (rms_norm(o, norm_g) * jax.nn.silu(zf)).reshape(bn, t_len, BRANCH_W).astype(zz.dtype)

    out = gated_out(o_lat, z)
    out_c = gated_out(o_ctx, z_c) if need_ctx else None
    return out, out_c


def merge_branches(outs, gate_logits, b_gate, w_branch, w_out):
    bn, t_len = gate_logits.shape[:2]
    o = jnp.stack(outs, axis=2)
    y = jnp.einsum('btrw,rwd->btrd', o, w_branch)
    gate = jax.nn.sigmoid(gate_logits + b_gate).reshape(bn, t_len, N_BRANCH, D_MODEL)
    return jnp.sum(gate * y, axis=2) @ w_out


def token_mixer(h, hc, cos, sin, w_in, b_gate, qk_g, diff_lam, diff_subln_g, wg_sink,
                gd_conv_w, gd_a_log, gd_dt_bias, gd_norm_g, w_branch, w_out, lam_init, need_ctx):
    qA, kA, vA, qB, kB, vB, qkvC, zC, aC, bC, gl = split_cols(h @ w_in)
    qAc, kAc, vAc, qBc, kBc, vBc, qkvCc, zCc, aCc, bCc, gc = split_cols(hc @ w_in)
    oA, oAc = diff_attention(qA, kA, vA, qAc, kAc, vAc, qk_g[0], qk_g[1], diff_lam, diff_subln_g,
                             lam_init, cos, sin, need_ctx)
    oB, oBc = window_gqa(qB, kB, vB, qBc, kBc, vBc, qk_g[2], qk_g[3], wg_sink, cos, sin, need_ctx)
    oC, oCc = gated_deltanet(qkvC, zC, aC, bC, qkvCc, zCc, aCc, bCc, gd_conv_w, gd_a_log,
                             gd_dt_bias, gd_norm_g, need_ctx)
    m = merge_branches((oA, oB, oC), gl, b_gate, w_branch, w_out)
    mc = merge_branches((oAc, oBc, oCc), gc, b_gate, w_branch, w_out) if need_ctx else None
    return m, mc


def peer_ffn(h, wq, keys, u_tab, v_tab):
    shp = h.shape
    t = h.reshape(-1, D_MODEL)
    n = t.shape[0]
    q = (t @ wq).reshape(n, PEER_HEADS, 2, PEER_HALF)
    s = jnp.einsum('thcd,cnd->thcn', q, keys).astype(F32)
    sv, si = lax.top_k(s, PEER_TOPK)
    cand = (sv[:, :, 0, :, None] + sv[:, :, 1, None, :]).reshape(n, PEER_HEADS, PEER_TOPK * PEER_TOPK)
    cv, ci = lax.top_k(cand, PEER_TOPK)
    i1 = jnp.take_along_axis(si[:, :, 0], ci // PEER_TOPK, axis=-1)
    i2 = jnp.take_along_axis(si[:, :, 1], ci % PEER_TOPK, axis=-1)
    eid = (i1 * N_KEYS + i2).reshape(n, PEER_HEADS * PEER_TOPK)
    gw = jax.nn.softmax(cv, axis=-1).reshape(n, PEER_HEADS * PEER_TOPK).astype(h.dtype)
    nb = n // PEER_BLOCK

    def block(args):
        tb, eb, gb = args
        act = jax.nn.gelu(jnp.einsum('pd,ped->pe', tb, u_tab[eb]), approximate=False)
        return jnp.einsum('pe,ped->pd', act * gb, v_tab[eb])

    out = lax.map(block, (t.reshape(nb, PEER_BLOCK, D_MODEL),
                          eid.reshape(nb, PEER_BLOCK, -1),
                          gw.reshape(nb, PEER_BLOCK, -1)))
    return out.reshape(shp)


def setup_inputs(seed: int = 0) -> dict:
    key = jax.random.key(seed)
    ks = jax.random.split(key, 24)
    D = D_MODEL

    def nrm(k, shape, s):
        return jax.random.normal(k, shape, F32) * s

    dt = jnp.exp(jax.random.uniform(ks[15], (DEPTH, 2, GD_HEADS), F32)
                 * (math.log(0.1) - math.log(0.001)) + math.log(0.001))
    return {
        'x': nrm(ks[0], (BATCH, SEQ, D), 1.0),
        'c': nrm(ks[1], (BATCH, D), 1.0),
        'ctx': nrm(ks[2], (BATCH, CTX_LEN, D), 1.0),
        'c_ctx': nrm(ks[3], (D,), 1.0),
        'norm1_g': 1.0 + nrm(ks[4], (DEPTH, D), 0.02),
        'norm2_g': 1.0 + nrm(ks[5], (DEPTH, D), 0.02),
        'w_ada': nrm(ks[6], (DEPTH, D, 6 * D), 0.5 * D ** -0.5),
        'b_ada': nrm(ks[7], (DEPTH, 6 * D), 0.02),
        'w_in': nrm(ks[8], (DEPTH, D, IN_COLS), D ** -0.5),
        'b_gate': nrm(ks[9], (DEPTH, N_BRANCH * D), 0.02),
        'qk_norm_g': 1.0 + nrm(ks[10], (DEPTH, 4, HEAD_DIM), 0.02),
        'diff_lam': nrm(ks[11], (DEPTH, 2, 2, HEAD_DIM), 0.1),
        'diff_subln_g': 1.0 + nrm(ks[12], (DEPTH, DA_VDIM), 0.02),
        'wg_sink': nrm(ks[13], (DEPTH, WG_HEADS), 1.0),
        'gd_conv_w': nrm(ks[14], (DEPTH, GD_CONV, GD_QKV_W), GD_CONV ** -0.5),
        'gd_a_log': jnp.log(jax.random.uniform(ks[16], (DEPTH, 2, GD_HEADS), F32, 1.0, 16.0)),
        'gd_dt_bias': dt + jnp.log(-jnp.expm1(-dt)),
        'gd_norm_g': 1.0 + nrm(ks[17], (DEPTH, GD_DV), 0.02),
        'w_branch': nrm(ks[18], (DEPTH, N_BRANCH, BRANCH_W, D), BRANCH_W ** -0.5),
        'w_out': nrm(ks[19], (DEPTH, D, D), D ** -0.5),
        'peer_wq': nrm(ks[20], (DEPTH, D, PEER_HEADS * PEER_DQ), D ** -0.5),
        'peer_keys': nrm(ks[21], (DEPTH, 2, N_KEYS, PEER_HALF), PEER_HALF ** -0.5),
        'peer_u': nrm(ks[22], (DEPTH, N_EXPERTS, D), D ** -0.5),
        'peer_v': nrm(ks[23], (DEPTH, N_EXPERTS, D), PEER_HEADS ** -0.5),
    }


def reference(x, c, ctx, c_ctx, norm1_g, norm2_g, w_ada, b_ada, w_in, b_gate, qk_norm_g,
              diff_lam, diff_subln_g, wg_sink, gd_conv_w, gd_a_log, gd_dt_bias, gd_norm_g,
              w_branch, w_out, peer_wq, peer_keys, peer_u, peer_v):
    s_len = x.shape[1]
    rows = s_len // GRID_W
    cos, sin = rope_2d_tables(rows)
    xc = ctx
    for l in range(DEPTH):
        need_ctx = l < DEPTH - 1
        lam_init = 0.8 - 0.6 * math.exp(-0.3 * l)
        sh1, sc1, gt1, sh2, sc2, gt2 = ada_mod(c, w_ada[l], b_ada[l])
        sh1c, sc1c, gt1c, sh2c, sc2c, gt2c = ada_mod(c_ctx[None], w_ada[l], b_ada[l])
        h = modulate(x, norm1_g[l], sh1, sc1)
        hc = modulate(xc, norm1_g[l], sh1c, sc1c)
        m, mc = token_mixer(h, hc, cos, sin, w_in[l], b_gate[l], qk_norm_g[l], diff_lam[l],
                            diff_subln_g[l], wg_sink[l], gd_conv_w[l], gd_a_log[l], gd_dt_bias[l],
                            gd_norm_g[l], w_branch[l], w_out[l], lam_init, need_ctx)
        x = x + gt1 * m
        x = x + gt2 * peer_ffn(modulate(x, norm2_g[l], sh2, sc2), peer_wq[l], peer_keys[l],
                               peer_u[l], peer_v[l])
        if need_ctx:
            xc = xc + gt1c * mc
            xc = xc + gt2c * peer_ffn(modulate(xc, norm2_g[l], sh2c, sc2c), peer_wq[l],
                                      peer_keys[l], peer_u[l], peer_v[l])
    return x
```

```python
import functools
import math

import jax
import jax.numpy as jnp
import numpy as np
from jax import lax
from jax.experimental import pallas as pl
from jax.experimental.pallas import tpu as pltpu

F32 = jnp.float32
BF16 = jnp.bfloat16
HIGHEST = lax.Precision.HIGHEST

EPS = 1e-6
GRID_W = 64
HEAD_DIM = 64
ROPE_THETA = 10000.0
ROPE_PAIRS = HEAD_DIM // 4
DA_HEADS = 4
DA_VDIM = 2 * HEAD_DIM
WG_HEADS = 8
WG_KV_HEADS = 2
WG_GROUP = WG_HEADS // WG_KV_HEADS
WINDOW = 128
WG_BLOCK = 128
GD_HEADS = 4
GD_DK = 128
GD_DV = 128
GD_CONV = 5
GD_CHUNK = 64
GD_QKV_W = GD_HEADS * (2 * GD_DK + GD_DV)
N_BRANCH = 3
PEER_HEADS = 8
PEER_HALF = 128
N_KEYS = 128
PEER_TOPK = 16

LANES = 128
VMEM_LIMIT = 56 * 1024 * 1024
NEG = -1e30


def _pick(n, cands):
    for c in cands:
        if n % c == 0:
            return c
    raise ValueError(f"no tile in {cands} divides {n}")


def _params(sem):
    return pltpu.CompilerParams(dimension_semantics=sem, vmem_limit_bytes=VMEM_LIMIT)


def _mm_body(x_ref, w_ref, o_ref):
    o_ref[...] = jnp.dot(x_ref[...], w_ref[...], preferred_element_type=F32)


def _mm(x, w, tn=None):
    m, k = x.shape
    n = w.shape[1]
    tm = _pick(m, (512, 256, 128, 16))
    tn = tn or _pick(n, (1536, 1024, 512, 256, 128))
    return pl.pallas_call(
        _mm_body,
        grid=(n // tn, m // tm),
        in_specs=[pl.BlockSpec((tm, k), lambda j, i: (i, 0)),
                  pl.BlockSpec((k, tn), lambda j, i: (0, j))],
        out_specs=pl.BlockSpec((tm, tn), lambda j, i: (i, j)),
        out_shape=jax.ShapeDtypeStruct((m, n), F32),
        compiler_params=_params(("parallel", "parallel")),
        name="mm",
    )(x, w)


def _diff_body(lam_ref, q_ref, k_ref, v_ref, g_ref, o_ref, *, post):
    q = q_ref[0]
    k = k_ref[0]
    v = v_ref[0]
    lane = lax.broadcasted_iota(jnp.int32, q.shape, 1)
    zero = jnp.zeros_like(q)
    dn = (((1,), (1,)), ((), ()))

    def probs(qm):
        s = lax.dot_general(qm, k, dn, preferred_element_type=F32)
        m = jnp.max(s, axis=-1, keepdims=True)
        p = jnp.exp(s - m)
        return p, jnp.sum(p, axis=-1, keepdims=True)

    p1, l1 = probs(jnp.where(lane < HEAD_DIM, q, zero))
    p2, l2 = probs(jnp.where(lane >= HEAD_DIM, q, zero))
    w = p1 * (1.0 / l1) - p2 * (lam_ref[0] / l2)
    o = jnp.dot(w.astype(BF16), v, preferred_element_type=F32)
    ms = jnp.mean(o * o, axis=-1, keepdims=True)
    o_ref[0] = o * lax.rsqrt(ms + EPS) * (g_ref[...] * post)


def _diff_attn(q, k, v, lam, subln_g, post, q_start, q_len, k_len):
    b = q.shape[0]
    tq = _pick(math.gcd(q_len, q_start), (256, 128))
    qo = q_start // tq
    return pl.pallas_call(
        functools.partial(_diff_body, post=post),
        grid=(b, DA_HEADS, q_len // tq),
        in_specs=[pl.BlockSpec(memory_space=pltpu.SMEM),
                  pl.BlockSpec((1, tq, LANES), lambda bi, h, i: (bi, qo + i, h)),
                  pl.BlockSpec((1, k_len, LANES), lambda bi, h, i: (bi, 0, h)),
                  pl.BlockSpec((1, k_len, LANES), lambda bi, h, i: (bi, 0, h)),
                  pl.BlockSpec((1, LANES), lambda bi, h, i: (0, 0))],
        out_specs=pl.BlockSpec((1, tq, LANES), lambda bi, h, i: (bi, i, h)),
        out_shape=jax.ShapeDtypeStruct((b, q_len, DA_HEADS * DA_VDIM), F32),
        compiler_params=_params(("parallel", "parallel", "parallel")),
        name="diff_attn",
    )(lam, q, k, v, subln_g)


def _win_body(sink_ref, q_ref, kc_ref, vc_ref, *rest, s_len, has_window):
    if has_window:
        kp_ref, kn_ref, kx_ref, vp_ref, vn_ref, vx_ref, o_ref = rest
        kcat = jnp.concatenate([kc_ref[0], kp_ref[0], kn_ref[0], kx_ref[0]], axis=0)
        vcat = jnp.concatenate([vc_ref[0], vp_ref[0], vn_ref[0], vx_ref[0]], axis=0)
    else:
        (o_ref,) = rest
        kcat = kc_ref[0]
        vcat = vc_ref[0]
    l_len = kc_ref.shape[1]
    nk = kcat.shape[0]
    q = q_ref[0]
    lane = lax.broadcasted_iota(jnp.int32, (WG_BLOCK, LANES), 1)
    vlane = lax.broadcasted_iota(jnp.int32, vcat.shape, 1)
    vz = jnp.zeros_like(vcat)
    vhalf = [jnp.where(vlane < HEAD_DIM, vcat, vz), jnp.where(vlane >= HEAD_DIM, vcat, vz)]
    if has_window:
        n = pl.program_id(1)
        col = lax.broadcasted_iota(jnp.int32, (WG_BLOCK, nk), 1)
        row = lax.broadcasted_iota(jnp.int32, (WG_BLOCK, nk), 0)
        j = col - l_len
        rel = j - WG_BLOCK - row
        kpos = n * WG_BLOCK + j - WG_BLOCK
        valid = (col < l_len) | ((jnp.abs(rel) <= WINDOW) & (kpos >= 0) & (kpos < s_len))
    dn = (((1,), (1,)), ((), ()))
    for g in range(WG_GROUP):
        qg = q[:, g * LANES:(g + 1) * LANES]
        zq = jnp.zeros_like(qg)
        acc = None
        for kv in range(WG_KV_HEADS):
            qm = jnp.where(lane < HEAD_DIM, qg, zq) if kv == 0 else jnp.where(lane >= HEAD_DIM, qg, zq)
            s = lax.dot_general(qm, kcat, dn, preferred_element_type=F32)
            if has_window:
                s = jnp.where(valid, s, NEG)
            sk = sink_ref[kv * WG_GROUP + g]
            m = jnp.maximum(jnp.max(s, axis=-1, keepdims=True), sk)
            p = jnp.exp(s - m)
            den = jnp.sum(p, axis=-1, keepdims=True) + jnp.exp(sk - m)
            o = jnp.dot((p * (1.0 / den)).astype(BF16), vhalf[kv], preferred_element_type=F32)
            acc = o if acc is None else acc + o
        o_ref[0, :, g * LANES:(g + 1) * LANES] = acc


def _win_attn(q, k, v, sink, l_len, has_window):
    b, t, _ = q.shape
    s_len = t - l_len
    blk = WG_BLOCK
    assert l_len % blk == 0 and s_len % blk == 0
    lo = l_len // blk
    ctx_spec = pl.BlockSpec((1, l_len, LANES), lambda bi, n: (bi, 0, 0))
    if has_window:
        nq = s_len // blk
        pad = ((0, 0), (blk, blk), (0, 0))
        kp = jnp.pad(k[:, l_len:], pad)
        vp = jnp.pad(v[:, l_len:], pad)
        win_specs = [pl.BlockSpec((1, blk, LANES), lambda bi, n, d=d: (bi, n + d, 0)) for d in range(3)]
        in_specs = [pl.BlockSpec(memory_space=pltpu.SMEM),
                    pl.BlockSpec((1, blk, 4 * LANES), lambda bi, n: (bi, lo + n, 0)),
                    ctx_spec, ctx_spec] + win_specs + win_specs
        args = (sink, q, k, v, kp, kp, kp, vp, vp, vp)
    else:
        nq = lo
        in_specs = [pl.BlockSpec(memory_space=pltpu.SMEM),
                    pl.BlockSpec((1, blk, 4 * LANES), lambda bi, n: (bi, n, 0)),
                    ctx_spec, ctx_spec]
        args = (sink, q, k, v)
    return pl.pallas_call(
        functools.partial(_win_body, s_len=s_len, has_window=has_window),
        grid=(b, nq),
        in_specs=in_specs,
        out_specs=pl.BlockSpec((1, blk, 4 * LANES), lambda bi, n: (bi, n, 0)),
        out_shape=jax.ShapeDtypeStruct((b, nq * blk, 4 * LANES), F32),
        compiler_params=_params(("parallel", "parallel")),
        name="win_attn",
    )(*args)


def _hdot(a, b):
    return jnp.dot(a, b, precision=HIGHEST, preferred_element_type=F32)


def _gdn_local_body(q_ref, k_ref, v_ref, kt_ref, gcol_ref, grow_ref, bcol_ref,
                    u_ref, w_ref, qg_ref, qk_ref, kdt_ref, egl_ref):
    c = GD_CHUNK
    ii = lax.broadcasted_iota(jnp.int32, (c, c), 0)
    jj = lax.broadcasted_iota(jnp.int32, (c, c), 1)
    incl = jj <= ii
    strict = jj < ii
    eye = (ii == jj).astype(F32)
    for h in range(GD_HEADS):
        sl = slice(h * LANES, (h + 1) * LANES)
        q = q_ref[0, :, sl]
        k = k_ref[0, :, sl]
        v = v_ref[0, :, sl]
        kt = kt_ref[0, 0, h]
        gcol = gcol_ref[0, 0, h]
        grow = grow_ref[0, 0, h]
        bcol = bcol_ref[0, 0, h]
        decay = jnp.exp(jnp.where(incl, gcol - grow, NEG))
        kb = k * bcol
        a = jnp.where(strict, _hdot(kb, kt) * decay, 0.0)
        tinv = eye - a
        p = a
        for _ in range(5):
            p = _hdot(p, p)
            tinv = tinv + _hdot(tinv, p)
        eg = jnp.exp(gcol)
        rhs = jnp.concatenate([v * bcol, kb * eg], axis=1)
        sol = _hdot(tinv, rhs)
        u_ref[0, :, sl] = sol[:, :LANES]
        w_ref[0, :, sl] = sol[:, LANES:]
        qg_ref[0, :, sl] = q * eg
        qk_ref[0, 0, h] = _hdot(q, kt) * decay
        glast = grow[:, c - 1:c]
        kdt_ref[0, 0, h] = kt * jnp.exp(glast - grow)
        egl_ref[0, 0, h] = jnp.broadcast_to(jnp.exp(glast), (1, LANES))


def _gdn_scan_body(u_ref, w_ref, qg_ref, qk_ref, kdt_ref, egl_ref, o_ref, s_ref):
    @pl.when(pl.program_id(1) == 0)
    def _():
        s_ref[...] = jnp.zeros_like(s_ref)

    c = GD_CHUNK
    for h in range(GD_HEADS):
        sl = slice(h * LANES, (h + 1) * LANES)
        s = s_ref[h]
        r = _hdot(jnp.concatenate([w_ref[0, :, sl], qg_ref[0, :, sl]], axis=0), s)
        v_new = u_ref[0, :, sl] - r[:c]
        o_ref[0, :, sl] = r[c:] + _hdot(qk_ref[0, 0, h], v_new)
        s_ref[h] = s * egl_ref[0, 0, h] + _hdot(kdt_ref[0, 0, h], v_new)


def _gdn(q, k, v, g, beta):
    cb, t, _ = q.shape
    c = GD_CHUNK
    nc = t // c
    h = GD_HEADS
    gc = jnp.cumsum(g.reshape(cb, nc, c, h), axis=2)
    gcol = jnp.moveaxis(gc, 3, 2)[..., None]
    grow = jnp.moveaxis(gc, 3, 2)[:, :, :, None, :]
    bcol = jnp.moveaxis(beta.reshape(cb, nc, c, h), 3, 2)[..., None]
    kt = jnp.transpose(k.reshape(cb, nc, c, h, GD_DK), (0, 1, 3, 4, 2))

    tok = pl.BlockSpec((1, c, h * LANES), lambda i, j: (i, j, 0))
    ch5 = lambda a, b2: pl.BlockSpec((1, 1, h, a, b2), lambda i, j: (i, j, 0, 0, 0))
    tok_shape = jax.ShapeDtypeStruct((cb, t, h * LANES), F32)
    u, w, qg, qk, kdt, egl = pl.pallas_call(
        _gdn_local_body,
        grid=(cb, nc),
        in_specs=[tok, tok, tok, ch5(GD_DK, c), ch5(c, 1), ch5(1, c), ch5(c, 1)],
        out_specs=[tok, tok, tok, ch5(c, c), ch5(GD_DK, c), ch5(1, LANES)],
        out_shape=[tok_shape, tok_shape, tok_shape,
                   jax.ShapeDtypeStruct((cb, nc, h, c, c), F32),
                   jax.ShapeDtypeStruct((cb, nc, h, GD_DK, c), F32),
                   jax.ShapeDtypeStruct((cb, nc, h, 1, LANES), F32)],
        compiler_params=_params(("parallel", "parallel")),
        name="gdn_local",
    )(q, k, v, kt, gcol, grow, bcol)
    return pl.pallas_call(
        _gdn_scan_body,
        grid=(cb, nc),
        in_specs=[tok, tok, tok, ch5(c, c), ch5(GD_DK, c), ch5(1, LANES)],
        out_specs=tok,
        out_shape=tok_shape,
        scratch_shapes=[pltpu.VMEM((h, GD_DK, GD_DV), F32)],
        compiler_params=_params(("parallel", "arbitrary")),
        name="gdn_scan",
    )(u, w, qg, qk, kdt, egl)


_CAND = [(r1, r2) for r1 in range(1, PEER_TOPK + 2) for r2 in range(1, PEER_TOPK + 2)
         if r1 * r2 <= PEER_TOPK + 1]


def _extract_top(cur, n):
    rank = jnp.full(cur.shape, 99.0, F32)
    vals = []
    for r in range(n):
        m = jnp.max(cur, axis=0, keepdims=True)
        hit = cur == m
        rank = jnp.where(hit, float(r + 1), rank)
        cur = jnp.where(hit, -jnp.inf, cur)
        vals.append(m)
    return vals, rank


def _topk_body(q_ref, keys_ref, cnt1_ref, e1_ref, rk2_ref, e2_ref):
    dn = (((1,), (1,)), ((), ()))
    nr = PEER_TOPK + 1
    for h in range(PEER_HEADS):
        st = []
        for c in range(2):
            qs = q_ref[:, (2 * h + c) * LANES:(2 * h + c + 1) * LANES]
            st.append(lax.dot_general(keys_ref[c], qs, dn, precision=HIGHEST,
                                      preferred_element_type=F32))
        a, rank1 = _extract_top(st[0], nr)
        b, rank2 = _extract_top(st[1], nr)
        rows = [a[r1 - 1] + b[r2 - 1] for r1, r2 in _CAND]
        pad = (-len(rows)) % 8
        cand = jnp.concatenate(rows + [jnp.full_like(rows[0], -jnp.inf)] * pad, axis=0)
        top, _ = _extract_top(cand, nr)
        tau = 0.5 * (top[PEER_TOPK - 1] + top[PEER_TOPK])
        z = jnp.sum(jnp.where(cand >= tau, jnp.exp(cand - top[0]), 0.0), axis=0, keepdims=True)
        cnt1 = jnp.zeros_like(rank1)
        for r1 in range(1, PEER_TOPK + 1):
            cnt = None
            for r2 in range(1, PEER_TOPK + 1):
                if r1 * r2 <= PEER_TOPK + 1:
                    t = (a[r1 - 1] + b[r2 - 1] >= tau).astype(F32)
                    cnt = t if cnt is None else cnt + t
            cnt1 = jnp.where(rank1 == float(r1), cnt, cnt1)
        cnt1_ref[h] = cnt1
        e1_ref[h] = jnp.exp(st[0] - a[0])
        rk2_ref[h] = rank2.astype(BF16)
        e2_ref[h] = (jnp.exp(st[1] - b[0]) * (1.0 / z)).astype(BF16)


def _peer_topk(q, keys):
    n = q.shape[0]
    tm = _pick(n, (256, 128))
    out = pl.BlockSpec((PEER_HEADS, N_KEYS, tm), lambda i: (0, 0, i))
    shp = lambda dt: jax.ShapeDtypeStruct((PEER_HEADS, N_KEYS, n), dt)
    return pl.pallas_call(
        _topk_body,
        grid=(n // tm,),
        in_specs=[pl.BlockSpec((tm, 2 * PEER_HEADS * PEER_HALF), lambda i: (i, 0)),
                  pl.BlockSpec((2, N_KEYS, PEER_HALF), lambda i: (0, 0, 0))],
        out_specs=[out, out, out, out],
        out_shape=[shp(F32), shp(F32), shp(BF16), shp(BF16)],
        compiler_params=_params(("parallel",)),
        name="peer_topk",
    )(q, keys)


def _dense_body(xt_ref, u_ref, vt_ref, cnt1_ref, e1_ref, rk2_ref, e2_ref, o_ref, *, te):
    e = pl.program_id(1)

    @pl.when(e == 0)
    def _():
        o_ref[...] = jnp.zeros_like(o_ref)

    act = jnp.dot(u_ref[...], xt_ref[...], preferred_element_type=F32)
    act = (0.5 * act * (1.0 + lax.erf(act * math.sqrt(0.5)))).astype(BF16)
    zero = jnp.zeros((N_KEYS, act.shape[1]), BF16)
    parts = []
    for j in range(te // N_KEYS):
        i1 = e * (te // N_KEYS) + j
        g = zero
        for h in range(PEER_HEADS):
            cnt = cnt1_ref[h, pl.ds(i1, 1), :].astype(BF16)
            w1 = e1_ref[h, pl.ds(i1, 1), :].astype(BF16)
            g = g + jnp.where(rk2_ref[h] <= cnt, e2_ref[h], zero) * w1
        parts.append(g)
    gate = jnp.concatenate(parts, axis=0)
    o_ref[...] += jnp.dot(vt_ref[...], act * gate, preferred_element_type=F32)


def _peer_dense(xt, u, vt, cnt1, e1, rk2, e2):
    d, n = xt.shape
    ne = u.shape[0]
    tm = _pick(n, (512, 256, 128))
    te = _pick(ne, (1024, 512))
    sel = pl.BlockSpec((PEER_HEADS, N_KEYS, tm), lambda i, e: (0, 0, i))
    return pl.pallas_call(
        functools.partial(_dense_body, te=te),
        grid=(n // tm, ne // te),
        in_specs=[pl.BlockSpec((d, tm), lambda i, e: (0, i)),
                  pl.BlockSpec((te, d), lambda i, e: (e, 0)),
                  pl.BlockSpec((d, te), lambda i, e: (0, e)),
                  sel, sel, sel, sel],
        out_specs=pl.BlockSpec((d, tm), lambda i, e: (0, i)),
        out_shape=jax.ShapeDtypeStruct((d, n), F32),
        compiler_params=_params(("parallel", "arbitrary")),
        name="peer_dense",
    )(xt, u, vt, cnt1, e1, rk2, e2)


def _rms(x, g):
    return x * lax.rsqrt(jnp.mean(x * x, axis=-1, keepdims=True) + EPS) * g


def _rope_tables(rows, l_len):
    row = jnp.repeat(jnp.arange(rows), GRID_W).astype(F32)
    col = jnp.tile(jnp.arange(GRID_W), rows).astype(F32)
    inv = jnp.power(ROPE_THETA, -jnp.arange(ROPE_PAIRS, dtype=F32) / ROPE_PAIRS)
    ar = row[:, None] * inv
    ac = col[:, None] * inv
    ang = jnp.concatenate([ar, ar, ac, ac], axis=-1)
    cos = jnp.concatenate([jnp.ones((l_len, HEAD_DIM), F32), jnp.cos(ang)], axis=0)
    sin = jnp.concatenate([jnp.zeros((l_len, HEAD_DIM), F32), jnp.sin(ang)], axis=0)
    return cos, sin


def _rope(x, cos, sin):
    xr = x.reshape(x.shape[:-1] + (2, 2, ROPE_PAIRS))
    rot = jnp.stack([-xr[..., 1, :], xr[..., 0, :]], axis=-2).reshape(x.shape)
    return x * cos[None, :, None, :] + rot * sin[None, :, None, :]


def _conv5(x, w):
    t = x.shape[1]
    xp = jnp.pad(x, ((0, 0), (GD_CONV // 2, GD_CONV // 2), (0, 0)))
    return sum(xp[:, i:i + t] * w[i] for i in range(GD_CONV))


def _l2(x):
    return x * lax.rsqrt(jnp.sum(x * x, axis=-1, keepdims=True) + EPS)


def kernel(x, c, ctx, c_ctx, norm1_g, norm2_g, w_ada, b_ada, w_in, b_gate, qk_norm_g, diff_lam,
           diff_subln_g, wg_sink, gd_conv_w, gd_a_log, gd_dt_bias, gd_norm_g, w_branch, w_out,
           peer_wq, peer_keys, peer_u, peer_v):
    bn, s_len, d = x.shape
    l_len = ctx.shape[1]
    t_len = l_len + s_len
    n_tok = bn * t_len
    depth = w_in.shape[0]
    in_cols = w_in.shape[2]
    in_pad = (-in_cols) % 1536
    cos, sin = _rope_tables(s_len // GRID_W, l_len)
    is_ctx = (jnp.arange(t_len) < l_len)[None, :, None]

    xs = jnp.concatenate([ctx, x], axis=1)
    cvec = jnp.concatenate([c, c_ctx[None]], axis=0)
    cact = jnp.pad(jax.nn.silu(cvec), ((0, 16 - (bn + 1) % 16), (0, 0))).astype(BF16)

    offs = np.cumsum([0, 512, 512, 512, 512, 128, 128, GD_QKV_W, 512, 8, 8, N_BRANCH * d])

    for l in range(depth):
        lam_init = 0.8 - 0.6 * math.exp(-0.3 * l)
        mod = _mm(cact, w_ada[l].astype(BF16))[:bn + 1] + b_ada[l]
        mods = [jnp.where(is_ctx, m[bn][None, None, :], m[:bn, None, :]) for m in jnp.split(mod, 6, axis=-1)]
        sh1, sc1, gt1, sh2, sc2, gt2 = mods

        h = (_rms(xs, norm1_g[l]) * (1.0 + sc1) + sh1).astype(BF16).reshape(n_tok, d)
        w_in_l = jnp.pad(w_in[l], ((0, 0), (0, in_pad))).astype(BF16)
        proj = _mm(h, w_in_l, tn=1536).reshape(bn, t_len, -1)
        qa, ka, va, qb, kb, vb, qkvc, zc, ac, bc, gl = [proj[..., offs[i]:offs[i + 1]] for i in range(11)]

        qg = qk_norm_g[l]
        qa = _rope(_rms(qa.reshape(bn, t_len, 8, HEAD_DIM), qg[0]), cos, sin) * (HEAD_DIM ** -0.5)
        ka = _rope(_rms(ka.reshape(bn, t_len, 8, HEAD_DIM), qg[1]), cos, sin)
        qa = qa.reshape(bn, t_len, 512).astype(BF16)
        ka = ka.reshape(bn, t_len, 512).astype(BF16)
        va = va.astype(BF16)
        lv = diff_lam[l]
        lam = (jnp.exp(jnp.sum(lv[0, 0] * lv[0, 1])) - jnp.exp(jnp.sum(lv[1, 0] * lv[1, 1])) + lam_init)
        lam = lam.reshape(1)
        sub_g = diff_subln_g[l].reshape(1, DA_VDIM)
        oa_l = _diff_attn(qa, ka, va, lam, sub_g, 1.0 - lam_init, l_len, s_len, t_len)
        oa_c = _diff_attn(qa, ka, va, lam, sub_g, 1.0 - lam_init, 0, l_len, l_len)
        oa = jnp.concatenate([oa_c, oa_l], axis=1)

        qb = _rope(_rms(qb.reshape(bn, t_len, WG_HEADS, HEAD_DIM), qg[2]), cos, sin) * (HEAD_DIM ** -0.5)
        qb = qb.reshape(bn, t_len, WG_KV_HEADS, WG_GROUP, HEAD_DIM).transpose(0, 1, 3, 2, 4)
        qb = qb.reshape(bn, t_len, 512).astype(BF16)
        kb = _rope(_rms(kb.reshape(bn, t_len, WG_KV_HEADS, HEAD_DIM), qg[3]), cos, sin)
        kb = kb.reshape(bn, t_len, 128).astype(BF16)
        vb = vb.astype(BF16)
        ob_l = _win_attn(qb, kb, vb, wg_sink[l], l_len, True)
        ob_c = _win_attn(qb, kb, vb, wg_sink[l], l_len, False)
        ob = jnp.concatenate([ob_c, ob_l], axis=1)
        ob = ob.reshape(bn, t_len, WG_GROUP, WG_KV_HEADS, HEAD_DIM).transpose(0, 1, 3, 2, 4)
        ob = ob.reshape(bn, t_len, 512)

        conv = jnp.concatenate([_conv5(qkvc[:, :l_len], gd_conv_w[l]),
                                _conv5(qkvc[:, l_len:], gd_conv_w[l])], axis=1)
        y = jax.nn.silu(conv)
        qq = _l2(y[..., :512].reshape(bn, t_len, GD_HEADS, GD_DK)) * (GD_DK ** -0.5)
        kk = _l2(y[..., 512:1024].reshape(bn, t_len, GD_HEADS, GD_DK))
        vv = y[..., 1024:]
        gg = -jnp.exp(gd_a_log[l]) * jax.nn.softplus(ac.reshape(bn, t_len, 2, GD_HEADS) + gd_dt_bias[l])
        bb = jax.nn.sigmoid(bc.reshape(bn, t_len, 2, GD_HEADS))

        def flip(a):
            return jnp.concatenate([jnp.flip(a[:, :l_len], axis=1), jnp.flip(a[:, l_len:], axis=1)], axis=1)

        def both(a):
            return jnp.stack([a, flip(a)], axis=1).reshape((bn * 2, t_len) + a.shape[2:])

        gdir = jnp.stack([gg[:, :, 0], flip(gg[:, :, 1])], axis=1).reshape(bn * 2, t_len, GD_HEADS)
        bdir = jnp.stack([bb[:, :, 0], flip(bb[:, :, 1])], axis=1).reshape(bn * 2, t_len, GD_HEADS)
        og = _gdn(both(qq.reshape(bn, t_len, 512)), both(kk.reshape(bn, t_len, 512)), both(vv), gdir, bdir)
        og = og.reshape(bn, 2, t_len, 512)
        oc = (og[:, 0] + flip(og[:, 1])).reshape(bn, t_len, GD_HEADS, GD_DV)
        oc = _rms(oc, gd_norm_g[l]) * jax.nn.silu(zc.reshape(bn, t_len, GD_HEADS, GD_DV))
        oc = oc.reshape(bn, t_len, 512)

        gate = jax.nn.sigmoid(gl + b_gate[l]).reshape(bn, t_len, N_BRANCH, d)
        merged = 0.0
        for r, o_r in enumerate((oa, ob, oc)):
            y_r = _mm(o_r.reshape(n_tok, 512).astype(BF16), w_branch[l, r].astype(BF16))
            merged = merged + gate[:, :, r] * y_r.reshape(bn, t_len, d)
        m_out = _mm(merged.reshape(n_tok, d).astype(BF16), w_out[l].astype(BF16)).reshape(bn, t_len, d)
        xs = xs + gt1 * m_out

        h2 = (_rms(xs, norm2_g[l]) * (1.0 + sc2) + sh2).astype(BF16).reshape(n_tok, d)
        pq = _mm(h2, peer_wq[l].astype(BF16))
        cnt1, e1, rk2, e2 = _peer_topk(pq, peer_keys[l])
        out_t = _peer_dense(h2.T, peer_u[l].astype(BF16), peer_v[l].astype(BF16).T, cnt1, e1, rk2, e2)
        xs = xs + gt2 * out_t.T.reshape(bn, t_len, d)

    return xs[:, l_len:]
```

```python
import functools
import math

import jax
import jax.numpy as jnp
from jax import lax
from jax.experimental import pallas as pl
from jax.experimental.pallas import tpu as pltpu

F32 = jnp.float32
BF16 = jnp.bfloat16

EPS = 1e-6
GRID_W = 64
HEAD_DIM = 64
ROPE_THETA = 10000.0
ROPE_PAIRS = HEAD_DIM // 4
DA_HEADS = 4
DA_VDIM = 2 * HEAD_DIM
WG_HEADS = 8
WG_KV_HEADS = 2
WG_GROUP = WG_HEADS // WG_KV_HEADS
WINDOW = 128
WG_BLOCK = 128
GD_HEADS = 4
GD_DK = 128
GD_DV = 128
GD_CONV = 5
GD_CHUNK = 64
GD_QKV_W = GD_HEADS * (2 * GD_DK + GD_DV)
N_BRANCH = 3
BRANCH_W = 512
PEER_HEADS = 8
PEER_HALF = 128
N_KEYS = 128
PEER_TOPK = 16

LANES = 128
SUBLANES = 8
VMEM_LIMIT = 56 * 1024 * 1024
NEG = -1e30
LOG2E = math.log2(math.e)

C_QA, C_KA, C_VA, C_QKVC, C_ZC, C_QB, C_KB, C_VB, C_AB, C_GL = (
    0, 512, 1024, 1536, 3072, 3584, 4096, 4224, 4352, 4608)
PROJ_W = C_GL + N_BRANCH * 1024
PREP_W = C_GL
CONV_BLK = 1536


def _pick(n, cands):
    for c in cands:
        if n % c == 0:
            return c
    raise ValueError(f"no tile in {cands} divides {n}")


def _params(sem):
    return pltpu.CompilerParams(dimension_semantics=sem, vmem_limit_bytes=VMEM_LIMIT)


def _split2(a):
    hi = a.astype(BF16)
    return hi, (a - hi.astype(F32)).astype(BF16)


def _dot1(a, b, dn=None):
    a = a.astype(BF16)
    b = b.astype(BF16)
    if dn is None:
        return jnp.dot(a, b, preferred_element_type=F32)
    return lax.dot_general(a, b, dn, preferred_element_type=F32)


def _dot3(a, b, dn=None):
    ah, al = _split2(a)
    bh, bl = _split2(b)
    return _dot1(ah, bh, dn) + (_dot1(ah, bl, dn) + _dot1(al, bh, dn))


_NT = (((1,), (1,)), ((), ()))


def _row_sel(i, tpb, lt, bn):
    return jnp.where(i % tpb < lt, bn, i // tpb)


def _mm_body(x_ref, w_ref, o_ref):
    o_ref[...] = jnp.dot(x_ref[...], w_ref[...], preferred_element_type=F32)


def _mm(x, w):
    m, k = x.shape
    n = w.shape[1]
    tn = _pick(n, (1536, 1024, 512, 256, 128))
    return pl.pallas_call(
        _mm_body,
        grid=(n // tn,),
        in_specs=[pl.BlockSpec((m, k), lambda j: (0, 0)),
                  pl.BlockSpec((k, tn), lambda j: (0, j))],
        out_specs=pl.BlockSpec((m, tn), lambda j: (0, j)),
        out_shape=jax.ShapeDtypeStruct((m, n), F32),
        compiler_params=_params(("parallel",)),
        name="ada_mm",
    )(x, w)


def _modulate(x, g, sh, sc):
    h = x * lax.rsqrt(jnp.mean(x * x, axis=-1, keepdims=True) + EPS) * g
    return h * (1.0 + sc) + sh


def _inproj_body(x_ref, g_ref, sh_ref, sc_ref, w_ref, o_ref):
    h = _modulate(x_ref[...], g_ref[...], sh_ref[0], sc_ref[0])
    o_ref[...] = jnp.dot(h.astype(BF16), w_ref[...], preferred_element_type=F32)


def _inproj(xs, g, mod, w, bn, t_len, l_len):
    n, d = xs.shape
    tm = _pick(math.gcd(l_len, t_len), (512, 256, 128))
    tn = CONV_BLK
    tpb, lt = t_len // tm, l_len // tm
    sel = lambda j, i: _row_sel(i, tpb, lt, bn)
    return pl.pallas_call(
        _inproj_body,
        grid=(w.shape[1] // tn, n // tm),
        in_specs=[pl.BlockSpec((tm, d), lambda j, i: (i, 0)),
                  pl.BlockSpec((1, d), lambda j, i: (0, 0)),
                  pl.BlockSpec((1, 1, d), lambda j, i: (sel(j, i) * 6, 0, 0)),
                  pl.BlockSpec((1, 1, d), lambda j, i: (sel(j, i) * 6 + 1, 0, 0)),
                  pl.BlockSpec((d, tn), lambda j, i: (0, j))],
        out_specs=pl.BlockSpec((tm, tn), lambda j, i: (i, j)),
        out_shape=jax.ShapeDtypeStruct((n, w.shape[1]), F32),
        compiler_params=_params(("parallel", "parallel")),
        name="inproj",
    )(xs, g, mod, mod, w)


def _prep_body(p_ref, prev_ref, next_ref, cos_ref, sa_ref, sb_ref, qkg_ref, cw_ref, alog_ref, dtb_ref,
               qa_ref, ka_ref, va_ref, qb_ref, kb_ref, vb_ref, gq_ref, gk_ref, gv_ref, gb_ref,
               xe_ref, *, tpb, lt):
    tm = p_ref.shape[0]
    li = lax.broadcasted_iota(jnp.int32, (LANES, LANES), 0)
    lj = lax.broadcasted_iota(jnp.int32, (LANES, LANES), 1)
    seg = jnp.where(li // HEAD_DIM == lj // HEAD_DIM, 1.0 / HEAD_DIM, 0.0).astype(BF16)
    cos = cos_ref[...]
    sa = sa_ref[...]
    sb = sb_ref[...]

    def normrope(x, gain):
        yh, yl = _split2(x * x)
        ms = jnp.dot(yh, seg, preferred_element_type=F32) + jnp.dot(yl, seg, preferred_element_type=F32)
        xn = x * lax.rsqrt(ms + EPS) * gain
        return xn * cos + pltpu.roll(xn, LANES - ROPE_PAIRS, 1) * sa + pltpu.roll(xn, ROPE_PAIRS, 1) * sb

    scale = HEAD_DIM ** -0.5 * LOG2E
    for c in range(4):
        cs = slice(c * LANES, (c + 1) * LANES)
        qa_ref[:, cs] = (normrope(p_ref[:, C_QA + c * LANES:C_QA + (c + 1) * LANES], qkg_ref[0:1]) * scale).astype(BF16)
        ka_ref[:, cs] = normrope(p_ref[:, C_KA + c * LANES:C_KA + (c + 1) * LANES], qkg_ref[1:2]).astype(BF16)
        qb_ref[:, cs] = (normrope(p_ref[:, C_QB + c * LANES:C_QB + (c + 1) * LANES], qkg_ref[2:3]) * scale).astype(BF16)
    kb_ref[...] = normrope(p_ref[:, C_KB:C_KB + LANES], qkg_ref[3:4]).astype(BF16)
    va_ref[...] = p_ref[:, C_VA:C_VA + 512].astype(BF16)
    vb_ref[...] = p_ref[:, C_VB:C_VB + LANES].astype(BF16)

    ti = pl.program_id(0) % tpb
    at_start = (ti == 0) | (ti == lt)
    at_end = (ti == lt - 1) | (ti == tpb - 1)
    xe_ref[0:SUBLANES] = jnp.where(at_start, 0.0, prev_ref[...])
    xe_ref[SUBLANES:SUBLANES + tm] = p_ref[:, C_QKVC:C_QKVC + GD_QKV_W]
    xe_ref[SUBLANES + tm:2 * SUBLANES + tm] = jnp.where(at_end, 0.0, next_ref[...])
    half = GD_CONV // 2
    y = None
    for i in range(GD_CONV):
        t = xe_ref[pl.ds(SUBLANES - half + i, tm), :] * cw_ref[i:i + 1]
        y = t if y is None else y + t
    y = y * jax.nn.sigmoid(y)
    for h in range(GD_HEADS):
        hs = slice(h * LANES, (h + 1) * LANES)
        q = y[:, h * LANES:(h + 1) * LANES]
        k = y[:, 512 + h * LANES:512 + (h + 1) * LANES]
        gq_ref[:, hs] = q * lax.rsqrt(jnp.sum(q * q, axis=-1, keepdims=True) + EPS) * (GD_DK ** -0.5)
        gk_ref[:, hs] = k * lax.rsqrt(jnp.sum(k * k, axis=-1, keepdims=True) + EPS)
    gv_ref[...] = y[:, 1024:]

    ab = p_ref[:, C_AB:C_AB + LANES]
    lane = lax.broadcasted_iota(jnp.int32, ab.shape, 1)
    gdec = -jnp.exp(alog_ref[...]) * jnp.logaddexp(ab + dtb_ref[...], 0.0)
    gb_ref[...] = jnp.where(lane < 2 * GD_HEADS, gdec, jax.nn.sigmoid(ab))


def _prep(proj, cos, sa, sb, qkg, cw, alog, dtb, t_len, l_len):
    n = proj.shape[0]
    tm = _pick(math.gcd(l_len, t_len), (256, 128))
    tpb, lt = t_len // tm, l_len // tm
    r8 = tm // SUBLANES
    nb8 = n // SUBLANES
    tok = lambda w: pl.BlockSpec((tm, w), lambda i: (i, 0))
    rope = pl.BlockSpec((tm, LANES), lambda i: (i % tpb, 0))
    full = lambda a: pl.BlockSpec(a.shape, lambda i: (0,) * a.ndim)
    shp = lambda w, dt: jax.ShapeDtypeStruct((n, w), dt)
    return pl.pallas_call(
        functools.partial(_prep_body, tpb=tpb, lt=lt),
        grid=(n // tm,),
        in_specs=[pl.BlockSpec((tm, PREP_W), lambda i: (i, 0)),
                  pl.BlockSpec((SUBLANES, CONV_BLK), lambda i: (jnp.maximum(i * r8 - 1, 0), 1)),
                  pl.BlockSpec((SUBLANES, CONV_BLK), lambda i: (jnp.minimum((i + 1) * r8, nb8 - 1), 1)),
                  rope, rope, rope, full(qkg), full(cw), full(alog), full(dtb)],
        out_specs=[tok(512), tok(512), tok(512), tok(512), tok(LANES), tok(LANES),
                   tok(512), tok(512), tok(512), tok(LANES)],
        out_shape=[shp(512, BF16), shp(512, BF16), shp(512, BF16), shp(512, BF16), shp(LANES, BF16),
                   shp(LANES, BF16), shp(512, F32), shp(512, F32), shp(512, F32), shp(LANES, F32)],
        scratch_shapes=[pltpu.VMEM((tm + 2 * SUBLANES, GD_QKV_W), F32)],
        compiler_params=_params(("parallel",)),
        name="prep",
    )(proj, proj, proj, cos, sa, sb, qkg, cw, alog, dtb)


def _diff_body(lam_ref, q_ref, k_ref, v_ref, g_ref, o_ref, *, post, l_len, n_ctx):
    q = q_ref[0]
    lane = lax.broadcasted_iota(jnp.int32, q.shape, 1)
    zero = jnp.zeros_like(q)
    q1 = jnp.where(lane < HEAD_DIM, q, zero)
    q2 = jnp.where(lane >= HEAD_DIM, q, zero)

    def run(k, v):
        def attend(s):
            p = jnp.exp2(s - jnp.max(s, axis=-1, keepdims=True))
            l = jnp.sum(p, axis=-1, keepdims=True)
            return jnp.dot(p.astype(BF16), v, preferred_element_type=F32) * (1.0 / l)

        s1 = lax.dot_general(q1, k, _NT, preferred_element_type=F32)
        s2 = lax.dot_general(q2, k, _NT, preferred_element_type=F32)
        o = attend(s1) - lam_ref[0] * attend(s2)
        ms = jnp.mean(o * o, axis=-1, keepdims=True)
        o_ref[0] = o * lax.rsqrt(ms + EPS) * (g_ref[...] * post)

    i = pl.program_id(2)

    @pl.when(i < n_ctx)
    def _():
        run(k_ref[0, :l_len], v_ref[0, :l_len])

    @pl.when(i >= n_ctx)
    def _():
        run(k_ref[0], v_ref[0])


def _diff_attn(q, k, v, lam, subln_g, post, l_len):
    b, t, _ = q.shape
    tq = _pick(math.gcd(l_len, t), (256, 128))
    return pl.pallas_call(
        functools.partial(_diff_body, post=post, l_len=l_len, n_ctx=l_len // tq),
        grid=(b, DA_HEADS, t // tq),
        in_specs=[pl.BlockSpec(memory_space=pltpu.SMEM),
                  pl.BlockSpec((1, tq, LANES), lambda bi, h, i: (bi, i, h)),
                  pl.BlockSpec((1, t, LANES), lambda bi, h, i: (bi, 0, h)),
                  pl.BlockSpec((1, t, LANES), lambda bi, h, i: (bi, 0, h)),
                  pl.BlockSpec((1, LANES), lambda bi, h, i: (0, 0))],
        out_specs=pl.BlockSpec((1, tq, LANES), lambda bi, h, i: (bi, i, h)),
        out_shape=jax.ShapeDtypeStruct((b, t, DA_HEADS * DA_VDIM), F32),
        compiler_params=_params(("parallel", "parallel", "parallel")),
        name="diff_attn",
    )(lam, q, k, v, subln_g)


def _win_body(sink_ref, q_ref, kc_ref, vc_ref, kp_ref, kn_ref, kx_ref, vp_ref, vn_ref, vx_ref, o_ref,
              *, lo, s_len):
    l_len = kc_ref.shape[1]
    q = q_ref[0]
    lane = lax.broadcasted_iota(jnp.int32, (WG_BLOCK, LANES), 1)

    def run(kcat, vcat, valid):
        vlane = lax.broadcasted_iota(jnp.int32, vcat.shape, 1)
        vz = jnp.zeros_like(vcat)
        vhalf = [jnp.where(vlane < HEAD_DIM, vcat, vz), jnp.where(vlane >= HEAD_DIM, vcat, vz)]
        for g in range(WG_GROUP):
            qg = q[:, g * LANES:(g + 1) * LANES]
            zq = jnp.zeros_like(qg)
            acc = None
            for kv in range(WG_KV_HEADS):
                qm = jnp.where(lane < HEAD_DIM, qg, zq) if kv == 0 else jnp.where(lane >= HEAD_DIM, qg, zq)
                s = lax.dot_general(qm, kcat, _NT, preferred_element_type=F32)
                if valid is not None:
                    s = jnp.where(valid, s, NEG)
                sk = sink_ref[kv * WG_GROUP + g] * LOG2E
                m = jnp.maximum(jnp.max(s, axis=-1, keepdims=True), sk)
                p = jnp.exp2(s - m)
                den = jnp.sum(p, axis=-1, keepdims=True) + jnp.exp2(sk - m)
                o = jnp.dot(p.astype(BF16), vhalf[kv], preferred_element_type=F32) * (1.0 / den)
                acc = o if acc is None else acc + o
            o_ref[0, :, g * LANES:(g + 1) * LANES] = acc

    n = pl.program_id(1)

    @pl.when(n < lo)
    def _():
        run(kc_ref[0], vc_ref[0], None)

    @pl.when(n >= lo)
    def _():
        kcat = jnp.concatenate([kc_ref[0], kp_ref[0], kn_ref[0], kx_ref[0]], axis=0)
        vcat = jnp.concatenate([vc_ref[0], vp_ref[0], vn_ref[0], vx_ref[0]], axis=0)
        nk = kcat.shape[0]
        col = lax.broadcasted_iota(jnp.int32, (WG_BLOCK, nk), 1)
        row = lax.broadcasted_iota(jnp.int32, (WG_BLOCK, nk), 0)
        j = col - l_len
        rel = j - WG_BLOCK - row
        kpos = (n - lo) * WG_BLOCK + j - WG_BLOCK
        valid = (col < l_len) | ((jnp.abs(rel) <= WINDOW) & (kpos >= 0) & (kpos < s_len))
        run(kcat, vcat, valid)


def _win_attn(q, k, v, sink, l_len):
    b, t, _ = q.shape
    blk = WG_BLOCK
    lo, nb = l_len // blk, t // blk
    ctx_spec = pl.BlockSpec((1, l_len, LANES), lambda bi, n: (bi, 0, 0))
    win = [pl.BlockSpec((1, blk, LANES), lambda bi, n, d=d: (bi, jnp.clip(n + d, lo, nb - 1), 0))
           for d in (-1, 0, 1)]
    return pl.pallas_call(
        functools.partial(_win_body, lo=lo, s_len=t - l_len),
        grid=(b, nb),
        in_specs=[pl.BlockSpec(memory_space=pltpu.SMEM),
                  pl.BlockSpec((1, blk, 4 * LANES), lambda bi, n: (bi, n, 0)),
                  ctx_spec, ctx_spec] + win + win,
        out_specs=pl.BlockSpec((1, blk, 4 * LANES), lambda bi, n: (bi, n, 0)),
        out_shape=jax.ShapeDtypeStruct((b, t, 4 * LANES), F32),
        compiler_params=_params(("parallel", "parallel")),
        name="win_attn",
    )(sink, q, k, v, k, k, k, v, v, v)


GD_LOCAL_CHUNKS = 4


def _dot3s(x, r):
    m = x.shape[0]
    xh, xl = _split2(x)
    rh, rl = _split2(r)
    t = jnp.dot(jnp.concatenate([xh, xl], axis=0), rh, preferred_element_type=F32)
    return t[:m] + t[m:] + jnp.dot(xh, rl, preferred_element_type=F32)


def _gdn_local_body(q_ref, k_ref, v_ref, gb_ref, u_ref, w_ref, qg_ref, qk_ref, kdt_ref, egl_ref, *, nch):
    d = pl.program_id(1)
    c = GD_CHUNK
    nh = GD_HEADS
    ii = lax.broadcasted_iota(jnp.int32, (c, nh * c), 0)
    jl = lax.broadcasted_iota(jnp.int32, (c, nh * c), 1)
    jj = jl % c
    blk = jl // c
    rel = (jj - ii) * (1 - 2 * d)
    incl = rel <= 0
    strict = rel < 0
    incl_t = rel >= 0
    eye = ii == jj
    eyef = eye.astype(F32)
    hl = lax.broadcasted_iota(jnp.int32, (c, nh * LANES), 1) // LANES
    e_i = lax.broadcasted_iota(jnp.int32, (nh * LANES, nh * LANES), 0)
    e_j = lax.broadcasted_iota(jnp.int32, (nh * LANES, nh * LANES), 1)
    eye_b = (e_i == e_j).astype(BF16)
    fwd = d == 0

    def cat(cols):
        out = jnp.zeros((c, nh * c), F32)
        for h in range(nh):
            out = jnp.where(blk == h, cols[h], out)
        return out

    def nat(cols):
        out = jnp.zeros((c, nh * LANES), F32)
        for h in range(nh):
            out = jnp.where(hl == h, cols[h], out)
        return out

    def bd_cat(p):
        return jnp.concatenate([jnp.where(blk == h, p, 0.0) for h in range(nh)], axis=0)

    def bd_nat(x):
        return jnp.concatenate([jnp.where(hl == h, x, 0.0) for h in range(nh)], axis=0)

    def setup(ch):
        rows = slice(ch * c, (ch + 1) * c)
        gb = gb_ref[0, rows, :]
        q = q_ref[0, rows, :]
        k = k_ref[0, rows, :]
        v = v_ref[0, rows, :]
        g_cols = [jnp.where(fwd, gb[:, h:h + 1], gb[:, nh + h:nh + h + 1]) for h in range(nh)]
        b_cols = [jnp.where(fwd, gb[:, 2 * nh + h:2 * nh + h + 1], gb[:, 3 * nh + h:3 * nh + h + 1])
                  for h in range(nh)]
        g_cat = cat(g_cols)
        g_row = jnp.sum(jnp.where(eye, g_cat, 0.0), axis=0, keepdims=True)
        gc_cols = [jnp.sum(jnp.where(incl & (blk == h), g_row, 0.0), axis=1, keepdims=True)
                   for h in range(nh)]
        grow = jnp.sum(jnp.where(incl_t, g_cat, 0.0), axis=0, keepdims=True)
        glast = [jnp.sum(g_cols[h], axis=0, keepdims=True) for h in range(nh)]
        decay = jnp.exp(jnp.where(incl, cat(gc_cols) - grow, NEG))
        b_nat = nat(b_cols)
        eg_nat = nat([jnp.exp(gc_cols[h]) for h in range(nh)])
        kb = k * b_nat
        kbh, kbl = _split2(kb)
        kdh, kdl = _split2(bd_nat(k))
        qg_ref[0, 0, rows, :] = q * eg_nat
        kd = (k * nat([jnp.exp(glast[h] - gc_cols[h]) for h in range(nh)])).astype(BF16)
        egl_ref[0, 0, ch] = nat([jnp.exp(glast[h]) for h in range(nh)])[0:1]
        return dict(rows=rows, ch=ch, q=q, decay=decay, kbh=kbh, kbl=kbl, kdh=kdh, kdl=kdl, kd=kd,
                    vb=v * b_nat, kbeg=kb * eg_nat)

    st = [setup(ch) for ch in range(nch)]
    kk1 = [lax.dot_general(jnp.concatenate([s["kbh"], s["kbl"]], axis=0), s["kdh"], _NT,
                           preferred_element_type=F32) for s in st]
    kk2 = [lax.dot_general(s["kbh"], s["kdl"], _NT, preferred_element_type=F32) for s in st]
    a = [jnp.where(strict, (x[:c] + x[c:] + y) * s["decay"], 0.0) for x, y, s in zip(kk1, kk2, st)]
    tinv = [eyef - x for x in a]
    p = [_dot3s(x, bd_cat(x)) for x in a]
    for _ in range(4):
        t = [_dot3s(jnp.concatenate([ti, pi], axis=0), bd_cat(pi)) for ti, pi in zip(tinv, p)]
        tinv = [ti + x[:c] for ti, x in zip(tinv, t)]
        p = [x[c:] for x in t]
    tinv = [ti + _dot3s(ti, bd_cat(pi)) for ti, pi in zip(tinv, p)]
    us = [_dot3s(ti, bd_nat(s["vb"])) for ti, s in zip(tinv, st)]
    ws = [_dot3s(ti, bd_nat(s["kbeg"])) for ti, s in zip(tinv, st)]
    qks = [lax.dot_general(s["q"].astype(BF16), s["kdh"], _NT, preferred_element_type=F32) for s in st]
    kdts = [lax.dot_general(eye_b, s["kd"], _NT, preferred_element_type=F32) for s in st]
    for s, u, w, qk, kdt in zip(st, us, ws, qks, kdts):
        u_ref[0, 0, s["rows"], :] = u
        w_ref[0, 0, s["rows"], :] = w
        qk_ref[0, 0, s["ch"]] = qk * s["decay"]
        kdt_ref[0, 0, s["ch"]] = kdt.astype(BF16)


def _gdn_scan_body(u_ref, w_ref, qg_ref, qk_ref, kdt_ref, egl_ref, o_ref, s_ref):
    @pl.when(pl.program_id(2) == 0)
    def _():
        s_ref[...] = jnp.zeros_like(s_ref)

    c = GD_CHUNK
    heads = range(GD_HEADS)
    sls = [slice(h * LANES, (h + 1) * LANES) for h in heads]
    hl = lax.broadcasted_iota(jnp.int32, (c, GD_HEADS * LANES), 1) // LANES
    ss = [s_ref[h] for h in heads]
    rs = [_dot1(jnp.concatenate([w_ref[0, 0, :, sl], qg_ref[0, 0, :, sl]], axis=0), s)
          for sl, s in zip(sls, ss)]
    v_news = [u_ref[0, 0, :, sl] - r[:c] for sl, r in zip(sls, rs)]
    upd = [_dot1(kdt_ref[0, 0, 0, sl, :], vn) for sl, vn in zip(sls, v_news)]
    v_all = jnp.concatenate(v_news, axis=1)
    vbd = jnp.concatenate([jnp.where(hl == h, v_all, 0.0) for h in heads], axis=0)
    o_ref[0, 0] = jnp.concatenate([r[c:] for r in rs], axis=1) + _dot1(qk_ref[0, 0, 0], vbd)
    for h in heads:
        s_ref[h] = ss[h] * egl_ref[0, 0, 0, :, sls[h]] + upd[h]


def _gdn(q, k, v, gb, l_len):
    b, t, _ = q.shape
    c = GD_CHUNK
    nc, ncc = t // c, l_len // c
    nch = _pick(nc, (GD_LOCAL_CHUNKS, 1))
    h = GD_HEADS
    hw = h * LANES
    tokl = pl.BlockSpec((1, nch * c, hw), lambda bi, d, j: (bi, j, 0))
    tok4 = pl.BlockSpec((1, 1, nch * c, hw), lambda bi, d, j: (bi, d, j, 0))
    ch5 = lambda r, w2: pl.BlockSpec((1, 1, nch, r, w2), lambda bi, d, j: (bi, d, j, 0, 0))
    tok_shape = jax.ShapeDtypeStruct((b, 2, t, hw), F32)
    u, w, qg, qk, kdt, egl = pl.pallas_call(
        functools.partial(_gdn_local_body, nch=nch),
        grid=(b, 2, nc // nch),
        in_specs=[tokl, tokl, tokl, pl.BlockSpec((1, nch * c, LANES), lambda bi, d, j: (bi, j, 0))],
        out_specs=[tok4, tok4, tok4, ch5(c, h * c), ch5(hw, c), ch5(1, hw)],
        out_shape=[tok_shape, tok_shape, tok_shape,
                   jax.ShapeDtypeStruct((b, 2, nc, c, h * c), F32),
                   jax.ShapeDtypeStruct((b, 2, nc, hw, c), BF16),
                   jax.ShapeDtypeStruct((b, 2, nc, 1, hw), F32)],
        compiler_params=_params(("parallel", "parallel", "parallel")),
        name="gdn_local",
    )(q, k, v, gb)

    def chunk(d, j):
        back = jnp.where(j < ncc, ncc - 1 - j, nc - 1 - j + ncc)
        return jnp.where(d == 0, j, back)

    stok = pl.BlockSpec((1, 1, c, hw), lambda bi, d, j: (bi, d, chunk(d, j), 0))
    sch = lambda r, w2: pl.BlockSpec((1, 1, 1, r, w2), lambda bi, d, j: (bi, d, chunk(d, j), 0, 0))
    return pl.pallas_call(
        _gdn_scan_body,
        grid=(b, 2, nc),
        in_specs=[stok, stok, stok, sch(c, h * c), sch(hw, c), sch(1, hw)],
        out_specs=stok,
        out_shape=tok_shape,
        scratch_shapes=[pltpu.VMEM((h, GD_DK, GD_DV), F32)],
        compiler_params=_params(("parallel", "parallel", "arbitrary")),
        name="gdn_scan",
    )(u, w, qg, qk, kdt, egl)


def _merge_body(x_ref, oa_ref, ob_ref, og0_ref, og1_ref, z_ref, gl0_ref, gl1_ref, bg_ref, ng_ref,
                gt_ref, wb_ref, wo_ref, o_ref):
    d = x_ref.shape[1]
    parts = []
    for h in range(GD_HEADS):
        sl = slice(h * LANES, (h + 1) * LANES)
        o = og0_ref[0, 0, :, sl] + og1_ref[0, 0, :, sl]
        z = z_ref[:, sl]
        o = o * lax.rsqrt(jnp.mean(o * o, axis=-1, keepdims=True) + EPS) * ng_ref[...]
        parts.append(o * (z * jax.nn.sigmoid(z)))
    oc = jnp.concatenate(parts, axis=1)
    gl = jnp.concatenate([gl0_ref[...], gl1_ref[...]], axis=1)
    merged = None
    for r, o_r in enumerate((oa_ref[...], ob_ref[...], oc)):
        y = jnp.dot(o_r.astype(BF16), wb_ref[r], preferred_element_type=F32)
        t = jax.nn.sigmoid(gl[:, r * d:(r + 1) * d] + bg_ref[:, r * d:(r + 1) * d]) * y
        merged = t if merged is None else merged + t
    m = jnp.dot(merged.astype(BF16), wo_ref[...], preferred_element_type=F32)
    o_ref[...] = x_ref[...] + gt_ref[0] * m


def _merge(xs, oa, ob, og, proj, b_gate, norm_g, mod, wb, wo, bn, t_len, l_len):
    n, d = xs.shape
    tm = _pick(math.gcd(l_len, t_len), (256, 128))
    tpb, lt = t_len // tm, l_len // tm
    tok = lambda w: pl.BlockSpec((tm, w), lambda i: (i, 0))
    og_spec = lambda dd: pl.BlockSpec((1, 1, tm, BRANCH_W), lambda i: (i // tpb, dd, i % tpb, 0))
    full = lambda a: pl.BlockSpec(a.shape, lambda i: (0,) * a.ndim)
    return pl.pallas_call(
        _merge_body,
        grid=(n // tm,),
        in_specs=[tok(d), tok(BRANCH_W), tok(BRANCH_W), og_spec(0), og_spec(1),
                  pl.BlockSpec((tm, BRANCH_W), lambda i: (i, C_ZC // BRANCH_W)),
                  pl.BlockSpec((tm, CONV_BLK), lambda i: (i, C_GL // CONV_BLK)),
                  pl.BlockSpec((tm, CONV_BLK), lambda i: (i, C_GL // CONV_BLK + 1)),
                  full(b_gate), full(norm_g),
                  pl.BlockSpec((1, 1, d), lambda i: (_row_sel(i, tpb, lt, bn) * 6 + 2, 0, 0)),
                  full(wb), full(wo)],
        out_specs=tok(d),
        out_shape=jax.ShapeDtypeStruct((n, d), F32),
        compiler_params=_params(("parallel",)),
        name="merge",
    )(xs, oa, ob, og, og, proj, proj, proj, b_gate, norm_g, mod, wb, wo)


_CAND = [(r1, r2) for r1 in range(1, PEER_TOPK + 2) for r2 in range(1, PEER_TOPK + 2)
         if r1 * r2 <= PEER_TOPK + 1]


def _extract_top(cur, n):
    rank = jnp.full(cur.shape, 99.0, F32)
    vals = []
    for r in range(n):
        m = jnp.max(cur, axis=0, keepdims=True)
        hit = cur == m
        rank = jnp.where(hit, float(r + 1), rank)
        cur = jnp.where(hit, -jnp.inf, cur)
        vals.append(m)
    return vals, rank


def _topk_body(x_ref, g_ref, sh_ref, sc_ref, wq_ref, keys_ref, h_ref, cnt1_ref, e1_ref, rk2_ref, e2_ref):
    hmod = _modulate(x_ref[...], g_ref[...], sh_ref[0], sc_ref[0]).astype(BF16)
    h_ref[...] = hmod
    nr = PEER_TOPK + 1
    for h in range(PEER_HEADS):
        st = []
        for c in range(2):
            col = (2 * h + c) * LANES
            qs = jnp.dot(hmod, wq_ref[:, col:col + LANES], preferred_element_type=F32)
            st.append(_dot3(keys_ref[c], qs, _NT))
        a, rank1 = _extract_top(st[0], nr)
        b, rank2 = _extract_top(st[1], nr)
        rows = [a[r1 - 1] + b[r2 - 1] for r1, r2 in _CAND]
        pad = (-len(rows)) % SUBLANES
        cand = jnp.concatenate(rows + [jnp.full_like(rows[0], -jnp.inf)] * pad, axis=0)
        top, _ = _extract_top(cand, nr)
        tau = 0.5 * (top[PEER_TOPK - 1] + top[PEER_TOPK])
        z = jnp.sum(jnp.where(cand >= tau, jnp.exp(cand - top[0]), 0.0), axis=0, keepdims=True)
        cnt1 = jnp.zeros_like(rank1)
        for r1 in range(1, PEER_TOPK + 1):
            cnt = None
            for r2 in range(1, PEER_TOPK + 1):
                if r1 * r2 <= PEER_TOPK + 1:
                    t = (a[r1 - 1] + b[r2 - 1] >= tau).astype(F32)
                    cnt = t if cnt is None else cnt + t
            cnt1 = jnp.where(rank1 == float(r1), cnt, cnt1)
        cnt1_ref[h] = cnt1
        e1_ref[h] = jnp.exp(st[0] - a[0])
        rk2_ref[h] = rank2.astype(BF16)
        e2_ref[h] = (jnp.exp(st[1] - b[0]) * (1.0 / z)).astype(BF16)


def _peer_topk(xs, g, mod, wq, keys, bn, t_len, l_len):
    n, d = xs.shape
    tm = _pick(math.gcd(l_len, t_len), (256, 128))
    tpb, lt = t_len // tm, l_len // tm
    sel = lambda i: _row_sel(i, tpb, lt, bn)
    out = pl.BlockSpec((PEER_HEADS, N_KEYS, tm), lambda i: (0, 0, i))
    shp = lambda dt: jax.ShapeDtypeStruct((PEER_HEADS, N_KEYS, n), dt)
    return pl.pallas_call(
        _topk_body,
        grid=(n // tm,),
        in_specs=[pl.BlockSpec((tm, d), lambda i: (i, 0)),
                  pl.BlockSpec((1, d), lambda i: (0, 0)),
                  pl.BlockSpec((1, 1, d), lambda i: (sel(i) * 6 + 3, 0, 0)),
                  pl.BlockSpec((1, 1, d), lambda i: (sel(i) * 6 + 4, 0, 0)),
                  pl.BlockSpec(wq.shape, lambda i: (0, 0)),
                  pl.BlockSpec(keys.shape, lambda i: (0, 0, 0))],
        out_specs=[pl.BlockSpec((tm, d), lambda i: (i, 0)), out, out, out, out],
        out_shape=[jax.ShapeDtypeStruct((n, d), BF16), shp(F32), shp(F32), shp(BF16), shp(BF16)],
        compiler_params=_params(("parallel",)),
        name="peer_topk",
    )(xs, g, mod, mod, wq, keys)


def _dense_body(x_ref, h_ref, u_ref, vt_ref, cnt1_ref, e1_ref, rk2_ref, e2_ref, *rest, te):
    gt_refs, o_ref, acc_ref = rest[:-2], rest[-2], rest[-1]
    e = pl.program_id(1)

    @pl.when(e == 0)
    def _():
        acc_ref[...] = jnp.zeros_like(acc_ref)

    act = lax.dot_general(u_ref[...], h_ref[...], _NT, preferred_element_type=F32)
    act = (0.5 * act * (1.0 + lax.erf(act * math.sqrt(0.5)))).astype(BF16)
    zero = jnp.zeros((N_KEYS, act.shape[1]), BF16)
    parts = []
    for j in range(te // N_KEYS):
        i1 = e * (te // N_KEYS) + j
        g = zero
        for h in range(PEER_HEADS):
            cnt = cnt1_ref[h, pl.ds(i1, 1), :].astype(BF16)
            w1 = e1_ref[h, pl.ds(i1, 1), :].astype(BF16)
            g = g + jnp.where(rk2_ref[h] <= cnt, e2_ref[h], zero) * w1
        parts.append(g)
    gate = jnp.concatenate(parts, axis=0)
    acc_ref[...] += jnp.dot(vt_ref[...], act * gate, preferred_element_type=F32)

    @pl.when(e == pl.num_programs(1) - 1)
    def _():
        base = acc_ref.shape[1] // len(gt_refs)
        for k, gt_ref in enumerate(gt_refs):
            rs = slice(k * base, (k + 1) * base)
            o_ref[rs] = x_ref[rs] + gt_ref[0] * acc_ref[:, rs].T


def _peer_dense(xs, hmod, u, vt, cnt1, e1, rk2, e2, mod, bn, t_len, l_len):
    n, d = xs.shape
    ne = u.shape[0]
    base = _pick(math.gcd(l_len, t_len), (256, 128))
    nsub = 2 if n % (2 * base) == 0 else 1
    tm = nsub * base
    te = _pick(ne, (1024, 512))
    tpb, lt = t_len // base, l_len // base
    gts = [pl.BlockSpec((1, 1, d), lambda i, e, k=k: (_row_sel(i * nsub + k, tpb, lt, bn) * 6 + 5, 0, 0))
           for k in range(nsub)]
    sel = pl.BlockSpec((PEER_HEADS, N_KEYS, tm), lambda i, e: (0, 0, i))
    tok = pl.BlockSpec((tm, d), lambda i, e: (i, 0))
    tab = pl.BlockSpec((te, d), lambda i, e: (e, 0))
    return pl.pallas_call(
        functools.partial(_dense_body, te=te),
        grid=(n // tm, ne // te),
        in_specs=[tok, tok, tab, pl.BlockSpec((d, te), lambda i, e: (0, e)), sel, sel, sel, sel] + gts,
        out_specs=tok,
        out_shape=jax.ShapeDtypeStruct((n, d), F32),
        scratch_shapes=[pltpu.VMEM((d, tm), F32)],
        compiler_params=_params(("parallel", "arbitrary")),
        name="peer_dense",
    )(xs, hmod, u, vt, cnt1, e1, rk2, e2, *([mod] * nsub))


def _rope_tables(rows, l_len):
    row = jnp.repeat(jnp.arange(rows), GRID_W).astype(F32)
    col = jnp.tile(jnp.arange(GRID_W), rows).astype(F32)
    inv = jnp.power(ROPE_THETA, -jnp.arange(ROPE_PAIRS, dtype=F32) / ROPE_PAIRS)
    ar = row[:, None] * inv
    ac = col[:, None] * inv
    ang = jnp.concatenate([ar, ar, ac, ac], axis=-1)
    cos = jnp.concatenate([jnp.ones((l_len, HEAD_DIM), F32), jnp.cos(ang)], axis=0)
    sin = jnp.concatenate([jnp.zeros((l_len, HEAD_DIM), F32), jnp.sin(ang)], axis=0)
    cos = jnp.tile(cos, (1, 2))
    sin = jnp.tile(sin, (1, 2))
    first = (jnp.arange(LANES) % (2 * ROPE_PAIRS)) < ROPE_PAIRS
    return cos, jnp.where(first, -sin, 0.0), jnp.where(first, 0.0, sin)


def _reorder_w_in(w):
    d = w.shape[0]
    qa_ka_va, qb, kb_vb = w[:, :1536], w[:, 1536:2048], w[:, 2048:2304]
    qkvc, zc, ab, gl = w[:, 2304:3840], w[:, 3840:4352], w[:, 4352:4368], w[:, 4368:]
    qb = qb.reshape(d, WG_KV_HEADS, WG_GROUP, HEAD_DIM).transpose(0, 2, 1, 3).reshape(d, 512)
    pad = jnp.zeros((d, C_GL - C_AB - ab.shape[1]), w.dtype)
    return jnp.concatenate([qa_ka_va, qkvc, zc, qb, kb_vb, ab, pad, gl], axis=1).astype(BF16)


def kernel(x, c, ctx, c_ctx, norm1_g, norm2_g, w_ada, b_ada, w_in, b_gate, qk_norm_g, diff_lam,
           diff_subln_g, wg_sink, gd_conv_w, gd_a_log, gd_dt_bias, gd_norm_g, w_branch, w_out,
           peer_wq, peer_keys, peer_u, peer_v):
    bn, s_len, d = x.shape
    l_len = ctx.shape[1]
    t_len = l_len + s_len
    n_tok = bn * t_len
    depth = w_in.shape[0]
    cos, sa, sb = _rope_tables(s_len // GRID_W, l_len)

    xs = jnp.concatenate([ctx, x], axis=1).reshape(n_tok, d)
    cvec = jnp.concatenate([c, c_ctx[None]], axis=0)
    cact = jnp.pad(jax.nn.silu(cvec), ((0, 16 - (bn + 1) % 16), (0, 0))).astype(BF16)
    lane_pad = lambda a: jnp.pad(a.reshape(1, -1), ((0, 0), (0, LANES - a.size)))

    for l in range(depth):
        lam_init = 0.8 - 0.6 * math.exp(-0.3 * l)
        mod = (_mm(cact, w_ada[l].astype(BF16))[:bn + 1] + b_ada[l]).reshape((bn + 1) * 6, 1, d)

        proj = _inproj(xs, norm1_g[l].reshape(1, d), mod, _reorder_w_in(w_in[l]), bn, t_len, l_len)
        qa, ka, va, qb, kb, vb, gq, gk, gv, gb = _prep(
            proj, cos, sa, sb, jnp.tile(qk_norm_g[l], (1, 2)), gd_conv_w[l],
            lane_pad(gd_a_log[l]), lane_pad(gd_dt_bias[l]), t_len, l_len)
        b3 = lambda a: a.reshape(bn, t_len, a.shape[-1])

        lv = diff_lam[l]
        lam = (jnp.exp(jnp.sum(lv[0, 0] * lv[0, 1])) - jnp.exp(jnp.sum(lv[1, 0] * lv[1, 1])) + lam_init)
        oa = _diff_attn(b3(qa), b3(ka), b3(va), lam.reshape(1), diff_subln_g[l].reshape(1, DA_VDIM),
                        1.0 - lam_init, l_len)
        ob = _win_attn(b3(qb), b3(kb), b3(vb), wg_sink[l], l_len)
        og = _gdn(b3(gq), b3(gk), b3(gv), b3(gb), l_len)

        wb = w_branch[l]
        wb1 = wb[1].reshape(WG_KV_HEADS, WG_GROUP, HEAD_DIM, d).transpose(1, 0, 2, 3).reshape(BRANCH_W, d)
        wb = jnp.stack([wb[0], wb1, wb[2]]).astype(BF16)
        xs = _merge(xs, oa.reshape(n_tok, BRANCH_W), ob.reshape(n_tok, BRANCH_W), og, proj,
                    b_gate[l].reshape(1, -1), gd_norm_g[l].reshape(1, GD_DV), mod, wb,
                    w_out[l].astype(BF16), bn, t_len, l_len)

        hmod, cnt1, e1, rk2, e2 = _peer_topk(xs, norm2_g[l].reshape(1, d), mod, peer_wq[l].astype(BF16),
                                             peer_keys[l], bn, t_len, l_len)
        xs = _peer_dense(xs, hmod, peer_u[l].astype(BF16), peer_v[l].astype(BF16).T, cnt1, e1, rk2, e2,
                         mod, bn, t_len, l_len)

    return xs.reshape(bn, t_len, d)[:, l_len:]
```

```python
import functools
import math

import jax
import jax.numpy as jnp
from jax import lax
from jax.experimental import pallas as pl
from jax.experimental.pallas import tpu as pltpu

F32 = jnp.float32
BF16 = jnp.bfloat16

EPS = 1e-6
GRID_W = 64
HEAD_DIM = 64
ROPE_THETA = 10000.0
ROPE_PAIRS = HEAD_DIM // 4
DA_HEADS = 4
DA_VDIM = 2 * HEAD_DIM
WG_HEADS = 8
WG_KV_HEADS = 2
WG_GROUP = WG_HEADS // WG_KV_HEADS
WINDOW = 128
WG_BLOCK = 128
GD_HEADS = 4
GD_DK = 128
GD_DV = 128
GD_CONV = 5
GD_CHUNK = 64
GD_QKV_W = GD_HEADS * (2 * GD_DK + GD_DV)
N_BRANCH = 3
BRANCH_W = 512
PEER_HEADS = 8
PEER_HALF = 128
N_KEYS = 128
PEER_TOPK = 16

LANES = 128
SUBLANES = 8
VMEM_LIMIT = 56 * 1024 * 1024
NEG = -1e30
LOG2E = math.log2(math.e)

C_QA, C_KA, C_VA, C_QKVC, C_ZC, C_QB, C_KB, C_VB, C_AB, C_GL = (
    0, 512, 1024, 1536, 3072, 3584, 4096, 4224, 4352, 4608)
PROJ_W = C_GL + N_BRANCH * 1024
PREP_W = C_GL
CONV_BLK = 1536


def _pick(n, cands):
    for c in cands:
        if n % c == 0:
            return c
    raise ValueError(f"no tile in {cands} divides {n}")


def _params(sem):
    return pltpu.CompilerParams(dimension_semantics=sem, vmem_limit_bytes=VMEM_LIMIT)


def _split2(a):
    hi = a.astype(BF16)
    return hi, (a - hi.astype(F32)).astype(BF16)


def _dot1(a, b, dn=None):
    a = a.astype(BF16)
    b = b.astype(BF16)
    if dn is None:
        return jnp.dot(a, b, preferred_element_type=F32)
    return lax.dot_general(a, b, dn, preferred_element_type=F32)


def _dot3(a, b, dn=None):
    ah, al = _split2(a)
    bh, bl = _split2(b)
    return _dot1(ah, bh, dn) + (_dot1(ah, bl, dn) + _dot1(al, bh, dn))


_NT = (((1,), (1,)), ((), ()))


def _row_sel(i, tpb, lt, bn):
    return jnp.where(i % tpb < lt, bn, i // tpb)


def _mm_body(x_ref, w_ref, o_ref):
    o_ref[...] = jnp.dot(x_ref[...], w_ref[...], preferred_element_type=F32)


def _mm(x, w):
    m, k = x.shape
    n = w.shape[1]
    tn = _pick(n, (1536, 1024, 512, 256, 128))
    return pl.pallas_call(
        _mm_body,
        grid=(n // tn,),
        in_specs=[pl.BlockSpec((m, k), lambda j: (0, 0)),
                  pl.BlockSpec((k, tn), lambda j: (0, j))],
        out_specs=pl.BlockSpec((m, tn), lambda j: (0, j)),
        out_shape=jax.ShapeDtypeStruct((m, n), F32),
        compiler_params=_params(("parallel",)),
        name="ada_mm",
    )(x, w)


def _modulate(x, g, sh, sc):
    h = x * lax.rsqrt(jnp.mean(x * x, axis=-1, keepdims=True) + EPS) * g
    return h * (1.0 + sc) + sh


def _inproj_body(x_ref, g_ref, sh_ref, sc_ref, w_ref, o_ref):
    h = _modulate(x_ref[...], g_ref[...], sh_ref[0], sc_ref[0])
    o_ref[...] = jnp.dot(h.astype(BF16), w_ref[...], preferred_element_type=F32)


def _inproj(xs, g, mod, w, bn, t_len, l_len):
    n, d = xs.shape
    tm = _pick(math.gcd(l_len, t_len), (512, 256, 128))
    tn = w.shape[1] // 2
    tpb, lt = t_len // tm, l_len // tm
    sel = lambda j, i: _row_sel(i, tpb, lt, bn)
    return pl.pallas_call(
        _inproj_body,
        grid=(w.shape[1] // tn, n // tm),
        in_specs=[pl.BlockSpec((tm, d), lambda j, i: (i, 0)),
                  pl.BlockSpec((1, d), lambda j, i: (0, 0)),
                  pl.BlockSpec((1, 1, d), lambda j, i: (sel(j, i) * 6, 0, 0)),
                  pl.BlockSpec((1, 1, d), lambda j, i: (sel(j, i) * 6 + 1, 0, 0)),
                  pl.BlockSpec((d, tn), lambda j, i: (0, j))],
        out_specs=pl.BlockSpec((tm, tn), lambda j, i: (i, j)),
        out_shape=jax.ShapeDtypeStruct((n, w.shape[1]), F32),
        compiler_params=_params(("parallel", "parallel")),
        name="inproj",
    )(xs, g, mod, mod, w)


def _prep_body(p_ref, prev_ref, next_ref, cos_ref, sa_ref, sb_ref, qkg_ref, cw_ref, alog_ref, dtb_ref,
               qa_ref, ka_ref, va_ref, qb_ref, kb_ref, vb_ref, gq_ref, gk_ref, gv_ref, gb_ref,
               xe_ref, *, tpb, lt):
    tm = p_ref.shape[0]
    li = lax.broadcasted_iota(jnp.int32, (LANES, LANES), 0)
    lj = lax.broadcasted_iota(jnp.int32, (LANES, LANES), 1)
    seg = jnp.where(li // HEAD_DIM == lj // HEAD_DIM, 1.0 / HEAD_DIM, 0.0).astype(BF16)
    cos = cos_ref[...]
    sa = sa_ref[...]
    sb = sb_ref[...]

    def normrope(x, gain):
        yh, yl = _split2(x * x)
        ms = jnp.dot(yh, seg, preferred_element_type=F32) + jnp.dot(yl, seg, preferred_element_type=F32)
        xn = x * lax.rsqrt(ms + EPS) * gain
        return xn * cos + pltpu.roll(xn, LANES - ROPE_PAIRS, 1) * sa + pltpu.roll(xn, ROPE_PAIRS, 1) * sb

    scale = HEAD_DIM ** -0.5 * LOG2E
    for c in range(4):
        cs = slice(c * LANES, (c + 1) * LANES)
        qa_ref[:, cs] = (normrope(p_ref[:, C_QA + c * LANES:C_QA + (c + 1) * LANES], qkg_ref[0:1]) * scale).astype(BF16)
        ka_ref[:, cs] = normrope(p_ref[:, C_KA + c * LANES:C_KA + (c + 1) * LANES], qkg_ref[1:2]).astype(BF16)
        qb_ref[:, cs] = (normrope(p_ref[:, C_QB + c * LANES:C_QB + (c + 1) * LANES], qkg_ref[2:3]) * scale).astype(BF16)
    kb_ref[...] = normrope(p_ref[:, C_KB:C_KB + LANES], qkg_ref[3:4]).astype(BF16)
    va_ref[...] = p_ref[:, C_VA:C_VA + 512].astype(BF16)
    vb_ref[...] = p_ref[:, C_VB:C_VB + LANES].astype(BF16)

    ti = pl.program_id(0) % tpb
    at_start = (ti == 0) | (ti == lt)
    at_end = (ti == lt - 1) | (ti == tpb - 1)
    xe_ref[0:SUBLANES] = jnp.where(at_start, 0.0, prev_ref[...])
    xe_ref[SUBLANES:SUBLANES + tm] = p_ref[:, C_QKVC:C_QKVC + GD_QKV_W]
    xe_ref[SUBLANES + tm:2 * SUBLANES + tm] = jnp.where(at_end, 0.0, next_ref[...])
    half = GD_CONV // 2
    y = None
    for i in range(GD_CONV):
        t = xe_ref[pl.ds(SUBLANES - half + i, tm), :] * cw_ref[i:i + 1]
        y = t if y is None else y + t
    y = y * jax.nn.sigmoid(y)
    for h in range(GD_HEADS):
        hs = slice(h * LANES, (h + 1) * LANES)
        q = y[:, h * LANES:(h + 1) * LANES]
        k = y[:, 512 + h * LANES:512 + (h + 1) * LANES]
        gq_ref[:, hs] = q * lax.rsqrt(jnp.sum(q * q, axis=-1, keepdims=True) + EPS) * (GD_DK ** -0.5)
        gk_ref[:, hs] = k * lax.rsqrt(jnp.sum(k * k, axis=-1, keepdims=True) + EPS)
    gv_ref[...] = y[:, 1024:]

    ab = p_ref[:, C_AB:C_AB + LANES]
    lane = lax.broadcasted_iota(jnp.int32, ab.shape, 1)
    gdec = -jnp.exp(alog_ref[...]) * jnp.logaddexp(ab + dtb_ref[...], 0.0)
    gb_ref[...] = jnp.where(lane < 2 * GD_HEADS, gdec, jax.nn.sigmoid(ab))


def _prep(proj, cos, sa, sb, qkg, cw, alog, dtb, t_len, l_len):
    n = proj.shape[0]
    tm = _pick(math.gcd(l_len, t_len), (256, 128))
    tpb, lt = t_len // tm, l_len // tm
    r8 = tm // SUBLANES
    nb8 = n // SUBLANES
    tok = lambda w: pl.BlockSpec((tm, w), lambda i: (i, 0))
    rope = pl.BlockSpec((tm, LANES), lambda i: (i % tpb, 0))
    full = lambda a: pl.BlockSpec(a.shape, lambda i: (0,) * a.ndim)
    shp = lambda w, dt: jax.ShapeDtypeStruct((n, w), dt)
    return pl.pallas_call(
        functools.partial(_prep_body, tpb=tpb, lt=lt),
        grid=(n // tm,),
        in_specs=[pl.BlockSpec((tm, PREP_W), lambda i: (i, 0)),
                  pl.BlockSpec((SUBLANES, CONV_BLK), lambda i: (jnp.maximum(i * r8 - 1, 0), 1)),
                  pl.BlockSpec((SUBLANES, CONV_BLK), lambda i: (jnp.minimum((i + 1) * r8, nb8 - 1), 1)),
                  rope, rope, rope, full(qkg), full(cw), full(alog), full(dtb)],
        out_specs=[tok(512), tok(512), tok(512), tok(512), tok(LANES), tok(LANES),
                   tok(512), tok(512), tok(512), tok(LANES)],
        out_shape=[shp(512, BF16), shp(512, BF16), shp(512, BF16), shp(512, BF16), shp(LANES, BF16),
                   shp(LANES, BF16), shp(512, F32), shp(512, F32), shp(512, F32), shp(LANES, F32)],
        scratch_shapes=[pltpu.VMEM((tm + 2 * SUBLANES, GD_QKV_W), F32)],
        compiler_params=_params(("parallel",)),
        name="prep",
    )(proj, proj, proj, cos, sa, sb, qkg, cw, alog, dtb)


def _diff_body(lam_ref, q_ref, k_ref, v_ref, g_ref, o_ref, *, post, l_len, n_ctx):
    q = q_ref[0]
    lane = lax.broadcasted_iota(jnp.int32, q.shape, 1)
    zero = jnp.zeros_like(q)
    q1 = jnp.where(lane < HEAD_DIM, q, zero)
    q2 = jnp.where(lane >= HEAD_DIM, q, zero)

    def run(k, v):
        def attend(s):
            p = jnp.exp2(s - jnp.max(s, axis=-1, keepdims=True))
            l = jnp.sum(p, axis=-1, keepdims=True)
            return jnp.dot(p.astype(BF16), v, preferred_element_type=F32) * (1.0 / l)

        s1 = lax.dot_general(q1, k, _NT, preferred_element_type=F32)
        s2 = lax.dot_general(q2, k, _NT, preferred_element_type=F32)
        o = attend(s1) - lam_ref[0] * attend(s2)
        ms = jnp.mean(o * o, axis=-1, keepdims=True)
        o_ref[0] = o * lax.rsqrt(ms + EPS) * (g_ref[...] * post)

    i = pl.program_id(2)

    @pl.when(i < n_ctx)
    def _():
        run(k_ref[0, :l_len], v_ref[0, :l_len])

    @pl.when(i >= n_ctx)
    def _():
        run(k_ref[0], v_ref[0])


def _diff_attn(q, k, v, lam, subln_g, post, l_len):
    b, t, _ = q.shape
    tq = _pick(math.gcd(l_len, t), (256, 128))
    return pl.pallas_call(
        functools.partial(_diff_body, post=post, l_len=l_len, n_ctx=l_len // tq),
        grid=(b, DA_HEADS, t // tq),
        in_specs=[pl.BlockSpec(memory_space=pltpu.SMEM),
                  pl.BlockSpec((1, tq, LANES), lambda bi, h, i: (bi, i, h)),
                  pl.BlockSpec((1, t, LANES), lambda bi, h, i: (bi, 0, h)),
                  pl.BlockSpec((1, t, LANES), lambda bi, h, i: (bi, 0, h)),
                  pl.BlockSpec((1, LANES), lambda bi, h, i: (0, 0))],
        out_specs=pl.BlockSpec((1, tq, LANES), lambda bi, h, i: (bi, i, h)),
        out_shape=jax.ShapeDtypeStruct((b, t, DA_HEADS * DA_VDIM), F32),
        compiler_params=_params(("parallel", "parallel", "parallel")),
        name="diff_attn",
    )(lam, q, k, v, subln_g)


def _win_body(sink_ref, q_ref, kc_ref, vc_ref, kp_ref, kn_ref, kx_ref, vp_ref, vn_ref, vx_ref, o_ref,
              *, lo, s_len):
    l_len = kc_ref.shape[1]
    q = q_ref[0]
    lane = lax.broadcasted_iota(jnp.int32, (WG_BLOCK, LANES), 1)

    def run(kcat, vcat, valid):
        vlane = lax.broadcasted_iota(jnp.int32, vcat.shape, 1)
        vz = jnp.zeros_like(vcat)
        vhalf = [jnp.where(vlane < HEAD_DIM, vcat, vz), jnp.where(vlane >= HEAD_DIM, vcat, vz)]
        heads = [(g, kv) for g in range(WG_GROUP) for kv in range(WG_KV_HEADS)]
        zq = jnp.zeros((WG_BLOCK, LANES), q.dtype)
        half = [lane < HEAD_DIM, lane >= HEAD_DIM]
        qms = [jnp.where(half[kv], q[:, g * LANES:(g + 1) * LANES], zq) for g, kv in heads]
        ss = [lax.dot_general(qm, kcat, _NT, preferred_element_type=F32) for qm in qms]
        if valid is not None:
            ss = [jnp.where(valid, s, NEG) for s in ss]
        sks = [sink_ref[kv * WG_GROUP + g] * LOG2E for g, kv in heads]
        ms = [jnp.maximum(jnp.max(s, axis=-1, keepdims=True), sk) for s, sk in zip(ss, sks)]
        ps = [jnp.exp2(s - m) for s, m in zip(ss, ms)]
        dens = [jnp.sum(p, axis=-1, keepdims=True) + jnp.exp2(sk - m) for p, sk, m in zip(ps, sks, ms)]
        os = [jnp.dot(p.astype(BF16), vhalf[kv], preferred_element_type=F32) * (1.0 / den)
              for p, den, (g, kv) in zip(ps, dens, heads)]
        for g in range(WG_GROUP):
            o_ref[0, :, g * LANES:(g + 1) * LANES] = os[2 * g] + os[2 * g + 1]

    n = pl.program_id(1)

    @pl.when(n < lo)
    def _():
        run(kc_ref[0], vc_ref[0], None)

    @pl.when(n >= lo)
    def _():
        kcat = jnp.concatenate([kc_ref[0], kp_ref[0], kn_ref[0], kx_ref[0]], axis=0)
        vcat = jnp.concatenate([vc_ref[0], vp_ref[0], vn_ref[0], vx_ref[0]], axis=0)
        nk = kcat.shape[0]
        col = lax.broadcasted_iota(jnp.int32, (WG_BLOCK, nk), 1)
        row = lax.broadcasted_iota(jnp.int32, (WG_BLOCK, nk), 0)
        j = col - l_len
        rel = j - WG_BLOCK - row
        kpos = (n - lo) * WG_BLOCK + j - WG_BLOCK
        valid = (col < l_len) | ((jnp.abs(rel) <= WINDOW) & (kpos >= 0) & (kpos < s_len))
        run(kcat, vcat, valid)


def _win_attn(q, k, v, sink, l_len):
    b, t, _ = q.shape
    blk = WG_BLOCK
    lo, nb = l_len // blk, t // blk
    ctx_spec = pl.BlockSpec((1, l_len, LANES), lambda bi, n: (bi, 0, 0))
    win = [pl.BlockSpec((1, blk, LANES), lambda bi, n, d=d: (bi, jnp.clip(n + d, lo, nb - 1), 0))
           for d in (-1, 0, 1)]
    return pl.pallas_call(
        functools.partial(_win_body, lo=lo, s_len=t - l_len),
        grid=(b, nb),
        in_specs=[pl.BlockSpec(memory_space=pltpu.SMEM),
                  pl.BlockSpec((1, blk, 4 * LANES), lambda bi, n: (bi, n, 0)),
                  ctx_spec, ctx_spec] + win + win,
        out_specs=pl.BlockSpec((1, blk, 4 * LANES), lambda bi, n: (bi, n, 0)),
        out_shape=jax.ShapeDtypeStruct((b, t, 4 * LANES), F32),
        compiler_params=_params(("parallel", "parallel")),
        name="win_attn",
    )(sink, q, k, v, k, k, k, v, v, v)


GD_LOCAL_CHUNKS = 4


def _dot3s(x, r):
    m = x.shape[0]
    xh, xl = _split2(x)
    rh, rl = _split2(r)
    t = jnp.dot(jnp.concatenate([xh, xl], axis=0), rh, preferred_element_type=F32)
    return t[:m] + t[m:] + jnp.dot(xh, rl, preferred_element_type=F32)


def _gdn_local_body(q_ref, k_ref, v_ref, gb_ref, u_ref, w_ref, qg_ref, qk_ref, kdt_ref, egl_ref, *, nch):
    d = pl.program_id(1)
    c = GD_CHUNK
    nh = GD_HEADS
    ii = lax.broadcasted_iota(jnp.int32, (c, nh * c), 0)
    jl = lax.broadcasted_iota(jnp.int32, (c, nh * c), 1)
    jj = jl % c
    blk = jl // c
    rel = (jj - ii) * (1 - 2 * d)
    incl = rel <= 0
    strict = rel < 0
    incl_t = rel >= 0
    eye = ii == jj
    eyef = eye.astype(F32)
    hl = lax.broadcasted_iota(jnp.int32, (c, nh * LANES), 1) // LANES
    e_i = lax.broadcasted_iota(jnp.int32, (nh * LANES, nh * LANES), 0)
    e_j = lax.broadcasted_iota(jnp.int32, (nh * LANES, nh * LANES), 1)
    eye_b = (e_i == e_j).astype(BF16)
    fwd = d == 0

    def cat(cols):
        out = jnp.zeros((c, nh * c), F32)
        for h in range(nh):
            out = jnp.where(blk == h, cols[h], out)
        return out

    def nat(cols):
        out = jnp.zeros((c, nh * LANES), F32)
        for h in range(nh):
            out = jnp.where(hl == h, cols[h], out)
        return out

    def bd_cat(p):
        return jnp.concatenate([jnp.where(blk == h, p, 0.0) for h in range(nh)], axis=0)

    def bd_nat(x):
        return jnp.concatenate([jnp.where(hl == h, x, 0.0) for h in range(nh)], axis=0)

    def setup(ch):
        rows = slice(ch * c, (ch + 1) * c)
        gb = gb_ref[0, rows, :]
        q = q_ref[0, rows, :]
        k = k_ref[0, rows, :]
        v = v_ref[0, rows, :]
        g_cols = [jnp.where(fwd, gb[:, h:h + 1], gb[:, nh + h:nh + h + 1]) for h in range(nh)]
        b_cols = [jnp.where(fwd, gb[:, 2 * nh + h:2 * nh + h + 1], gb[:, 3 * nh + h:3 * nh + h + 1])
                  for h in range(nh)]
        g_cat = cat(g_cols)
        g_row = jnp.sum(jnp.where(eye, g_cat, 0.0), axis=0, keepdims=True)
        gc_cols = [jnp.sum(jnp.where(incl & (blk == h), g_row, 0.0), axis=1, keepdims=True)
                   for h in range(nh)]
        grow = jnp.sum(jnp.where(incl_t, g_cat, 0.0), axis=0, keepdims=True)
        glast = [jnp.sum(g_cols[h], axis=0, keepdims=True) for h in range(nh)]
        decay = jnp.exp(jnp.where(incl, cat(gc_cols) - grow, NEG))
        b_nat = nat(b_cols)
        eg_nat = nat([jnp.exp(gc_cols[h]) for h in range(nh)])
        kb = k * b_nat
        kbh, kbl = _split2(kb)
        kdh, kdl = _split2(bd_nat(k))
        qg_ref[0, 0, rows, :] = q * eg_nat
        kd = (k * nat([jnp.exp(glast[h] - gc_cols[h]) for h in range(nh)])).astype(BF16)
        egl_ref[0, 0, ch] = nat([jnp.exp(glast[h]) for h in range(nh)])[0:1]
        return dict(rows=rows, ch=ch, q=q, decay=decay, kbh=kbh, kbl=kbl, kdh=kdh, kdl=kdl, kd=kd,
                    vb=v * b_nat, kbeg=kb * eg_nat)

    st = [setup(ch) for ch in range(nch)]
    kk1 = [lax.dot_general(jnp.concatenate([s["kbh"], s["kbl"]], axis=0), s["kdh"], _NT,
                           preferred_element_type=F32) for s in st]
    kk2 = [lax.dot_general(s["kbh"], s["kdl"], _NT, preferred_element_type=F32) for s in st]
    a = [jnp.where(strict, (x[:c] + x[c:] + y) * s["decay"], 0.0) for x, y, s in zip(kk1, kk2, st)]
    tinv = [eyef - x for x in a]
    p = [_dot3s(x, bd_cat(x)) for x in a]
    for _ in range(4):
        t = [_dot3s(jnp.concatenate([ti, pi], axis=0), bd_cat(pi)) for ti, pi in zip(tinv, p)]
        tinv = [ti + x[:c] for ti, x in zip(tinv, t)]
        p = [x[c:] for x in t]
    tinv = [ti + _dot3s(ti, bd_cat(pi)) for ti, pi in zip(tinv, p)]
    us = [_dot3s(ti, bd_nat(s["vb"])) for ti, s in zip(tinv, st)]
    ws = [_dot3s(ti, bd_nat(s["kbeg"])) for ti, s in zip(tinv, st)]
    qks = [lax.dot_general(s["q"].astype(BF16), s["kdh"], _NT, preferred_element_type=F32) for s in st]
    kdts = [lax.dot_general(eye_b, s["kd"], _NT, preferred_element_type=F32) for s in st]
    for s, u, w, qk, kdt in zip(st, us, ws, qks, kdts):
        u_ref[0, 0, s["rows"], :] = u
        w_ref[0, 0, s["rows"], :] = w
        qk_ref[0, 0, s["ch"]] = qk * s["decay"]
        kdt_ref[0, 0, s["ch"]] = kdt.astype(BF16)


def _gdn_scan_body(*refs):
    ins, o_refs, s_ref = (refs[:6], refs[6:12]), refs[12:14], refs[14]

    @pl.when(pl.program_id(1) == 0)
    def _():
        s_ref[...] = jnp.zeros_like(s_ref)

    c = GD_CHUNK
    heads = range(GD_HEADS)
    sls = [slice(h * LANES, (h + 1) * LANES) for h in heads]
    hl = lax.broadcasted_iota(jnp.int32, (c, GD_HEADS * LANES), 1) // LANES
    chains = [(d, h) for d in range(2) for h in heads]
    ss = [s_ref[d, h] for d, h in chains]
    rs = [_dot1(jnp.concatenate([ins[d][1][0, 0, :, sls[h]], ins[d][2][0, 0, :, sls[h]]], axis=0), s)
          for (d, h), s in zip(chains, ss)]
    v_news = [ins[d][0][0, 0, :, sls[h]] - r[:c] for (d, h), r in zip(chains, rs)]
    upd = [_dot1(ins[d][4][0, 0, 0, sls[h], :], vn) for (d, h), vn in zip(chains, v_news)]
    for d in range(2):
        lo = d * GD_HEADS
        v_all = jnp.concatenate(v_news[lo:lo + GD_HEADS], axis=1)
        vbd = jnp.concatenate([jnp.where(hl == h, v_all, 0.0) for h in heads], axis=0)
        o_refs[d][0] = (jnp.concatenate([r[c:] for r in rs[lo:lo + GD_HEADS]], axis=1)
                        + _dot1(ins[d][3][0, 0, 0], vbd))
    for i, (d, h) in enumerate(chains):
        s_ref[d, h] = ss[i] * ins[d][5][0, 0, 0, :, sls[h]] + upd[i]


def _gdn(q, k, v, gb, l_len):
    b, t, _ = q.shape
    c = GD_CHUNK
    nc, ncc = t // c, l_len // c
    nch = _pick(nc, (GD_LOCAL_CHUNKS, 1))
    h = GD_HEADS
    hw = h * LANES
    tokl = pl.BlockSpec((1, nch * c, hw), lambda bi, d, j: (bi, j, 0))
    tok4 = pl.BlockSpec((1, 1, nch * c, hw), lambda bi, d, j: (bi, d, j, 0))
    ch5 = lambda r, w2: pl.BlockSpec((1, 1, nch, r, w2), lambda bi, d, j: (bi, d, j, 0, 0))
    tok_shape = jax.ShapeDtypeStruct((b, 2, t, hw), F32)
    u, w, qg, qk, kdt, egl = pl.pallas_call(
        functools.partial(_gdn_local_body, nch=nch),
        grid=(b, 2, nc // nch),
        in_specs=[tokl, tokl, tokl, pl.BlockSpec((1, nch * c, LANES), lambda bi, d, j: (bi, j, 0))],
        out_specs=[tok4, tok4, tok4, ch5(c, h * c), ch5(hw, c), ch5(1, hw)],
        out_shape=[tok_shape, tok_shape, tok_shape,
                   jax.ShapeDtypeStruct((b, 2, nc, c, h * c), F32),
                   jax.ShapeDtypeStruct((b, 2, nc, hw, c), BF16),
                   jax.ShapeDtypeStruct((b, 2, nc, 1, hw), F32)],
        compiler_params=_params(("parallel", "parallel", "parallel")),
        name="gdn_local",
    )(q, k, v, gb)

    def chunk(d, j):
        back = jnp.where(j < ncc, ncc - 1 - j, nc - 1 - j + ncc)
        return jnp.where(d == 0, j, back)

    def specs(d):
        stok = pl.BlockSpec((1, 1, c, hw), lambda bi, j: (bi, d, chunk(d, j), 0))
        sch = lambda r, w2: pl.BlockSpec((1, 1, 1, r, w2), lambda bi, j: (bi, d, chunk(d, j), 0, 0))
        return [stok, stok, stok, sch(c, h * c), sch(hw, c), sch(1, hw)]

    out_spec = lambda d: pl.BlockSpec((1, c, hw), lambda bi, j: (bi, chunk(d, j), 0))
    out_shape = jax.ShapeDtypeStruct((b, t, hw), F32)
    return pl.pallas_call(
        _gdn_scan_body,
        grid=(b, nc),
        in_specs=specs(0) + specs(1),
        out_specs=[out_spec(0), out_spec(1)],
        out_shape=[out_shape, out_shape],
        scratch_shapes=[pltpu.VMEM((2, h, GD_DK, GD_DV), F32)],
        compiler_params=_params(("parallel", "arbitrary")),
        name="gdn_scan",
    )(u, w, qg, qk, kdt, egl, u, w, qg, qk, kdt, egl)


def _merge_body(x_ref, oa_ref, ob_ref, og0_ref, og1_ref, z_ref, gl0_ref, gl1_ref, bg_ref, ng_ref,
                gt_ref, wb_ref, wo_ref, o_ref):
    d = x_ref.shape[1]
    parts = []
    for h in range(GD_HEADS):
        sl = slice(h * LANES, (h + 1) * LANES)
        o = og0_ref[:, sl] + og1_ref[:, sl]
        z = z_ref[:, sl]
        o = o * lax.rsqrt(jnp.mean(o * o, axis=-1, keepdims=True) + EPS) * ng_ref[...]
        parts.append(o * (z * jax.nn.sigmoid(z)))
    oc = jnp.concatenate(parts, axis=1)
    gl = jnp.concatenate([gl0_ref[...], gl1_ref[...]], axis=1)
    merged = None
    for r, o_r in enumerate((oa_ref[...], ob_ref[...], oc)):
        y = jnp.dot(o_r.astype(BF16), wb_ref[r], preferred_element_type=F32)
        t = jax.nn.sigmoid(gl[:, r * d:(r + 1) * d] + bg_ref[:, r * d:(r + 1) * d]) * y
        merged = t if merged is None else merged + t
    m = jnp.dot(merged.astype(BF16), wo_ref[...], preferred_element_type=F32)
    o_ref[...] = x_ref[...] + gt_ref[0] * m


def _merge(xs, oa, ob, og, proj, b_gate, norm_g, mod, wb, wo, bn, t_len, l_len):
    n, d = xs.shape
    tm = _pick(math.gcd(l_len, t_len), (256, 128))
    tpb, lt = t_len // tm, l_len // tm
    tok = lambda w: pl.BlockSpec((tm, w), lambda i: (i, 0))
    full = lambda a: pl.BlockSpec(a.shape, lambda i: (0,) * a.ndim)
    return pl.pallas_call(
        _merge_body,
        grid=(n // tm,),
        in_specs=[tok(d), tok(BRANCH_W), tok(BRANCH_W), tok(BRANCH_W), tok(BRANCH_W),
                  pl.BlockSpec((tm, BRANCH_W), lambda i: (i, C_ZC // BRANCH_W)),
                  pl.BlockSpec((tm, CONV_BLK), lambda i: (i, C_GL // CONV_BLK)),
                  pl.BlockSpec((tm, CONV_BLK), lambda i: (i, C_GL // CONV_BLK + 1)),
                  full(b_gate), full(norm_g),
                  pl.BlockSpec((1, 1, d), lambda i: (_row_sel(i, tpb, lt, bn) * 6 + 2, 0, 0)),
                  full(wb), full(wo)],
        out_specs=tok(d),
        out_shape=jax.ShapeDtypeStruct((n, d), F32),
        compiler_params=_params(("parallel",)),
        name="merge",
    )(xs, oa, ob, og[0].reshape(n, BRANCH_W), og[1].reshape(n, BRANCH_W), proj, proj, proj,
      b_gate, norm_g, mod, wb, wo)


_CAND = [(r1, r2) for r1 in range(1, PEER_TOPK + 2) for r2 in range(1, PEER_TOPK + 2)
         if r1 * r2 <= PEER_TOPK + 1]


def _extract_top(cur, n):
    rank = jnp.full(cur.shape, 99.0, F32)
    vals = []
    for r in range(n):
        m = jnp.max(cur, axis=0, keepdims=True)
        hit = cur == m
        rank = jnp.where(hit, float(r + 1), rank)
        cur = jnp.where(hit, -jnp.inf, cur)
        vals.append(m)
    return vals, rank


def _topk_body(x_ref, g_ref, sh_ref, sc_ref, wq_ref, keys_ref, h_ref, cnt1_ref, e1_ref, rk2_ref, e2_ref):
    hm = _modulate(x_ref[...], g_ref[...], sh_ref[0], sc_ref[0])
    h_ref[...] = (hm * math.sqrt(0.5)).astype(BF16)
    hmod = hm.astype(BF16)
    nr = PEER_TOPK + 1
    for h in range(PEER_HEADS):
        st = []
        for c in range(2):
            col = (2 * h + c) * LANES
            qs = jnp.dot(hmod, wq_ref[:, col:col + LANES], preferred_element_type=F32)
            st.append(_dot3(keys_ref[c], qs, _NT))
        a, rank1 = _extract_top(st[0], nr)
        b, rank2 = _extract_top(st[1], nr)
        rows = [a[r1 - 1] + b[r2 - 1] for r1, r2 in _CAND]
        pad = (-len(rows)) % SUBLANES
        cand = jnp.concatenate(rows + [jnp.full_like(rows[0], -jnp.inf)] * pad, axis=0)
        top, _ = _extract_top(cand, nr)
        tau = 0.5 * (top[PEER_TOPK - 1] + top[PEER_TOPK])
        z = jnp.sum(jnp.where(cand >= tau, jnp.exp(cand - top[0]), 0.0), axis=0, keepdims=True)
        cnt1 = jnp.zeros_like(rank1)
        for r1 in range(1, PEER_TOPK + 1):
            cnt = None
            for r2 in range(1, PEER_TOPK + 1):
                if r1 * r2 <= PEER_TOPK + 1:
                    t = (a[r1 - 1] + b[r2 - 1] >= tau).astype(F32)
                    cnt = t if cnt is None else cnt + t
            cnt1 = jnp.where(rank1 == float(r1), cnt, cnt1)
        cnt1_ref[h] = cnt1
        e1_ref[h] = jnp.exp(st[0] - a[0])
        rk2_ref[h] = rank2.astype(BF16)
        e2_ref[h] = (jnp.exp(st[1] - b[0]) * (1.0 / z)).astype(BF16)


def _peer_topk(xs, g, mod, wq, keys, bn, t_len, l_len):
    n, d = xs.shape
    tm = _pick(math.gcd(l_len, t_len), (256, 128))
    tpb, lt = t_len // tm, l_len // tm
    sel = lambda i: _row_sel(i, tpb, lt, bn)
    out = pl.BlockSpec((PEER_HEADS, N_KEYS, tm), lambda i: (0, 0, i))
    shp = lambda dt: jax.ShapeDtypeStruct((PEER_HEADS, N_KEYS, n), dt)
    return pl.pallas_call(
        _topk_body,
        grid=(n // tm,),
        in_specs=[pl.BlockSpec((tm, d), lambda i: (i, 0)),
                  pl.BlockSpec((1, d), lambda i: (0, 0)),
                  pl.BlockSpec((1, 1, d), lambda i: (sel(i) * 6 + 3, 0, 0)),
                  pl.BlockSpec((1, 1, d), lambda i: (sel(i) * 6 + 4, 0, 0)),
                  pl.BlockSpec(wq.shape, lambda i: (0, 0)),
                  pl.BlockSpec(keys.shape, lambda i: (0, 0, 0))],
        out_specs=[pl.BlockSpec((tm, d), lambda i: (i, 0)), out, out, out, out],
        out_shape=[jax.ShapeDtypeStruct((n, d), BF16), shp(F32), shp(F32), shp(BF16), shp(BF16)],
        compiler_params=_params(("parallel",)),
        name="peer_topk",
    )(xs, g, mod, mod, wq, keys)


DENSE_CHUNK = 256


def _dense_body(x_ref, h_ref, u_ref, vt_ref, cnt1_ref, e1_ref, rk2_ref, e2_ref, *rest, te):
    gt_refs, o_ref, acc_ref = rest[:-2], rest[-2], rest[-1]
    e = pl.program_id(1)

    @pl.when(e == 0)
    def _():
        acc_ref[...] = jnp.zeros_like(acc_ref)

    hs = h_ref[...]
    zero = jnp.zeros((N_KEYS, hs.shape[0]), BF16)
    ck = DENSE_CHUNK
    nck = te // ck

    def mm1(c):
        return lax.dot_general(u_ref[c * ck:(c + 1) * ck, :], hs, _NT, preferred_element_type=F32)

    def gate(c):
        parts = []
        for j in range(c * ck // N_KEYS, (c + 1) * ck // N_KEYS):
            i1 = e * (te // N_KEYS) + j
            g = zero
            for h in range(PEER_HEADS):
                cnt = cnt1_ref[h, pl.ds(i1, 1), :].astype(BF16)
                w1 = e1_ref[h, pl.ds(i1, 1), :].astype(BF16)
                g = g + jnp.where(rk2_ref[h] <= cnt, e2_ref[h], zero) * w1
            parts.append(g)
        return jnp.concatenate(parts, axis=0)

    acts = [mm1(0)]
    hids = []
    for c in range(nck):
        if c + 1 < nck:
            acts.append(mm1(c + 1))
        y = acts[c]
        hids.append((y * (1.0 + lax.erf(y))).astype(BF16) * gate(c))
        if c % 2 == 1:
            pair = jnp.concatenate(hids[c - 1:c + 1], axis=0)
            acc_ref[...] += jnp.dot(vt_ref[:, (c - 1) * ck:(c + 1) * ck], pair,
                                    preferred_element_type=F32)

    @pl.when(e == pl.num_programs(1) - 1)
    def _():
        base = acc_ref.shape[1] // len(gt_refs)
        for k, gt_ref in enumerate(gt_refs):
            rs = slice(k * base, (k + 1) * base)
            o_ref[rs] = x_ref[rs] + (gt_ref[0] * math.sqrt(0.5)) * acc_ref[:, rs].T


def _peer_dense(xs, hmod, u, vt, cnt1, e1, rk2, e2, mod, bn, t_len, l_len):
    n, d = xs.shape
    ne = u.shape[0]
    base = _pick(math.gcd(l_len, t_len), (256, 128))
    nsub = 2 if n % (2 * base) == 0 else 1
    tm = nsub * base
    te = _pick(ne, (2048, 1024, 512))
    tpb, lt = t_len // base, l_len // base
    gts = [pl.BlockSpec((1, 1, d), lambda i, e, k=k: (_row_sel(i * nsub + k, tpb, lt, bn) * 6 + 5, 0, 0))
           for k in range(nsub)]
    sel = pl.BlockSpec((PEER_HEADS, N_KEYS, tm), lambda i, e: (0, 0, i))
    tok = pl.BlockSpec((tm, d), lambda i, e: (i, 0))
    tab = pl.BlockSpec((te, d), lambda i, e: (e, 0))
    return pl.pallas_call(
        functools.partial(_dense_body, te=te),
        grid=(n // tm, ne // te),
        in_specs=[tok, tok, tab, pl.BlockSpec((d, te), lambda i, e: (0, e)), sel, sel, sel, sel] + gts,
        out_specs=tok,
        out_shape=jax.ShapeDtypeStruct((n, d), F32),
        scratch_shapes=[pltpu.VMEM((d, tm), F32)],
        compiler_params=_params(("parallel", "arbitrary")),
        name="peer_dense",
    )(xs, hmod, u, vt, cnt1, e1, rk2, e2, *([mod] * nsub))


def _rope_tables(rows, l_len):
    row = jnp.repeat(jnp.arange(rows), GRID_W).astype(F32)
    col = jnp.tile(jnp.arange(GRID_W), rows).astype(F32)
    inv = jnp.power(ROPE_THETA, -jnp.arange(ROPE_PAIRS, dtype=F32) / ROPE_PAIRS)
    ar = row[:, None] * inv
    ac = col[:, None] * inv
    ang = jnp.concatenate([ar, ar, ac, ac], axis=-1)
    cos = jnp.concatenate([jnp.ones((l_len, HEAD_DIM), F32), jnp.cos(ang)], axis=0)
    sin = jnp.concatenate([jnp.zeros((l_len, HEAD_DIM), F32), jnp.sin(ang)], axis=0)
    cos = jnp.tile(cos, (1, 2))
    sin = jnp.tile(sin, (1, 2))
    first = (jnp.arange(LANES) % (2 * ROPE_PAIRS)) < ROPE_PAIRS
    return cos, jnp.where(first, -sin, 0.0), jnp.where(first, 0.0, sin)


def _reorder_w_in(w):
    d = w.shape[0]
    qa_ka_va, qb, kb_vb = w[:, :1536], w[:, 1536:2048], w[:, 2048:2304]
    qkvc, zc, ab, gl = w[:, 2304:3840], w[:, 3840:4352], w[:, 4352:4368], w[:, 4368:]
    qb = qb.reshape(d, WG_KV_HEADS, WG_GROUP, HEAD_DIM).transpose(0, 2, 1, 3).reshape(d, 512)
    pad = jnp.zeros((d, C_GL - C_AB - ab.shape[1]), w.dtype)
    return jnp.concatenate([qa_ka_va, qkvc, zc, qb, kb_vb, ab, pad, gl], axis=1).astype(BF16)


def kernel(x, c, ctx, c_ctx, norm1_g, norm2_g, w_ada, b_ada, w_in, b_gate, qk_norm_g, diff_lam,
           diff_subln_g, wg_sink, gd_conv_w, gd_a_log, gd_dt_bias, gd_norm_g, w_branch, w_out,
           peer_wq, peer_keys, peer_u, peer_v):
    bn, s_len, d = x.shape
    l_len = ctx.shape[1]
    t_len = l_len + s_len
    n_tok = bn * t_len
    depth = w_in.shape[0]
    cos, sa, sb = _rope_tables(s_len // GRID_W, l_len)

    xs = jnp.concatenate([ctx, x], axis=1).reshape(n_tok, d)
    cvec = jnp.concatenate([c, c_ctx[None]], axis=0)
    cact = jnp.pad(jax.nn.silu(cvec), ((0, 16 - (bn + 1) % 16), (0, 0))).astype(BF16)
    lane_pad = lambda a: jnp.pad(a.reshape(1, -1), ((0, 0), (0, LANES - a.size)))

    for l in range(depth):
        lam_init = 0.8 - 0.6 * math.exp(-0.3 * l)
        mod = (_mm(cact, w_ada[l].astype(BF16))[:bn + 1] + b_ada[l]).reshape((bn + 1) * 6, 1, d)

        proj = _inproj(xs, norm1_g[l].reshape(1, d), mod, _reorder_w_in(w_in[l]), bn, t_len, l_len)
        qa, ka, va, qb, kb, vb, gq, gk, gv, gb = _prep(
            proj, cos, sa, sb, jnp.tile(qk_norm_g[l], (1, 2)), gd_conv_w[l],
            lane_pad(gd_a_log[l]), lane_pad(gd_dt_bias[l]), t_len, l_len)
        b3 = lambda a: a.reshape(bn, t_len, a.shape[-1])

        lv = diff_lam[l]
        lam = (jnp.exp(jnp.sum(lv[0, 0] * lv[0, 1])) - jnp.exp(jnp.sum(lv[1, 0] * lv[1, 1])) + lam_init)
        oa = _diff_attn(b3(qa), b3(ka), b3(va), lam.reshape(1), diff_subln_g[l].reshape(1, DA_VDIM),
                        1.0 - lam_init, l_len)
        ob = _win_attn(b3(qb), b3(kb), b3(vb), wg_sink[l], l_len)
        og = _gdn(b3(gq), b3(gk), b3(gv), b3(gb), l_len)

        wb = w_branch[l]
        wb1 = wb[1].reshape(WG_KV_HEADS, WG_GROUP, HEAD_DIM, d).transpose(1, 0, 2, 3).reshape(BRANCH_W, d)
        wb = jnp.stack([wb[0], wb1, wb[2]]).astype(BF16)
        xs = _merge(xs, oa.reshape(n_tok, BRANCH_W), ob.reshape(n_tok, BRANCH_W), og, proj,
                    b_gate[l].reshape(1, -1), gd_norm_g[l].reshape(1, GD_DV), mod, wb,
                    w_out[l].astype(BF16), bn, t_len, l_len)

        hmod, cnt1, e1, rk2, e2 = _peer_topk(xs, norm2_g[l].reshape(1, d), mod, peer_wq[l].astype(BF16),
                                             peer_keys[l], bn, t_len, l_len)
        xs = _peer_dense(xs, hmod, peer_u[l].astype(BF16), peer_v[l].astype(BF16).T, cnt1, e1, rk2, e2,
                         mod, bn, t_len, l_len)

    return xs.reshape(bn, t_len, d)[:, l_len:]
```

```python
import functools
import math

import jax
import jax.numpy as jnp
from jax import lax
from jax.experimental import pallas as pl
from jax.experimental.pallas import tpu as pltpu

F32 = jnp.float32
BF16 = jnp.bfloat16

EPS = 1e-6
GRID_W = 64
HEAD_DIM = 64
ROPE_THETA = 10000.0
ROPE_PAIRS = HEAD_DIM // 4
DA_HEADS = 4
DA_VDIM = 2 * HEAD_DIM
WG_HEADS = 8
WG_KV_HEADS = 2
WG_GROUP = WG_HEADS // WG_KV_HEADS
WINDOW = 128
WG_BLOCK = 128
GD_HEADS = 4
GD_DK = 128
GD_DV = 128
GD_CONV = 5
GD_CHUNK = 64
GD_QKV_W = GD_HEADS * (2 * GD_DK + GD_DV)
N_BRANCH = 3
BRANCH_W = 512
PEER_HEADS = 8
PEER_HALF = 128
N_KEYS = 128
PEER_TOPK = 16

LANES = 128
SUBLANES = 8
VMEM_LIMIT = 56 * 1024 * 1024
NEG = -1e30
LOG2E = math.log2(math.e)

C_QA, C_KA, C_VA, C_QKVC, C_ZC, C_QB, C_KB, C_VB, C_AB, C_GL = (
    0, 512, 1024, 1536, 3072, 3584, 4096, 4224, 4352, 4608)
PROJ_W = C_GL + N_BRANCH * 1024
PREP_W = C_GL
CONV_BLK = 1536


def _pick(n, cands):
    for c in cands:
        if n % c == 0:
            return c
    raise ValueError(f"no tile in {cands} divides {n}")


def _params(sem):
    return pltpu.CompilerParams(dimension_semantics=sem, vmem_limit_bytes=VMEM_LIMIT)


def _split2(a):
    hi = a.astype(BF16)
    return hi, (a - hi.astype(F32)).astype(BF16)


def _dot1(a, b, dn=None):
    a = a.astype(BF16)
    b = b.astype(BF16)
    if dn is None:
        return jnp.dot(a, b, preferred_element_type=F32)
    return lax.dot_general(a, b, dn, preferred_element_type=F32)


def _dot3(a, b, dn=None):
    ah, al = _split2(a)
    bh, bl = _split2(b)
    return _dot1(ah, bh, dn) + (_dot1(ah, bl, dn) + _dot1(al, bh, dn))


_NT = (((1,), (1,)), ((), ()))


def _row_sel(i, tpb, lt, bn):
    return jnp.where(i % tpb < lt, bn, i // tpb)


def _mm_body(x_ref, w_ref, o_ref):
    o_ref[...] = jnp.dot(x_ref[...], w_ref[...], preferred_element_type=F32)


def _mm(x, w):
    m, k = x.shape
    n = w.shape[1]
    tn = _pick(n, (1536, 1024, 512, 256, 128))
    return pl.pallas_call(
        _mm_body,
        grid=(n // tn,),
        in_specs=[pl.BlockSpec((m, k), lambda j: (0, 0)),
                  pl.BlockSpec((k, tn), lambda j: (0, j))],
        out_specs=pl.BlockSpec((m, tn), lambda j: (0, j)),
        out_shape=jax.ShapeDtypeStruct((m, n), F32),
        compiler_params=_params(("parallel",)),
        name="ada_mm",
    )(x, w)


def _modulate(x, g, sh, sc):
    h = x * lax.rsqrt(jnp.mean(x * x, axis=-1, keepdims=True) + EPS) * g
    return h * (1.0 + sc) + sh


def _inproj_body(x_ref, g_ref, sh_ref, sc_ref, w_ref, o_ref):
    h = _modulate(x_ref[...], g_ref[...], sh_ref[0], sc_ref[0])
    o_ref[...] = jnp.dot(h.astype(BF16), w_ref[...], preferred_element_type=F32)


def _inproj(xs, g, mod, w, bn, t_len, l_len):
    n, d = xs.shape
    tm = _pick(math.gcd(l_len, t_len), (512, 256, 128))
    tn = w.shape[1] // 2
    tpb, lt = t_len // tm, l_len // tm
    sel = lambda j, i: _row_sel(i, tpb, lt, bn)
    return pl.pallas_call(
        _inproj_body,
        grid=(w.shape[1] // tn, n // tm),
        in_specs=[pl.BlockSpec((tm, d), lambda j, i: (i, 0)),
                  pl.BlockSpec((1, d), lambda j, i: (0, 0)),
                  pl.BlockSpec((1, 1, d), lambda j, i: (sel(j, i) * 6, 0, 0)),
                  pl.BlockSpec((1, 1, d), lambda j, i: (sel(j, i) * 6 + 1, 0, 0)),
                  pl.BlockSpec((d, tn), lambda j, i: (0, j))],
        out_specs=pl.BlockSpec((tm, tn), lambda j, i: (i, j)),
        out_shape=jax.ShapeDtypeStruct((n, w.shape[1]), F32),
        compiler_params=_params(("parallel", "parallel")),
        name="inproj",
    )(xs, g, mod, mod, w)


def _prep_body(p_ref, prev_ref, next_ref, cos_ref, sa_ref, sb_ref, qkg_ref, cw_ref, alog_ref, dtb_ref,
               qa_ref, ka_ref, va_ref, qb_ref, kb_ref, vb_ref, gq_ref, gk_ref, gv_ref, gb_ref,
               xe_ref, *, tpb, lt):
    tm = p_ref.shape[0]
    li = lax.broadcasted_iota(jnp.int32, (LANES, LANES), 0)
    lj = lax.broadcasted_iota(jnp.int32, (LANES, LANES), 1)
    seg = jnp.where(li // HEAD_DIM == lj // HEAD_DIM, 1.0 / HEAD_DIM, 0.0).astype(BF16)
    cos = cos_ref[...]
    sa = sa_ref[...]
    sb = sb_ref[...]

    def normrope(x, gain):
        yh, yl = _split2(x * x)
        ms = jnp.dot(yh, seg, preferred_element_type=F32) + jnp.dot(yl, seg, preferred_element_type=F32)
        xn = x * lax.rsqrt(ms + EPS) * gain
        return xn * cos + pltpu.roll(xn, LANES - ROPE_PAIRS, 1) * sa + pltpu.roll(xn, ROPE_PAIRS, 1) * sb

    scale = HEAD_DIM ** -0.5 * LOG2E
    for c in range(4):
        cs = slice(c * LANES, (c + 1) * LANES)
        qa_ref[:, cs] = (normrope(p_ref[:, C_QA + c * LANES:C_QA + (c + 1) * LANES], qkg_ref[0:1]) * scale).astype(BF16)
        ka_ref[:, cs] = normrope(p_ref[:, C_KA + c * LANES:C_KA + (c + 1) * LANES], qkg_ref[1:2]).astype(BF16)
        qb_ref[:, cs] = (normrope(p_ref[:, C_QB + c * LANES:C_QB + (c + 1) * LANES], qkg_ref[2:3]) * scale).astype(BF16)
    kb_ref[...] = normrope(p_ref[:, C_KB:C_KB + LANES], qkg_ref[3:4]).astype(BF16)
    va_ref[...] = p_ref[:, C_VA:C_VA + 512].astype(BF16)
    vb_ref[...] = p_ref[:, C_VB:C_VB + LANES].astype(BF16)

    ti = pl.program_id(0) % tpb
    at_start = (ti == 0) | (ti == lt)
    at_end = (ti == lt - 1) | (ti == tpb - 1)
    xe_ref[0:SUBLANES] = jnp.where(at_start, 0.0, prev_ref[...])
    xe_ref[SUBLANES:SUBLANES + tm] = p_ref[:, C_QKVC:C_QKVC + GD_QKV_W]
    xe_ref[SUBLANES + tm:2 * SUBLANES + tm] = jnp.where(at_end, 0.0, next_ref[...])
    half = GD_CONV // 2
    y = None
    for i in range(GD_CONV):
        t = xe_ref[pl.ds(SUBLANES - half + i, tm), :] * cw_ref[i:i + 1]
        y = t if y is None else y + t
    y = y * jax.nn.sigmoid(y)
    for h in range(GD_HEADS):
        hs = slice(h * LANES, (h + 1) * LANES)
        q = y[:, h * LANES:(h + 1) * LANES]
        k = y[:, 512 + h * LANES:512 + (h + 1) * LANES]
        gq_ref[:, hs] = q * lax.rsqrt(jnp.sum(q * q, axis=-1, keepdims=True) + EPS) * (GD_DK ** -0.5)
        gk_ref[:, hs] = k * lax.rsqrt(jnp.sum(k * k, axis=-1, keepdims=True) + EPS)
    gv_ref[...] = y[:, 1024:]

    ab = p_ref[:, C_AB:C_AB + LANES]
    lane = lax.broadcasted_iota(jnp.int32, ab.shape, 1)
    gdec = -jnp.exp(alog_ref[...]) * jnp.logaddexp(ab + dtb_ref[...], 0.0)
    gb_ref[...] = jnp.where(lane < 2 * GD_HEADS, gdec, jax.nn.sigmoid(ab))


def _prep(proj, cos, sa, sb, qkg, cw, alog, dtb, t_len, l_len):
    n = proj.shape[0]
    tm = _pick(math.gcd(l_len, t_len), (256, 128))
    tpb, lt = t_len // tm, l_len // tm
    r8 = tm // SUBLANES
    nb8 = n // SUBLANES
    tok = lambda w: pl.BlockSpec((tm, w), lambda i: (i, 0))
    rope = pl.BlockSpec((tm, LANES), lambda i: (i % tpb, 0))
    full = lambda a: pl.BlockSpec(a.shape, lambda i: (0,) * a.ndim)
    shp = lambda w, dt: jax.ShapeDtypeStruct((n, w), dt)
    return pl.pallas_call(
        functools.partial(_prep_body, tpb=tpb, lt=lt),
        grid=(n // tm,),
        in_specs=[pl.BlockSpec((tm, PREP_W), lambda i: (i, 0)),
                  pl.BlockSpec((SUBLANES, CONV_BLK), lambda i: (jnp.maximum(i * r8 - 1, 0), 1)),
                  pl.BlockSpec((SUBLANES, CONV_BLK), lambda i: (jnp.minimum((i + 1) * r8, nb8 - 1), 1)),
                  rope, rope, rope, full(qkg), full(cw), full(alog), full(dtb)],
        out_specs=[tok(512), tok(512), tok(512), tok(512), tok(LANES), tok(LANES),
                   tok(512), tok(512), tok(512), tok(LANES)],
        out_shape=[shp(512, BF16), shp(512, BF16), shp(512, BF16), shp(512, BF16), shp(LANES, BF16),
                   shp(LANES, BF16), shp(512, F32), shp(512, F32), shp(512, F32), shp(LANES, F32)],
        scratch_shapes=[pltpu.VMEM((tm + 2 * SUBLANES, GD_QKV_W), F32)],
        compiler_params=_params(("parallel",)),
        name="prep",
    )(proj, proj, proj, cos, sa, sb, qkg, cw, alog, dtb)


def _diff_body(lam_ref, q_ref, k_ref, v_ref, g_ref, o_ref, *, post, l_len, n_ctx):
    q = q_ref[0]
    lane = lax.broadcasted_iota(jnp.int32, q.shape, 1)
    zero = jnp.zeros_like(q)
    q1 = jnp.where(lane < HEAD_DIM, q, zero)
    q2 = jnp.where(lane >= HEAD_DIM, q, zero)

    def run(k, v):
        def attend(s):
            p = jnp.exp2(s - jnp.max(s, axis=-1, keepdims=True))
            l = jnp.sum(p, axis=-1, keepdims=True)
            return jnp.dot(p.astype(BF16), v, preferred_element_type=F32) * (1.0 / l)

        s1 = lax.dot_general(q1, k, _NT, preferred_element_type=F32)
        s2 = lax.dot_general(q2, k, _NT, preferred_element_type=F32)
        o = attend(s1) - lam_ref[0] * attend(s2)
        ms = jnp.mean(o * o, axis=-1, keepdims=True)
        o_ref[0] = o * lax.rsqrt(ms + EPS) * (g_ref[...] * post)

    i = pl.program_id(2)

    @pl.when(i < n_ctx)
    def _():
        run(k_ref[0, :l_len], v_ref[0, :l_len])

    @pl.when(i >= n_ctx)
    def _():
        run(k_ref[0], v_ref[0])


def _diff_attn(q, k, v, lam, subln_g, post, l_len):
    b, t, _ = q.shape
    tq = _pick(math.gcd(l_len, t), (256, 128))
    return pl.pallas_call(
        functools.partial(_diff_body, post=post, l_len=l_len, n_ctx=l_len // tq),
        grid=(b, DA_HEADS, t // tq),
        in_specs=[pl.BlockSpec(memory_space=pltpu.SMEM),
                  pl.BlockSpec((1, tq, LANES), lambda bi, h, i: (bi, i, h)),
                  pl.BlockSpec((1, t, LANES), lambda bi, h, i: (bi, 0, h)),
                  pl.BlockSpec((1, t, LANES), lambda bi, h, i: (bi, 0, h)),
                  pl.BlockSpec((1, LANES), lambda bi, h, i: (0, 0))],
        out_specs=pl.BlockSpec((1, tq, LANES), lambda bi, h, i: (bi, i, h)),
        out_shape=jax.ShapeDtypeStruct((b, t, DA_HEADS * DA_VDIM), F32),
        compiler_params=_params(("parallel", "parallel", "parallel")),
        name="diff_attn",
    )(lam, q, k, v, subln_g)


def _win_body(sink_ref, q_ref, kc_ref, vc_ref, kp_ref, kn_ref, kx_ref, vp_ref, vn_ref, vx_ref, o_ref,
              *, lo, s_len):
    l_len = kc_ref.shape[1]
    q = q_ref[0]
    lane = lax.broadcasted_iota(jnp.int32, (WG_BLOCK, LANES), 1)

    def run(kcat, vcat, valid):
        vlane = lax.broadcasted_iota(jnp.int32, vcat.shape, 1)
        vz = jnp.zeros_like(vcat)
        vhalf = [jnp.where(vlane < HEAD_DIM, vcat, vz), jnp.where(vlane >= HEAD_DIM, vcat, vz)]
        heads = [(g, kv) for g in range(WG_GROUP) for kv in range(WG_KV_HEADS)]
        zq = jnp.zeros((WG_BLOCK, LANES), q.dtype)
        half = [lane < HEAD_DIM, lane >= HEAD_DIM]
        qms = [jnp.where(half[kv], q[:, g * LANES:(g + 1) * LANES], zq) for g, kv in heads]
        ss = [lax.dot_general(qm, kcat, _NT, preferred_element_type=F32) for qm in qms]
        if valid is not None:
            ss = [jnp.where(valid, s, NEG) for s in ss]
        sks = [sink_ref[kv * WG_GROUP + g] * LOG2E for g, kv in heads]
        ms = [jnp.maximum(jnp.max(s, axis=-1, keepdims=True), sk) for s, sk in zip(ss, sks)]
        ps = [jnp.exp2(s - m) for s, m in zip(ss, ms)]
        dens = [jnp.sum(p, axis=-1, keepdims=True) + jnp.exp2(sk - m) for p, sk, m in zip(ps, sks, ms)]
        os = [jnp.dot(p.astype(BF16), vhalf[kv], preferred_element_type=F32) * (1.0 / den)
              for p, den, (g, kv) in zip(ps, dens, heads)]
        for g in range(WG_GROUP):
            o_ref[0, :, g * LANES:(g + 1) * LANES] = os[2 * g] + os[2 * g + 1]

    n = pl.program_id(1)

    @pl.when(n < lo)
    def _():
        run(kc_ref[0], vc_ref[0], None)

    @pl.when(n >= lo)
    def _():
        kcat = jnp.concatenate([kc_ref[0], kp_ref[0], kn_ref[0], kx_ref[0]], axis=0)
        vcat = jnp.concatenate([vc_ref[0], vp_ref[0], vn_ref[0], vx_ref[0]], axis=0)
        nk = kcat.shape[0]
        col = lax.broadcasted_iota(jnp.int32, (WG_BLOCK, nk), 1)
        row = lax.broadcasted_iota(jnp.int32, (WG_BLOCK, nk), 0)
        j = col - l_len
        rel = j - WG_BLOCK - row
        kpos = (n - lo) * WG_BLOCK + j - WG_BLOCK
        valid = (col < l_len) | ((jnp.abs(rel) <= WINDOW) & (kpos >= 0) & (kpos < s_len))
        run(kcat, vcat, valid)


def _win_attn(q, k, v, sink, l_len):
    b, t, _ = q.shape
    blk = WG_BLOCK
    lo, nb = l_len // blk, t // blk
    ctx_spec = pl.BlockSpec((1, l_len, LANES), lambda bi, n: (bi, 0, 0))
    win = [pl.BlockSpec((1, blk, LANES), lambda bi, n, d=d: (bi, jnp.clip(n + d, lo, nb - 1), 0))
           for d in (-1, 0, 1)]
    return pl.pallas_call(
        functools.partial(_win_body, lo=lo, s_len=t - l_len),
        grid=(b, nb),
        in_specs=[pl.BlockSpec(memory_space=pltpu.SMEM),
                  pl.BlockSpec((1, blk, 4 * LANES), lambda bi, n: (bi, n, 0)),
                  ctx_spec, ctx_spec] + win + win,
        out_specs=pl.BlockSpec((1, blk, 4 * LANES), lambda bi, n: (bi, n, 0)),
        out_shape=jax.ShapeDtypeStruct((b, t, 4 * LANES), F32),
        compiler_params=_params(("parallel", "parallel")),
        name="win_attn",
    )(sink, q, k, v, k, k, k, v, v, v)


GD_LOCAL_CHUNKS = 4


def _dot3s(x, r):
    m = x.shape[0]
    xh, xl = _split2(x)
    rh, rl = _split2(r)
    t = jnp.dot(jnp.concatenate([xh, xl], axis=0), rh, preferred_element_type=F32)
    return t[:m] + t[m:] + jnp.dot(xh, rl, preferred_element_type=F32)


def _gdn_local_body(q_ref, k_ref, v_ref, gb_ref, u_ref, w_ref, qg_ref, qk_ref, kdt_ref, egl_ref, *, nch):
    d = pl.program_id(1)
    c = GD_CHUNK
    nh = GD_HEADS
    ii = lax.broadcasted_iota(jnp.int32, (c, nh * c), 0)
    jl = lax.broadcasted_iota(jnp.int32, (c, nh * c), 1)
    jj = jl % c
    blk = jl // c
    rel = (jj - ii) * (1 - 2 * d)
    incl = rel <= 0
    strict = rel < 0
    incl_t = rel >= 0
    eye = ii == jj
    eyef = eye.astype(F32)
    hl = lax.broadcasted_iota(jnp.int32, (c, nh * LANES), 1) // LANES
    e_i = lax.broadcasted_iota(jnp.int32, (LANES, LANES), 0)
    e_j = lax.broadcasted_iota(jnp.int32, (LANES, LANES), 1)
    eye_b = (e_i == e_j).astype(BF16)
    fwd = d == 0

    def cat(cols):
        out = jnp.zeros((c, nh * c), F32)
        for h in range(nh):
            out = jnp.where(blk == h, cols[h], out)
        return out

    def nat(cols):
        out = jnp.zeros((c, nh * LANES), F32)
        for h in range(nh):
            out = jnp.where(hl == h, cols[h], out)
        return out

    def bd_cat(p):
        return jnp.concatenate([jnp.where(blk == h, p, 0.0) for h in range(nh)], axis=0)

    def bd_nat(x):
        return jnp.concatenate([jnp.where(hl == h, x, 0.0) for h in range(nh)], axis=0)

    def setup(ch):
        rows = slice(ch * c, (ch + 1) * c)
        gb = gb_ref[0, rows, :]
        q = q_ref[0, rows, :]
        k = k_ref[0, rows, :]
        v = v_ref[0, rows, :]
        g_cols = [jnp.where(fwd, gb[:, h:h + 1], gb[:, nh + h:nh + h + 1]) for h in range(nh)]
        b_cols = [jnp.where(fwd, gb[:, 2 * nh + h:2 * nh + h + 1], gb[:, 3 * nh + h:3 * nh + h + 1])
                  for h in range(nh)]
        g_cat = cat(g_cols)
        g_row = jnp.sum(jnp.where(eye, g_cat, 0.0), axis=0, keepdims=True)
        gc_cols = [jnp.sum(jnp.where(incl & (blk == h), g_row, 0.0), axis=1, keepdims=True)
                   for h in range(nh)]
        grow = jnp.sum(jnp.where(incl_t, g_cat, 0.0), axis=0, keepdims=True)
        glast = [jnp.sum(g_cols[h], axis=0, keepdims=True) for h in range(nh)]
        decay = jnp.exp(jnp.where(incl, cat(gc_cols) - grow, NEG))
        b_nat = nat(b_cols)
        eg_nat = nat([jnp.exp(gc_cols[h]) for h in range(nh)])
        kb = k * b_nat
        kbh, kbl = _split2(kb)
        kdh, kdl = _split2(bd_nat(k))
        qg_ref[0, 0, rows, :] = q * eg_nat
        kd = (k * nat([jnp.exp(glast[h] - gc_cols[h]) for h in range(nh)])).astype(BF16)
        egl_ref[0, 0, ch] = nat([jnp.exp(glast[h]) for h in range(nh)])[0:1]
        return dict(rows=rows, ch=ch, q=q, decay=decay, kbh=kbh, kbl=kbl, kdh=kdh, kdl=kdl, kd=kd,
                    vb=v * b_nat, kbeg=kb * eg_nat)

    st = [setup(ch) for ch in range(nch)]
    kk1 = [lax.dot_general(jnp.concatenate([s["kbh"], s["kbl"]], axis=0), s["kdh"], _NT,
                           preferred_element_type=F32) for s in st]
    kk2 = [lax.dot_general(s["kbh"], s["kdl"], _NT, preferred_element_type=F32) for s in st]
    a = [jnp.where(strict, (x[:c] + x[c:] + y) * s["decay"], 0.0) for x, y, s in zip(kk1, kk2, st)]
    tinv = [eyef - x for x in a]
    p = [_dot3s(x, bd_cat(x)) for x in a]
    for _ in range(4):
        t = [_dot3s(jnp.concatenate([ti, pi], axis=0), bd_cat(pi)) for ti, pi in zip(tinv, p)]
        tinv = [ti + x[:c] for ti, x in zip(tinv, t)]
        p = [x[c:] for x in t]
    tinv = [ti + _dot3s(ti, bd_cat(pi)) for ti, pi in zip(tinv, p)]
    us = [_dot3s(ti, bd_nat(s["vb"])) for ti, s in zip(tinv, st)]
    ws = [_dot3s(ti, bd_nat(s["kbeg"])) for ti, s in zip(tinv, st)]
    qks = [lax.dot_general(s["q"].astype(BF16), s["kdh"], _NT, preferred_element_type=F32) for s in st]
    kdts = [jnp.concatenate([lax.dot_general(eye_b, s["kd"][:, h * LANES:(h + 1) * LANES], _NT,
                                             preferred_element_type=F32) for h in range(nh)], axis=0)
            for s in st]
    for s, u, w, qk, kdt in zip(st, us, ws, qks, kdts):
        u_ref[0, 0, s["rows"], :] = u
        w_ref[0, 0, s["rows"], :] = w
        qk_ref[0, 0, s["ch"]] = qk * s["decay"]
        kdt_ref[0, 0, s["ch"]] = kdt.astype(BF16)


def _gdn_scan_body(*refs):
    ins, o_refs, s_ref = (refs[:6], refs[6:12]), refs[12:14], refs[14]

    @pl.when(pl.program_id(1) == 0)
    def _():
        s_ref[...] = jnp.zeros_like(s_ref)

    c = GD_CHUNK
    heads = range(GD_HEADS)
    sls = [slice(h * LANES, (h + 1) * LANES) for h in heads]
    hl = lax.broadcasted_iota(jnp.int32, (c, GD_HEADS * LANES), 1) // LANES
    chains = [(d, h) for d in range(2) for h in heads]
    ns = ins[0][3].shape[2]
    ss = [s_ref[d, h] for d, h in chains]
    for step in range(ns):
        pos = [step, ns - 1 - step]
        rows = [slice(p * c, (p + 1) * c) for p in pos]
        rs = [_dot1(jnp.concatenate([ins[d][1][0, 0, rows[d], sls[h]], ins[d][2][0, 0, rows[d], sls[h]]],
                                    axis=0), s) for (d, h), s in zip(chains, ss)]
        v_news = [ins[d][0][0, 0, rows[d], sls[h]] - r[:c] for (d, h), r in zip(chains, rs)]
        upd = [_dot1(ins[d][4][0, 0, pos[d], sls[h], :], vn) for (d, h), vn in zip(chains, v_news)]
        for d in range(2):
            lo = d * GD_HEADS
            v_all = jnp.concatenate(v_news[lo:lo + GD_HEADS], axis=1)
            vbd = jnp.concatenate([jnp.where(hl == h, v_all, 0.0) for h in heads], axis=0)
            o_refs[d][0, rows[d], :] = (jnp.concatenate([r[c:] for r in rs[lo:lo + GD_HEADS]], axis=1)
                                        + _dot1(ins[d][3][0, 0, pos[d]], vbd))
        ss = [s * ins[d][5][0, 0, pos[d], :, sls[h]] + up for (d, h), s, up in zip(chains, ss, upd)]
    for (d, h), s in zip(chains, ss):
        s_ref[d, h] = s


def _gdn(q, k, v, gb, l_len):
    b, t, _ = q.shape
    c = GD_CHUNK
    nc, ncc = t // c, l_len // c
    nch = _pick(nc, (GD_LOCAL_CHUNKS, 1))
    h = GD_HEADS
    hw = h * LANES
    tokl = pl.BlockSpec((1, nch * c, hw), lambda bi, d, j: (bi, j, 0))
    tok4 = pl.BlockSpec((1, 1, nch * c, hw), lambda bi, d, j: (bi, d, j, 0))
    ch5 = lambda r, w2: pl.BlockSpec((1, 1, nch, r, w2), lambda bi, d, j: (bi, d, j, 0, 0))
    tok_shape = jax.ShapeDtypeStruct((b, 2, t, hw), F32)
    u, w, qg, qk, kdt, egl = pl.pallas_call(
        functools.partial(_gdn_local_body, nch=nch),
        grid=(b, 2, nc // nch),
        in_specs=[tokl, tokl, tokl, pl.BlockSpec((1, nch * c, LANES), lambda bi, d, j: (bi, j, 0))],
        out_specs=[tok4, tok4, tok4, ch5(c, h * c), ch5(hw, c), ch5(1, hw)],
        out_shape=[tok_shape, tok_shape, tok_shape,
                   jax.ShapeDtypeStruct((b, 2, nc, c, h * c), F32),
                   jax.ShapeDtypeStruct((b, 2, nc, hw, c), BF16),
                   jax.ShapeDtypeStruct((b, 2, nc, 1, hw), F32)],
        compiler_params=_params(("parallel", "parallel", "parallel")),
        name="gdn_local",
    )(q, k, v, gb)

    ns = 2 if (ncc % 2 == 0 and nc % 2 == 0) else 1

    def blk(d, j):
        first = j * ns
        last = jnp.where(first < ncc, ncc - 1 - first, nc - 1 - first + ncc) - (ns - 1)
        return j if d == 0 else last // ns

    def specs(d):
        stok = pl.BlockSpec((1, 1, ns * c, hw), lambda bi, j: (bi, d, blk(d, j), 0))
        sch = lambda r, w2: pl.BlockSpec((1, 1, ns, r, w2), lambda bi, j: (bi, d, blk(d, j), 0, 0))
        return [stok, stok, stok, sch(c, h * c), sch(hw, c), sch(1, hw)]

    out_spec = lambda d: pl.BlockSpec((1, ns * c, hw), lambda bi, j: (bi, blk(d, j), 0))
    out_shape = jax.ShapeDtypeStruct((b, t, hw), F32)
    return pl.pallas_call(
        _gdn_scan_body,
        grid=(b, nc // ns),
        in_specs=specs(0) + specs(1),
        out_specs=[out_spec(0), out_spec(1)],
        out_shape=[out_shape, out_shape],
        scratch_shapes=[pltpu.VMEM((2, h, GD_DK, GD_DV), F32)],
        compiler_params=_params(("parallel", "arbitrary")),
        name="gdn_scan",
    )(u, w, qg, qk, kdt, egl, u, w, qg, qk, kdt, egl)


def _merge_body(x_ref, oa_ref, ob_ref, og0_ref, og1_ref, z_ref, gl0_ref, gl1_ref, bg_ref, ng_ref,
                gt_ref, wb_ref, wo_ref, o_ref):
    d = x_ref.shape[1]
    parts = []
    for h in range(GD_HEADS):
        sl = slice(h * LANES, (h + 1) * LANES)
        o = og0_ref[:, sl] + og1_ref[:, sl]
        z = z_ref[:, sl]
        o = o * lax.rsqrt(jnp.mean(o * o, axis=-1, keepdims=True) + EPS) * ng_ref[...]
        parts.append(o * (z * jax.nn.sigmoid(z)))
    oc = jnp.concatenate(parts, axis=1)
    gl = jnp.concatenate([gl0_ref[...], gl1_ref[...]], axis=1)
    merged = None
    for r, o_r in enumerate((oa_ref[...], ob_ref[...], oc)):
        y = jnp.dot(o_r.astype(BF16), wb_ref[r], preferred_element_type=F32)
        t = jax.nn.sigmoid(gl[:, r * d:(r + 1) * d] + bg_ref[:, r * d:(r + 1) * d]) * y
        merged = t if merged is None else merged + t
    m = jnp.dot(merged.astype(BF16), wo_ref[...], preferred_element_type=F32)
    o_ref[...] = x_ref[...] + gt_ref[0] * m


def _merge(xs, oa, ob, og, proj, b_gate, norm_g, mod, wb, wo, bn, t_len, l_len):
    n, d = xs.shape
    tm = _pick(math.gcd(l_len, t_len), (256, 128))
    tpb, lt = t_len // tm, l_len // tm
    tok = lambda w: pl.BlockSpec((tm, w), lambda i: (i, 0))
    full = lambda a: pl.BlockSpec(a.shape, lambda i: (0,) * a.ndim)
    return pl.pallas_call(
        _merge_body,
        grid=(n // tm,),
        in_specs=[tok(d), tok(BRANCH_W), tok(BRANCH_W), tok(BRANCH_W), tok(BRANCH_W),
                  pl.BlockSpec((tm, BRANCH_W), lambda i: (i, C_ZC // BRANCH_W)),
                  pl.BlockSpec((tm, CONV_BLK), lambda i: (i, C_GL // CONV_BLK)),
                  pl.BlockSpec((tm, CONV_BLK), lambda i: (i, C_GL // CONV_BLK + 1)),
                  full(b_gate), full(norm_g),
                  pl.BlockSpec((1, 1, d), lambda i: (_row_sel(i, tpb, lt, bn) * 6 + 2, 0, 0)),
                  full(wb), full(wo)],
        out_specs=tok(d),
        out_shape=jax.ShapeDtypeStruct((n, d), F32),
        compiler_params=_params(("parallel",)),
        name="merge",
    )(xs, oa, ob, og[0].reshape(n, BRANCH_W), og[1].reshape(n, BRANCH_W), proj, proj, proj,
      b_gate, norm_g, mod, wb, wo)


_CAND = [(r1, r2) for r1 in range(1, PEER_TOPK + 1) for r2 in range(1, PEER_TOPK + 1)
         if r1 * r2 <= PEER_TOPK]


def _extract_top(cur, n, want_rank=False):
    rank = jnp.full(cur.shape, 99.0, F32) if want_rank else None
    vals = []
    for r in range(n):
        m = jnp.max(cur, axis=0, keepdims=True)
        hit = cur == m
        if want_rank:
            rank = jnp.where(hit, float(r + 1), rank)
        cur = jnp.where(hit, -jnp.inf, cur)
        vals.append(m)
    return vals, rank


def _topk_body(x_ref, g_ref, sh_ref, sc_ref, wq_ref, keys_ref, h_ref, cnt1_ref, e1_ref, rk2_ref, e2_ref):
    hm = _modulate(x_ref[...], g_ref[...], sh_ref[0], sc_ref[0])
    h_ref[...] = (hm * math.sqrt(0.5)).astype(BF16)
    hmod = hm.astype(BF16)
    nr = PEER_TOPK
    for h in range(PEER_HEADS):
        st = []
        for c in range(2):
            col = (2 * h + c) * LANES
            qs = jnp.dot(hmod, wq_ref[:, col:col + LANES], preferred_element_type=F32)
            st.append(_dot3(keys_ref[c], qs, _NT))
        a, _ = _extract_top(st[0], nr)
        b, rank2 = _extract_top(st[1], nr, want_rank=True)
        rows = [a[r1 - 1] + b[r2 - 1] for r1, r2 in _CAND]
        pad = (-len(rows)) % SUBLANES
        cand = jnp.concatenate(rows + [jnp.full_like(rows[0], -jnp.inf)] * pad, axis=0)
        top, _ = _extract_top(cand, nr)
        tau = top[PEER_TOPK - 1]
        z = jnp.sum(jnp.where(cand >= tau, jnp.exp(cand - top[0]), 0.0), axis=0, keepdims=True)
        cnt1 = jnp.zeros_like(st[0])
        for r1 in range(1, PEER_TOPK + 1):
            cnt = None
            for r2 in range(1, PEER_TOPK + 1):
                if r1 * r2 <= PEER_TOPK:
                    t = (a[r1 - 1] + b[r2 - 1] >= tau).astype(F32)
                    cnt = t if cnt is None else cnt + t
            cnt1 = jnp.where(st[0] == a[r1 - 1], cnt, cnt1)
        cnt1_ref[h] = cnt1
        e1_ref[h] = jnp.exp(st[0] - a[0])
        rk2_ref[h] = rank2.astype(BF16)
        e2_ref[h] = (jnp.exp(st[1] - b[0]) * (1.0 / z)).astype(BF16)


def _peer_topk(xs, g, mod, wq, keys, bn, t_len, l_len):
    n, d = xs.shape
    tm = _pick(math.gcd(l_len, t_len), (256, 128))
    tpb, lt = t_len // tm, l_len // tm
    sel = lambda i: _row_sel(i, tpb, lt, bn)
    out = pl.BlockSpec((PEER_HEADS, N_KEYS, tm), lambda i: (0, 0, i))
    shp = lambda dt: jax.ShapeDtypeStruct((PEER_HEADS, N_KEYS, n), dt)
    return pl.pallas_call(
        _topk_body,
        grid=(n // tm,),
        in_specs=[pl.BlockSpec((tm, d), lambda i: (i, 0)),
                  pl.BlockSpec((1, d), lambda i: (0, 0)),
                  pl.BlockSpec((1, 1, d), lambda i: (sel(i) * 6 + 3, 0, 0)),
                  pl.BlockSpec((1, 1, d), lambda i: (sel(i) * 6 + 4, 0, 0)),
                  pl.BlockSpec(wq.shape, lambda i: (0, 0)),
                  pl.BlockSpec(keys.shape, lambda i: (0, 0, 0))],
        out_specs=[pl.BlockSpec((tm, d), lambda i: (i, 0)), out, out, out, out],
        out_shape=[jax.ShapeDtypeStruct((n, d), BF16), shp(F32), shp(F32), shp(BF16), shp(BF16)],
        compiler_params=_params(("parallel",)),
        name="peer_topk",
    )(xs, g, mod, mod, wq, keys)


DENSE_CHUNK = 256
DENSE_KDEPTH = 1024


def _dense_body(x_ref, h_ref, u_ref, vt_ref, cnt1_ref, e1_ref, rk2_ref, e2_ref, *rest, te):
    gt_refs, o_ref, acc_ref = rest[:-2], rest[-2], rest[-1]
    e = pl.program_id(1)

    @pl.when(e == 0)
    def _():
        acc_ref[...] = jnp.zeros_like(acc_ref)

    hs = h_ref[...]
    zero = jnp.zeros((N_KEYS, hs.shape[0]), BF16)
    ck = DENSE_CHUNK
    nck = te // ck

    def mm1(c):
        return lax.dot_general(u_ref[c * ck:(c + 1) * ck, :], hs, _NT, preferred_element_type=F32)

    def gate(c):
        parts = []
        for j in range(c * ck // N_KEYS, (c + 1) * ck // N_KEYS):
            i1 = e * (te // N_KEYS) + j
            g = zero
            for h in range(PEER_HEADS):
                cnt = cnt1_ref[h, pl.ds(i1, 1), :].astype(BF16)
                w1 = e1_ref[h, pl.ds(i1, 1), :].astype(BF16)
                g = g + jnp.where(rk2_ref[h] <= cnt, e2_ref[h], zero) * w1
            parts.append(g)
        return jnp.concatenate(parts, axis=0)

    grp = max(1, DENSE_KDEPTH // ck)
    acts = [mm1(0)]
    hids = []
    for c in range(nck):
        if c + 1 < nck:
            acts.append(mm1(c + 1))
        y = acts[c]
        hids.append((y * (1.0 + lax.erf(y))).astype(BF16) * gate(c))
        if c % grp == grp - 1:
            hid = jnp.concatenate(hids[c + 1 - grp:c + 1], axis=0)
            acc_ref[...] += jnp.dot(vt_ref[:, (c + 1 - grp) * ck:(c + 1) * ck], hid,
                                    preferred_element_type=F32)

    @pl.when(e == pl.num_programs(1) - 1)
    def _():
        base = acc_ref.shape[1] // len(gt_refs)
        for k, gt_ref in enumerate(gt_refs):
            rs = slice(k * base, (k + 1) * base)
            o_ref[rs] = x_ref[rs] + (gt_ref[0] * math.sqrt(0.5)) * acc_ref[:, rs].T


def _peer_dense(xs, hmod, u, vt, cnt1, e1, rk2, e2, mod, bn, t_len, l_len):
    n, d = xs.shape
    ne = u.shape[0]
    base = _pick(math.gcd(l_len, t_len), (256, 128))
    nsub = 2 if n % (2 * base) == 0 else 1
    tm = nsub * base
    te = _pick(ne, (2048, 1024, 512))
    tpb, lt = t_len // base, l_len // base
    gts = [pl.BlockSpec((1, 1, d), lambda i, e, k=k: (_row_sel(i * nsub + k, tpb, lt, bn) * 6 + 5, 0, 0))
           for k in range(nsub)]
    sel = pl.BlockSpec((PEER_HEADS, N_KEYS, tm), lambda i, e: (0, 0, i))
    tok = pl.BlockSpec((tm, d), lambda i, e: (i, 0))
    tab = pl.BlockSpec((te, d), lambda i, e: (e, 0))
    return pl.pallas_call(
        functools.partial(_dense_body, te=te),
        grid=(n // tm, ne // te),
        in_specs=[tok, tok, tab, pl.BlockSpec((d, te), lambda i, e: (0, e)), sel, sel, sel, sel] + gts,
        out_specs=tok,
        out_shape=jax.ShapeDtypeStruct((n, d), F32),
        scratch_shapes=[pltpu.VMEM((d, tm), F32)],
        compiler_params=_params(("parallel", "arbitrary")),
        name="peer_dense",
    )(xs, hmod, u, vt, cnt1, e1, rk2, e2, *([mod] * nsub))


def _rope_tables(rows, l_len):
    row = jnp.repeat(jnp.arange(rows), GRID_W).astype(F32)
    col = jnp.tile(jnp.arange(GRID_W), rows).astype(F32)
    inv = jnp.power(ROPE_THETA, -jnp.arange(ROPE_PAIRS, dtype=F32) / ROPE_PAIRS)
    ar = row[:, None] * inv
    ac = col[:, None] * inv
    ang = jnp.concatenate([ar, ar, ac, ac], axis=-1)
    cos = jnp.concatenate([jnp.ones((l_len, HEAD_DIM), F32), jnp.cos(ang)], axis=0)
    sin = jnp.concatenate([jnp.zeros((l_len, HEAD_DIM), F32), jnp.sin(ang)], axis=0)
    cos = jnp.tile(cos, (1, 2))
    sin = jnp.tile(sin, (1, 2))
    first = (jnp.arange(LANES) % (2 * ROPE_PAIRS)) < ROPE_PAIRS
    return cos, jnp.where(first, -sin, 0.0), jnp.where(first, 0.0, sin)


def _reorder_w_in(w):
    d = w.shape[0]
    qa_ka_va, qb, kb_vb = w[:, :1536], w[:, 1536:2048], w[:, 2048:2304]
    qkvc, zc, ab, gl = w[:, 2304:3840], w[:, 3840:4352], w[:, 4352:4368], w[:, 4368:]
    qb = qb.reshape(d, WG_KV_HEADS, WG_GROUP, HEAD_DIM).transpose(0, 2, 1, 3).reshape(d, 512)
    pad = jnp.zeros((d, C_GL - C_AB - ab.shape[1]), w.dtype)
    return jnp.concatenate([qa_ka_va, qkvc, zc, qb, kb_vb, ab, pad, gl], axis=1).astype(BF16)


def kernel(x, c, ctx, c_ctx, norm1_g, norm2_g, w_ada, b_ada, w_in, b_gate, qk_norm_g, diff_lam,
           diff_subln_g, wg_sink, gd_conv_w, gd_a_log, gd_dt_bias, gd_norm_g, w_branch, w_out,
           peer_wq, peer_keys, peer_u, peer_v):
    bn, s_len, d = x.shape
    l_len = ctx.shape[1]
    t_len = l_len + s_len
    n_tok = bn * t_len
    depth = w_in.shape[0]
    cos, sa, sb = _rope_tables(s_len // GRID_W, l_len)

    xs = jnp.concatenate([ctx, x], axis=1).reshape(n_tok, d)
    cvec = jnp.concatenate([c, c_ctx[None]], axis=0)
    cact = jnp.pad(jax.nn.silu(cvec), ((0, 16 - (bn + 1) % 16), (0, 0))).astype(BF16)
    lane_pad = lambda a: jnp.pad(a.reshape(1, -1), ((0, 0), (0, LANES - a.size)))

    for l in range(depth):
        lam_init = 0.8 - 0.6 * math.exp(-0.3 * l)
        mod = (_mm(cact, w_ada[l].astype(BF16))[:bn + 1] + b_ada[l]).reshape((bn + 1) * 6, 1, d)

        proj = _inproj(xs, norm1_g[l].reshape(1, d), mod, _reorder_w_in(w_in[l]), bn, t_len, l_len)
        qa, ka, va, qb, kb, vb, gq, gk, gv, gb = _prep(
            proj, cos, sa, sb, jnp.tile(qk_norm_g[l], (1, 2)), gd_conv_w[l],
            lane_pad(gd_a_log[l]), lane_pad(gd_dt_bias[l]), t_len, l_len)
        b3 = lambda a: a.reshape(bn, t_len, a.shape[-1])

        lv = diff_lam[l]
        lam = (jnp.exp(jnp.sum(lv[0, 0] * lv[0, 1])) - jnp.exp(jnp.sum(lv[1, 0] * lv[1, 1])) + lam_init)
        oa = _diff_attn(b3(qa), b3(ka), b3(va), lam.reshape(1), diff_subln_g[l].reshape(1, DA_VDIM),
                        1.0 - lam_init, l_len)
        ob = _win_attn(b3(qb), b3(kb), b3(vb), wg_sink[l], l_len)
        og = _gdn(b3(gq), b3(gk), b3(gv), b3(gb), l_len)

        wb = w_branch[l]
        wb1 = wb[1].reshape(WG_KV_HEADS, WG_GROUP, HEAD_DIM, d).transpose(1, 0, 2, 3).reshape(BRANCH_W, d)
        wb = jnp.stack([wb[0], wb1, wb[2]]).astype(BF16)
        xs = _merge(xs, oa.reshape(n_tok, BRANCH_W), ob.reshape(n_tok, BRANCH_W), og, proj,
                    b_gate[l].reshape(1, -1), gd_norm_g[l].reshape(1, GD_DV), mod, wb,
                    w_out[l].astype(BF16), bn, t_len, l_len)

        hmod, cnt1, e1, rk2, e2 = _peer_topk(xs, norm2_g[l].reshape(1, d), mod, peer_wq[l].astype(BF16),
                                             peer_keys[l], bn, t_len, l_len)
        xs = _peer_dense(xs, hmod, peer_u[l].astype(BF16), peer_v[l].astype(BF16).T, cnt1, e1, rk2, e2,
                         mod, bn, t_len, l_len)

    return xs.reshape(bn, t_len, d)[:, l_len:]
```

```python
import functools
import math

import jax
import jax.numpy as jnp
from jax import lax
from jax.experimental import pallas as pl
from jax.experimental.pallas import tpu as pltpu

F32 = jnp.float32
BF16 = jnp.bfloat16

EPS = 1e-6
GRID_W = 64
HEAD_DIM = 64
ROPE_THETA = 10000.0
ROPE_PAIRS = HEAD_DIM // 4
DA_HEADS = 4
DA_VDIM = 2 * HEAD_DIM
WG_HEADS = 8
WG_KV_HEADS = 2
WG_GROUP = WG_HEADS // WG_KV_HEADS
WINDOW = 128
WG_BLOCK = 128
GD_HEADS = 4
GD_DK = 128
GD_DV = 128
GD_CONV = 5
GD_CHUNK = 64
GD_QKV_W = GD_HEADS * (2 * GD_DK + GD_DV)
N_BRANCH = 3
BRANCH_W = 512
PEER_HEADS = 8
PEER_HALF = 128
N_KEYS = 128
PEER_TOPK = 16

LANES = 128
SUBLANES = 8
VMEM_LIMIT = 56 * 1024 * 1024
NEG = -1e30
LOG2E = math.log2(math.e)

C_QA, C_KA, C_VA, C_QKVC, C_ZC, C_QB, C_KB, C_VB, C_AB, C_GL = (
    0, 512, 1024, 1536, 3072, 3584, 4096, 4224, 4352, 4608)
PROJ_W = C_GL + N_BRANCH * 1024
PREP_W = C_GL
CONV_BLK = 1536


def _pick(n, cands):
    for c in cands:
        if n % c == 0:
            return c
    raise ValueError(f"no tile in {cands} divides {n}")


def _params(sem):
    return pltpu.CompilerParams(dimension_semantics=sem, vmem_limit_bytes=VMEM_LIMIT)


def _split2(a):
    hi = a.astype(BF16)
    return hi, (a - hi.astype(F32)).astype(BF16)


def _dot1(a, b, dn=None):
    a = a.astype(BF16)
    b = b.astype(BF16)
    if dn is None:
        return jnp.dot(a, b, preferred_element_type=F32)
    return lax.dot_general(a, b, dn, preferred_element_type=F32)


def _dot3(a, b, dn=None):
    ah, al = _split2(a)
    bh, bl = _split2(b)
    return _dot1(ah, bh, dn) + (_dot1(ah, bl, dn) + _dot1(al, bh, dn))


_NT = (((1,), (1,)), ((), ()))


def _row_sel(i, tpb, lt, bn):
    return jnp.where(i % tpb < lt, bn, i // tpb)


def _mm_body(x_ref, w_ref, o_ref):
    o_ref[...] = jnp.dot(x_ref[...], w_ref[...], preferred_element_type=F32)


def _mm(x, w):
    m, k = x.shape
    n = w.shape[1]
    tn = _pick(n, (1536, 1024, 512, 256, 128))
    return pl.pallas_call(
        _mm_body,
        grid=(n // tn,),
        in_specs=[pl.BlockSpec((m, k), lambda j: (0, 0)),
                  pl.BlockSpec((k, tn), lambda j: (0, j))],
        out_specs=pl.BlockSpec((m, tn), lambda j: (0, j)),
        out_shape=jax.ShapeDtypeStruct((m, n), F32),
        compiler_params=_params(("parallel",)),
        name="ada_mm",
    )(x, w)


def _modulate(x, g, sh, sc):
    h = x * lax.rsqrt(jnp.mean(x * x, axis=-1, keepdims=True) + EPS) * g
    return h * (1.0 + sc) + sh


def _inproj_body(x_ref, g_ref, sh_ref, sc_ref, w_ref, o_ref):
    h = _modulate(x_ref[...], g_ref[...], sh_ref[0], sc_ref[0])
    o_ref[...] = jnp.dot(h.astype(BF16), w_ref[...], preferred_element_type=F32)


def _inproj(xs, g, mod, w, bn, t_len, l_len):
    n, d = xs.shape
    tm = _pick(math.gcd(l_len, t_len), (512, 256, 128))
    tn = w.shape[1] // 2
    tpb, lt = t_len // tm, l_len // tm
    sel = lambda j, i: _row_sel(i, tpb, lt, bn)
    return pl.pallas_call(
        _inproj_body,
        grid=(w.shape[1] // tn, n // tm),
        in_specs=[pl.BlockSpec((tm, d), lambda j, i: (i, 0)),
                  pl.BlockSpec((1, d), lambda j, i: (0, 0)),
                  pl.BlockSpec((1, 1, d), lambda j, i: (sel(j, i) * 6, 0, 0)),
                  pl.BlockSpec((1, 1, d), lambda j, i: (sel(j, i) * 6 + 1, 0, 0)),
                  pl.BlockSpec((d, tn), lambda j, i: (0, j))],
        out_specs=pl.BlockSpec((tm, tn), lambda j, i: (i, j)),
        out_shape=jax.ShapeDtypeStruct((n, w.shape[1]), F32),
        compiler_params=_params(("parallel", "parallel")),
        name="inproj",
    )(xs, g, mod, mod, w)


def _prep_body(p_ref, prev_ref, next_ref, cos_ref, sa_ref, sb_ref, qkg_ref, cw_ref, alog_ref, dtb_ref,
               qa_ref, ka_ref, va_ref, qb_ref, kb_ref, vb_ref, gq_ref, gk_ref, gv_ref, gb_ref,
               xe_ref, *, tpb, lt):
    tm = p_ref.shape[0]
    li = lax.broadcasted_iota(jnp.int32, (LANES, LANES), 0)
    lj = lax.broadcasted_iota(jnp.int32, (LANES, LANES), 1)
    seg = jnp.where(li // HEAD_DIM == lj // HEAD_DIM, 1.0 / HEAD_DIM, 0.0).astype(BF16)
    cos = cos_ref[...]
    sa = sa_ref[...]
    sb = sb_ref[...]

    def normrope(x, gain):
        yh, yl = _split2(x * x)
        ms = jnp.dot(yh, seg, preferred_element_type=F32) + jnp.dot(yl, seg, preferred_element_type=F32)
        xn = x * lax.rsqrt(ms + EPS) * gain
        return xn * cos + pltpu.roll(xn, LANES - ROPE_PAIRS, 1) * sa + pltpu.roll(xn, ROPE_PAIRS, 1) * sb

    scale = HEAD_DIM ** -0.5 * LOG2E
    for c in range(4):
        cs = slice(c * LANES, (c + 1) * LANES)
        qa_ref[:, cs] = (normrope(p_ref[:, C_QA + c * LANES:C_QA + (c + 1) * LANES], qkg_ref[0:1]) * scale).astype(BF16)
        ka_ref[:, cs] = normrope(p_ref[:, C_KA + c * LANES:C_KA + (c + 1) * LANES], qkg_ref[1:2]).astype(BF16)
        qb_ref[:, cs] = (normrope(p_ref[:, C_QB + c * LANES:C_QB + (c + 1) * LANES], qkg_ref[2:3]) * scale).astype(BF16)
    kb_ref[...] = normrope(p_ref[:, C_KB:C_KB + LANES], qkg_ref[3:4]).astype(BF16)
    va_ref[...] = p_ref[:, C_VA:C_VA + 512].astype(BF16)
    vb_ref[...] = p_ref[:, C_VB:C_VB + LANES].astype(BF16)

    ti = pl.program_id(0) % tpb
    at_start = (ti == 0) | (ti == lt)
    at_end = (ti == lt - 1) | (ti == tpb - 1)
    xe_ref[0:SUBLANES] = jnp.where(at_start, 0.0, prev_ref[...])
    xe_ref[SUBLANES:SUBLANES + tm] = p_ref[:, C_QKVC:C_QKVC + GD_QKV_W]
    xe_ref[SUBLANES + tm:2 * SUBLANES + tm] = jnp.where(at_end, 0.0, next_ref[...])
    half = GD_CONV // 2
    y = None
    for i in range(GD_CONV):
        t = xe_ref[pl.ds(SUBLANES - half + i, tm), :] * cw_ref[i:i + 1]
        y = t if y is None else y + t
    y = y * jax.nn.sigmoid(y)
    for h in range(GD_HEADS):
        hs = slice(h * LANES, (h + 1) * LANES)
        q = y[:, h * LANES:(h + 1) * LANES]
        k = y[:, 512 + h * LANES:512 + (h + 1) * LANES]
        gq_ref[:, hs] = q * lax.rsqrt(jnp.sum(q * q, axis=-1, keepdims=True) + EPS) * (GD_DK ** -0.5)
        gk_ref[:, hs] = k * lax.rsqrt(jnp.sum(k * k, axis=-1, keepdims=True) + EPS)
    gv_ref[...] = y[:, 1024:]

    ab = p_ref[:, C_AB:C_AB + LANES]
    lane = lax.broadcasted_iota(jnp.int32, ab.shape, 1)
    gdec = -jnp.exp(alog_ref[...]) * jnp.logaddexp(ab + dtb_ref[...], 0.0)
    gb_ref[...] = jnp.where(lane < 2 * GD_HEADS, gdec, jax.nn.sigmoid(ab))


def _prep(proj, cos, sa, sb, qkg, cw, alog, dtb, t_len, l_len):
    n = proj.shape[0]
    tm = _pick(math.gcd(l_len, t_len), (256, 128))
    tpb, lt = t_len // tm, l_len // tm
    r8 = tm // SUBLANES
    nb8 = n // SUBLANES
    tok = lambda w: pl.BlockSpec((tm, w), lambda i: (i, 0))
    rope = pl.BlockSpec((tm, LANES), lambda i: (i % tpb, 0))
    full = lambda a: pl.BlockSpec(a.shape, lambda i: (0,) * a.ndim)
    shp = lambda w, dt: jax.ShapeDtypeStruct((n, w), dt)
    return pl.pallas_call(
        functools.partial(_prep_body, tpb=tpb, lt=lt),
        grid=(n // tm,),
        in_specs=[pl.BlockSpec((tm, PREP_W), lambda i: (i, 0)),
                  pl.BlockSpec((SUBLANES, CONV_BLK), lambda i: (jnp.maximum(i * r8 - 1, 0), 1)),
                  pl.BlockSpec((SUBLANES, CONV_BLK), lambda i: (jnp.minimum((i + 1) * r8, nb8 - 1), 1)),
                  rope, rope, rope, full(qkg), full(cw), full(alog), full(dtb)],
        out_specs=[tok(512), tok(512), tok(512), tok(512), tok(LANES), tok(LANES),
                   tok(512), tok(512), tok(512), tok(LANES)],
        out_shape=[shp(512, BF16), shp(512, BF16), shp(512, BF16), shp(512, BF16), shp(LANES, BF16),
                   shp(LANES, BF16), shp(512, F32), shp(512, F32), shp(512, F32), shp(LANES, F32)],
        scratch_shapes=[pltpu.VMEM((tm + 2 * SUBLANES, GD_QKV_W), F32)],
        compiler_params=_params(("parallel",)),
        name="prep",
    )(proj, proj, proj, cos, sa, sb, qkg, cw, alog, dtb)


def _diff_body(lam_ref, q_ref, k_ref, v_ref, g_ref, o_ref, *, post, l_len, n_ctx):
    q = q_ref[0]
    lane = lax.broadcasted_iota(jnp.int32, q.shape, 1)
    zero = jnp.zeros_like(q)
    q1 = jnp.where(lane < HEAD_DIM, q, zero)
    q2 = jnp.where(lane >= HEAD_DIM, q, zero)

    def run(k, v):
        def attend(s):
            p = jnp.exp2(s - jnp.max(s, axis=-1, keepdims=True))
            l = jnp.sum(p, axis=-1, keepdims=True)
            return jnp.dot(p.astype(BF16), v, preferred_element_type=F32) * (1.0 / l)

        s1 = lax.dot_general(q1, k, _NT, preferred_element_type=F32)
        s2 = lax.dot_general(q2, k, _NT, preferred_element_type=F32)
        o = attend(s1) - lam_ref[0] * attend(s2)
        ms = jnp.mean(o * o, axis=-1, keepdims=True)
        o_ref[0] = o * lax.rsqrt(ms + EPS) * (g_ref[...] * post)

    i = pl.program_id(2)

    @pl.when(i < n_ctx)
    def _():
        run(k_ref[0, :l_len], v_ref[0, :l_len])

    @pl.when(i >= n_ctx)
    def _():
        run(k_ref[0], v_ref[0])


def _diff_attn(q, k, v, lam, subln_g, post, l_len):
    b, t, _ = q.shape
    tq = _pick(math.gcd(l_len, t), (256, 128))
    return pl.pallas_call(
        functools.partial(_diff_body, post=post, l_len=l_len, n_ctx=l_len // tq),
        grid=(b, DA_HEADS, t // tq),
        in_specs=[pl.BlockSpec(memory_space=pltpu.SMEM),
                  pl.BlockSpec((1, tq, LANES), lambda bi, h, i: (bi, i, h)),
                  pl.BlockSpec((1, t, LANES), lambda bi, h, i: (bi, 0, h)),
                  pl.BlockSpec((1, t, LANES), lambda bi, h, i: (bi, 0, h)),
                  pl.BlockSpec((1, LANES), lambda bi, h, i: (0, 0))],
        out_specs=pl.BlockSpec((1, tq, LANES), lambda bi, h, i: (bi, i, h)),
        out_shape=jax.ShapeDtypeStruct((b, t, DA_HEADS * DA_VDIM), F32),
        compiler_params=_params(("parallel", "parallel", "parallel")),
        name="diff_attn",
    )(lam, q, k, v, subln_g)


def _win_body(sink_ref, q_ref, kc_ref, vc_ref, kp_ref, kn_ref, kx_ref, vp_ref, vn_ref, vx_ref, o_ref,
              *, lo, s_len):
    l_len = kc_ref.shape[1]
    q = q_ref[0]
    lane = lax.broadcasted_iota(jnp.int32, (WG_BLOCK, LANES), 1)

    def run(kcat, vcat, valid):
        vlane = lax.broadcasted_iota(jnp.int32, vcat.shape, 1)
        vz = jnp.zeros_like(vcat)
        vhalf = [jnp.where(vlane < HEAD_DIM, vcat, vz), jnp.where(vlane >= HEAD_DIM, vcat, vz)]
        heads = [(g, kv) for g in range(WG_GROUP) for kv in range(WG_KV_HEADS)]
        zq = jnp.zeros((WG_BLOCK, LANES), q.dtype)
        half = [lane < HEAD_DIM, lane >= HEAD_DIM]
        qms = [jnp.where(half[kv], q[:, g * LANES:(g + 1) * LANES], zq) for g, kv in heads]
        ss = [lax.dot_general(qm, kcat, _NT, preferred_element_type=F32) for qm in qms]
        if valid is not None:
            ss = [jnp.where(valid, s, NEG) for s in ss]
        sks = [sink_ref[kv * WG_GROUP + g] * LOG2E for g, kv in heads]
        ms = [jnp.maximum(jnp.max(s, axis=-1, keepdims=True), sk) for s, sk in zip(ss, sks)]
        ps = [jnp.exp2(s - m) for s, m in zip(ss, ms)]
        dens = [jnp.sum(p, axis=-1, keepdims=True) + jnp.exp2(sk - m) for p, sk, m in zip(ps, sks, ms)]
        os = [jnp.dot(p.astype(BF16), vhalf[kv], preferred_element_type=F32) * (1.0 / den)
              for p, den, (g, kv) in zip(ps, dens, heads)]
        for g in range(WG_GROUP):
            o_ref[0, :, g * LANES:(g + 1) * LANES] = os[2 * g] + os[2 * g + 1]

    n = pl.program_id(1)

    @pl.when(n < lo)
    def _():
        run(kc_ref[0], vc_ref[0], None)

    @pl.when(n >= lo)
    def _():
        kcat = jnp.concatenate([kc_ref[0], kp_ref[0], kn_ref[0], kx_ref[0]], axis=0)
        vcat = jnp.concatenate([vc_ref[0], vp_ref[0], vn_ref[0], vx_ref[0]], axis=0)
        nk = kcat.shape[0]
        col = lax.broadcasted_iota(jnp.int32, (WG_BLOCK, nk), 1)
        row = lax.broadcasted_iota(jnp.int32, (WG_BLOCK, nk), 0)
        j = col - l_len
        rel = j - WG_BLOCK - row
        kpos = (n - lo) * WG_BLOCK + j - WG_BLOCK
        valid = (col < l_len) | ((jnp.abs(rel) <= WINDOW) & (kpos >= 0) & (kpos < s_len))
        run(kcat, vcat, valid)


def _win_attn(q, k, v, sink, l_len):
    b, t, _ = q.shape
    blk = WG_BLOCK
    lo, nb = l_len // blk, t // blk
    ctx_spec = pl.BlockSpec((1, l_len, LANES), lambda bi, n: (bi, 0, 0))
    win = [pl.BlockSpec((1, blk, LANES), lambda bi, n, d=d: (bi, jnp.clip(n + d, lo, nb - 1), 0))
           for d in (-1, 0, 1)]
    return pl.pallas_call(
        functools.partial(_win_body, lo=lo, s_len=t - l_len),
        grid=(b, nb),
        in_specs=[pl.BlockSpec(memory_space=pltpu.SMEM),
                  pl.BlockSpec((1, blk, 4 * LANES), lambda bi, n: (bi, n, 0)),
                  ctx_spec, ctx_spec] + win + win,
        out_specs=pl.BlockSpec((1, blk, 4 * LANES), lambda bi, n: (bi, n, 0)),
        out_shape=jax.ShapeDtypeStruct((b, t, 4 * LANES), F32),
        compiler_params=_params(("parallel", "parallel")),
        name="win_attn",
    )(sink, q, k, v, k, k, k, v, v, v)


GD_LOCAL_CHUNKS = 4


def _dot3s(x, r):
    m = x.shape[0]
    xh, xl = _split2(x)
    rh, rl = _split2(r)
    t = jnp.dot(jnp.concatenate([xh, xl], axis=0), rh, preferred_element_type=F32)
    return t[:m] + t[m:] + jnp.dot(xh, rl, preferred_element_type=F32)


def _gdn_local_body(q_ref, k_ref, v_ref, gb_ref, u_ref, w_ref, qg_ref, qk_ref, kdt_ref, egl_ref, *, nch):
    c = GD_CHUNK
    nh = GD_HEADS
    ii = lax.broadcasted_iota(jnp.int32, (c, nh * c), 0)
    jl = lax.broadcasted_iota(jnp.int32, (c, nh * c), 1)
    jj = jl % c
    blk = jl // c
    eye = ii == jj
    eyef = eye.astype(F32)
    tri = [(jj <= ii, jj < ii, jj >= ii), (jj >= ii, jj > ii, jj <= ii)]
    hl = lax.broadcasted_iota(jnp.int32, (c, nh * LANES), 1) // LANES
    e_i = lax.broadcasted_iota(jnp.int32, (LANES, LANES), 0)
    e_j = lax.broadcasted_iota(jnp.int32, (LANES, LANES), 1)
    eye_b = (e_i == e_j).astype(BF16)

    def cat(cols):
        out = jnp.zeros((c, nh * c), F32)
        for h in range(nh):
            out = jnp.where(blk == h, cols[h], out)
        return out

    def nat(cols):
        out = jnp.zeros((c, nh * LANES), F32)
        for h in range(nh):
            out = jnp.where(hl == h, cols[h], out)
        return out

    def bd_cat(p):
        return jnp.concatenate([jnp.where(blk == h, p, 0.0) for h in range(nh)], axis=0)

    def bd_nat(x):
        return jnp.concatenate([jnp.where(hl == h, x, 0.0) for h in range(nh)], axis=0)

    def setup(ch):
        rows = slice(ch * c, (ch + 1) * c)
        gb = gb_ref[0, rows, :]
        q = q_ref[0, rows, :]
        k = k_ref[0, rows, :]
        v = v_ref[0, rows, :]
        kdh, kdl = _split2(bd_nat(k))
        out = []
        for d in range(2):
            incl, strict, incl_t = tri[d]
            g_cols = [gb[:, d * nh + h:d * nh + h + 1] for h in range(nh)]
            b_cols = [gb[:, (2 + d) * nh + h:(2 + d) * nh + h + 1] for h in range(nh)]
            g_cat = cat(g_cols)
            g_row = jnp.sum(jnp.where(eye, g_cat, 0.0), axis=0, keepdims=True)
            gc_cols = [jnp.sum(jnp.where(incl & (blk == h), g_row, 0.0), axis=1, keepdims=True)
                       for h in range(nh)]
            grow = jnp.sum(jnp.where(incl_t, g_cat, 0.0), axis=0, keepdims=True)
            glast = [jnp.sum(g_cols[h], axis=0, keepdims=True) for h in range(nh)]
            decay = jnp.exp(jnp.where(incl, cat(gc_cols) - grow, NEG))
            b_nat = nat(b_cols)
            eg_nat = nat([jnp.exp(gc_cols[h]) for h in range(nh)])
            kb = k * b_nat
            kbh, kbl = _split2(kb)
            qg_ref[0, d, rows, :] = q * eg_nat
            kd = (k * nat([jnp.exp(glast[h] - gc_cols[h]) for h in range(nh)])).astype(BF16)
            egl_ref[0, d, ch] = nat([jnp.exp(glast[h]) for h in range(nh)])[0:1]
            out.append(dict(rows=rows, ch=ch, d=d, q=q, decay=decay, strict=strict, kbh=kbh, kbl=kbl,
                            kdh=kdh, kdl=kdl, kd=kd, vb=v * b_nat, kbeg=kb * eg_nat))
        return out

    st = [s for ch in range(nch) for s in setup(ch)]
    kk1 = [lax.dot_general(jnp.concatenate([s["kbh"], s["kbl"]], axis=0), s["kdh"], _NT,
                           preferred_element_type=F32) for s in st]
    kk2 = [lax.dot_general(s["kbh"], s["kdl"], _NT, preferred_element_type=F32) for s in st]
    a = [jnp.where(s["strict"], (x[:c] + x[c:] + y) * s["decay"], 0.0) for x, y, s in zip(kk1, kk2, st)]
    tinv = [eyef - x for x in a]
    p = [_dot3s(x, bd_cat(x)) for x in a]
    for _ in range(4):
        t = [_dot3s(jnp.concatenate([ti, pi], axis=0), bd_cat(pi)) for ti, pi in zip(tinv, p)]
        tinv = [ti + x[:c] for ti, x in zip(tinv, t)]
        p = [x[c:] for x in t]
    tinv = [ti + _dot3s(ti, bd_cat(pi)) for ti, pi in zip(tinv, p)]
    us = [_dot3s(ti, bd_nat(s["vb"])) for ti, s in zip(tinv, st)]
    ws = [_dot3s(ti, bd_nat(s["kbeg"])) for ti, s in zip(tinv, st)]
    qks = [lax.dot_general(s["q"].astype(BF16), s["kdh"], _NT, preferred_element_type=F32) for s in st]
    kdts = [jnp.concatenate([lax.dot_general(eye_b, s["kd"][:, h * LANES:(h + 1) * LANES], _NT,
                                             preferred_element_type=F32) for h in range(nh)], axis=0)
            for s in st]
    for s, u, w, qk, kdt in zip(st, us, ws, qks, kdts):
        u_ref[0, s["d"], s["rows"], :] = u
        w_ref[0, s["d"], s["rows"], :] = w
        qk_ref[0, s["d"], s["ch"]] = qk * s["decay"]
        kdt_ref[0, s["d"], s["ch"]] = kdt.astype(BF16)


def _gdn_scan_body(*refs):
    ins, o_refs, s_ref = (refs[:6], refs[6:12]), refs[12:14], refs[14]

    @pl.when(pl.program_id(1) == 0)
    def _():
        s_ref[...] = jnp.zeros_like(s_ref)

    c = GD_CHUNK
    heads = range(GD_HEADS)
    sls = [slice(h * LANES, (h + 1) * LANES) for h in heads]
    hl = lax.broadcasted_iota(jnp.int32, (c, GD_HEADS * LANES), 1) // LANES
    chains = [(d, h) for d in range(2) for h in heads]
    ns = ins[0][3].shape[2]
    ss = [s_ref[d, h] for d, h in chains]
    for step in range(ns):
        pos = [step, ns - 1 - step]
        rows = [slice(p * c, (p + 1) * c) for p in pos]
        rs = [_dot1(jnp.concatenate([ins[d][1][0, 0, rows[d], sls[h]], ins[d][2][0, 0, rows[d], sls[h]]],
                                    axis=0), s) for (d, h), s in zip(chains, ss)]
        v_news = [ins[d][0][0, 0, rows[d], sls[h]] - r[:c] for (d, h), r in zip(chains, rs)]
        upd = [_dot1(ins[d][4][0, 0, pos[d], sls[h], :], vn) for (d, h), vn in zip(chains, v_news)]
        for d in range(2):
            lo = d * GD_HEADS
            v_all = jnp.concatenate(v_news[lo:lo + GD_HEADS], axis=1)
            vbd = jnp.concatenate([jnp.where(hl == h, v_all, 0.0) for h in heads], axis=0)
            o_refs[d][0, rows[d], :] = (jnp.concatenate([r[c:] for r in rs[lo:lo + GD_HEADS]], axis=1)
                                        + _dot1(ins[d][3][0, 0, pos[d]], vbd))
        ss = [s * ins[d][5][0, 0, pos[d], :, sls[h]] + up for (d, h), s, up in zip(chains, ss, upd)]
    for (d, h), s in zip(chains, ss):
        s_ref[d, h] = s


def _gdn(q, k, v, gb, l_len):
    b, t, _ = q.shape
    c = GD_CHUNK
    nc, ncc = t // c, l_len // c
    nch = _pick(nc, (GD_LOCAL_CHUNKS, 1))
    h = GD_HEADS
    hw = h * LANES
    tokl = pl.BlockSpec((1, nch * c, hw), lambda bi, j: (bi, j, 0))
    tok4 = pl.BlockSpec((1, 2, nch * c, hw), lambda bi, j: (bi, 0, j, 0))
    ch5 = lambda r, w2: pl.BlockSpec((1, 2, nch, r, w2), lambda bi, j: (bi, 0, j, 0, 0))
    tok_shape = jax.ShapeDtypeStruct((b, 2, t, hw), F32)
    u, w, qg, qk, kdt, egl = pl.pallas_call(
        functools.partial(_gdn_local_body, nch=nch),
        grid=(b, nc // nch),
        in_specs=[tokl, tokl, tokl, pl.BlockSpec((1, nch * c, LANES), lambda bi, j: (bi, j, 0))],
        out_specs=[tok4, tok4, tok4, ch5(c, h * c), ch5(hw, c), ch5(1, hw)],
        out_shape=[tok_shape, tok_shape, tok_shape,
                   jax.ShapeDtypeStruct((b, 2, nc, c, h * c), F32),
                   jax.ShapeDtypeStruct((b, 2, nc, hw, c), BF16),
                   jax.ShapeDtypeStruct((b, 2, nc, 1, hw), F32)],
        compiler_params=_params(("parallel", "parallel")),
        name="gdn_local",
    )(q, k, v, gb)

    ns = 2 if (ncc % 2 == 0 and nc % 2 == 0) else 1

    def blk(d, j):
        first = j * ns
        last = jnp.where(first < ncc, ncc - 1 - first, nc - 1 - first + ncc) - (ns - 1)
        return j if d == 0 else last // ns

    def specs(d):
        stok = pl.BlockSpec((1, 1, ns * c, hw), lambda bi, j: (bi, d, blk(d, j), 0))
        sch = lambda r, w2: pl.BlockSpec((1, 1, ns, r, w2), lambda bi, j: (bi, d, blk(d, j), 0, 0))
        return [stok, stok, stok, sch(c, h * c), sch(hw, c), sch(1, hw)]

    out_spec = lambda d: pl.BlockSpec((1, ns * c, hw), lambda bi, j: (bi, blk(d, j), 0))
    out_shape = jax.ShapeDtypeStruct((b, t, hw), F32)
    return pl.pallas_call(
        _gdn_scan_body,
        grid=(b, nc // ns),
        in_specs=specs(0) + specs(1),
        out_specs=[out_spec(0), out_spec(1)],
        out_shape=[out_shape, out_shape],
        scratch_shapes=[pltpu.VMEM((2, h, GD_DK, GD_DV), F32)],
        compiler_params=_params(("parallel", "arbitrary")),
        name="gdn_scan",
    )(u, w, qg, qk, kdt, egl, u, w, qg, qk, kdt, egl)


def _merge_body(x_ref, oa_ref, ob_ref, og0_ref, og1_ref, z_ref, gl0_ref, gl1_ref, bg_ref, ng_ref,
                gt_ref, wb_ref, wo_ref, o_ref):
    d = x_ref.shape[1]
    parts = []
    for h in range(GD_HEADS):
        sl = slice(h * LANES, (h + 1) * LANES)
        o = og0_ref[:, sl] + og1_ref[:, sl]
        z = z_ref[:, sl]
        o = o * lax.rsqrt(jnp.mean(o * o, axis=-1, keepdims=True) + EPS) * ng_ref[...]
        parts.append(o * (z * jax.nn.sigmoid(z)))
    oc = jnp.concatenate(parts, axis=1)
    gl = jnp.concatenate([gl0_ref[...], gl1_ref[...]], axis=1)
    merged = None
    for r, o_r in enumerate((oa_ref[...], ob_ref[...], oc)):
        y = jnp.dot(o_r.astype(BF16), wb_ref[r], preferred_element_type=F32)
        t = jax.nn.sigmoid(gl[:, r * d:(r + 1) * d] + bg_ref[:, r * d:(r + 1) * d]) * y
        merged = t if merged is None else merged + t
    m = jnp.dot(merged.astype(BF16), wo_ref[...], preferred_element_type=F32)
    o_ref[...] = x_ref[...] + gt_ref[0] * m


def _merge(xs, oa, ob, og, proj, b_gate, norm_g, mod, wb, wo, bn, t_len, l_len):
    n, d = xs.shape
    tm = _pick(math.gcd(l_len, t_len), (256, 128))
    tpb, lt = t_len // tm, l_len // tm
    tok = lambda w: pl.BlockSpec((tm, w), lambda i: (i, 0))
    full = lambda a: pl.BlockSpec(a.shape, lambda i: (0,) * a.ndim)
    return pl.pallas_call(
        _merge_body,
        grid=(n // tm,),
        in_specs=[tok(d), tok(BRANCH_W), tok(BRANCH_W), tok(BRANCH_W), tok(BRANCH_W),
                  pl.BlockSpec((tm, BRANCH_W), lambda i: (i, C_ZC // BRANCH_W)),
                  pl.BlockSpec((tm, CONV_BLK), lambda i: (i, C_GL // CONV_BLK)),
                  pl.BlockSpec((tm, CONV_BLK), lambda i: (i, C_GL // CONV_BLK + 1)),
                  full(b_gate), full(norm_g),
                  pl.BlockSpec((1, 1, d), lambda i: (_row_sel(i, tpb, lt, bn) * 6 + 2, 0, 0)),
                  full(wb), full(wo)],
        out_specs=tok(d),
        out_shape=jax.ShapeDtypeStruct((n, d), F32),
        compiler_params=_params(("parallel",)),
        name="merge",
    )(xs, oa, ob, og[0].reshape(n, BRANCH_W), og[1].reshape(n, BRANCH_W), proj, proj, proj,
      b_gate, norm_g, mod, wb, wo)


_CAND = [(r1, r2) for r1 in range(1, PEER_TOPK + 1) for r2 in range(1, PEER_TOPK + 1)
         if r1 * r2 <= PEER_TOPK]


def _extract_top(cur, n, want_rank=False):
    rank = jnp.full(cur.shape, 99.0, F32) if want_rank else None
    vals = []
    for r in range(n):
        m = jnp.max(cur, axis=0, keepdims=True)
        hit = cur == m
        if want_rank:
            rank = jnp.where(hit, float(r + 1), rank)
        cur = jnp.where(hit, -jnp.inf, cur)
        vals.append(m)
    return vals, rank


def _topk_body(x_ref, g_ref, sh_ref, sc_ref, wq_ref, keys_ref, h_ref, cnt1_ref, e1_ref, rk2_ref, e2_ref):
    hm = _modulate(x_ref[...], g_ref[...], sh_ref[0], sc_ref[0])
    h_ref[...] = (hm * math.sqrt(0.5)).astype(BF16)
    hmod = hm.astype(BF16)
    nr = PEER_TOPK
    for h in range(PEER_HEADS):
        st = []
        for c in range(2):
            col = (2 * h + c) * LANES
            qs = jnp.dot(hmod, wq_ref[:, col:col + LANES], preferred_element_type=F32)
            st.append(_dot3(keys_ref[c], qs, _NT))
        a, _ = _extract_top(st[0], nr)
        b, rank2 = _extract_top(st[1], nr, want_rank=True)
        rows = [a[r1 - 1] + b[r2 - 1] for r1, r2 in _CAND]
        pad = (-len(rows)) % SUBLANES
        cand = jnp.concatenate(rows + [jnp.full_like(rows[0], -jnp.inf)] * pad, axis=0)
        top, _ = _extract_top(cand, nr)
        tau = top[PEER_TOPK - 1]
        z = jnp.sum(jnp.where(cand >= tau, jnp.exp(cand - top[0]), 0.0), axis=0, keepdims=True)
        cnt1 = jnp.zeros_like(st[0])
        for r1 in range(1, PEER_TOPK + 1):
            cnt = None
            for r2 in range(1, PEER_TOPK + 1):
                if r1 * r2 <= PEER_TOPK:
                    t = (a[r1 - 1] + b[r2 - 1] >= tau).astype(F32)
                    cnt = t if cnt is None else cnt + t
            cnt1 = jnp.where(st[0] == a[r1 - 1], cnt, cnt1)
        cnt1_ref[h] = cnt1
        e1_ref[h] = jnp.exp(st[0] - a[0])
        rk2_ref[h] = rank2.astype(BF16)
        e2_ref[h] = (jnp.exp(st[1] - b[0]) * (1.0 / z)).astype(BF16)


def _peer_topk(xs, g, mod, wq, keys, bn, t_len, l_len):
    n, d = xs.shape
    tm = _pick(math.gcd(l_len, t_len), (256, 128))
    tpb, lt = t_len // tm, l_len // tm
    sel = lambda i: _row_sel(i, tpb, lt, bn)
    out = pl.BlockSpec((PEER_HEADS, N_KEYS, tm), lambda i: (0, 0, i))
    shp = lambda dt: jax.ShapeDtypeStruct((PEER_HEADS, N_KEYS, n), dt)
    return pl.pallas_call(
        _topk_body,
        grid=(n // tm,),
        in_specs=[pl.BlockSpec((tm, d), lambda i: (i, 0)),
                  pl.BlockSpec((1, d), lambda i: (0, 0)),
                  pl.BlockSpec((1, 1, d), lambda i: (sel(i) * 6 + 3, 0, 0)),
                  pl.BlockSpec((1, 1, d), lambda i: (sel(i) * 6 + 4, 0, 0)),
                  pl.BlockSpec(wq.shape, lambda i: (0, 0)),
                  pl.BlockSpec(keys.shape, lambda i: (0, 0, 0))],
        out_specs=[pl.BlockSpec((tm, d), lambda i: (i, 0)), out, out, out, out],
        out_shape=[jax.ShapeDtypeStruct((n, d), BF16), shp(F32), shp(F32), shp(BF16), shp(BF16)],
        compiler_params=_params(("parallel",)),
        name="peer_topk",
    )(xs, g, mod, mod, wq, keys)


DENSE_CHUNK = 256
DENSE_KDEPTH = 1024


def _dense_body(x_ref, h_ref, u_ref, vt_ref, cnt1_ref, e1_ref, rk2_ref, e2_ref, *rest, te):
    gt_refs, o_ref, acc_ref = rest[:-2], rest[-2], rest[-1]
    e = pl.program_id(1)

    @pl.when(e == 0)
    def _():
        acc_ref[...] = jnp.zeros_like(acc_ref)

    hs = h_ref[...]
    zero = jnp.zeros((N_KEYS, hs.shape[0]), BF16)
    ck = DENSE_CHUNK
    nck = te // ck

    def mm1(c):
        return lax.dot_general(u_ref[c * ck:(c + 1) * ck, :], hs, _NT, preferred_element_type=F32)

    def gate(c):
        parts = []
        for j in range(c * ck // N_KEYS, (c + 1) * ck // N_KEYS):
            i1 = e * (te // N_KEYS) + j
            g = zero
            for h in range(PEER_HEADS):
                cnt = cnt1_ref[h, pl.ds(i1, 1), :].astype(BF16)
                w1 = e1_ref[h, pl.ds(i1, 1), :].astype(BF16)
                g = g + jnp.where(rk2_ref[h] <= cnt, e2_ref[h], zero) * w1
            parts.append(g)
        return jnp.concatenate(parts, axis=0)

    def mm2(c0, c1):
        hid = jnp.concatenate(hids[c0:c1], axis=0)
        acc_ref[...] += jnp.dot(vt_ref[:, c0 * ck:c1 * ck], hid, preferred_element_type=F32)

    grp = max(1, DENSE_KDEPTH // ck)
    acts = [mm1(c) for c in range(min(2, nck))]
    hids = []
    pending = None
    for c in range(nck):
        if c + 2 < nck:
            acts.append(mm1(c + 2))
        y = acts[c]
        hids.append((y * (1.0 + lax.erf(y))).astype(BF16) * gate(c))
        if pending is not None:
            mm2(*pending)
            pending = None
        if c % grp == grp - 1:
            pending = (c + 1 - grp, c + 1)
    mm2(*pending)

    @pl.when(e == pl.num_programs(1) - 1)
    def _():
        base = acc_ref.shape[1] // len(gt_refs)
        for k, gt_ref in enumerate(gt_refs):
            rs = slice(k * base, (k + 1) * base)
            o_ref[rs] = x_ref[rs] + (gt_ref[0] * math.sqrt(0.5)) * acc_ref[:, rs].T


def _peer_dense(xs, hmod, u, vt, cnt1, e1, rk2, e2, mod, bn, t_len, l_len):
    n, d = xs.shape
    ne = u.shape[0]
    base = _pick(math.gcd(l_len, t_len), (256, 128))
    nsub = 2 if n % (2 * base) == 0 else 1
    tm = nsub * base
    te = _pick(ne, (2048, 1024, 512))
    tpb, lt = t_len // base, l_len // base
    gts = [pl.BlockSpec((1, 1, d), lambda i, e, k=k: (_row_sel(i * nsub + k, tpb, lt, bn) * 6 + 5, 0, 0))
           for k in range(nsub)]
    sel = pl.BlockSpec((PEER_HEADS, N_KEYS, tm), lambda i, e: (0, 0, i))
    tok = pl.BlockSpec((tm, d), lambda i, e: (i, 0))
    tab = pl.BlockSpec((te, d), lambda i, e: (e, 0))
    return pl.pallas_call(
        functools.partial(_dense_body, te=te),
        grid=(n // tm, ne // te),
        in_specs=[tok, tok, tab, pl.BlockSpec((d, te), lambda i, e: (0, e)), sel, sel, sel, sel] + gts,
        out_specs=tok,
        out_shape=jax.ShapeDtypeStruct((n, d), F32),
        scratch_shapes=[pltpu.VMEM((d, tm), F32)],
        compiler_params=_params(("parallel", "arbitrary")),
        name="peer_dense",
    )(xs, hmod, u, vt, cnt1, e1, rk2, e2, *([mod] * nsub))


def _rope_tables(rows, l_len):
    row = jnp.repeat(jnp.arange(rows), GRID_W).astype(F32)
    col = jnp.tile(jnp.arange(GRID_W), rows).astype(F32)
    inv = jnp.power(ROPE_THETA, -jnp.arange(ROPE_PAIRS, dtype=F32) / ROPE_PAIRS)
    ar = row[:, None] * inv
    ac = col[:, None] * inv
    ang = jnp.concatenate([ar, ar, ac, ac], axis=-1)
    cos = jnp.concatenate([jnp.ones((l_len, HEAD_DIM), F32), jnp.cos(ang)], axis=0)
    sin = jnp.concatenate([jnp.zeros((l_len, HEAD_DIM), F32), jnp.sin(ang)], axis=0)
    cos = jnp.tile(cos, (1, 2))
    sin = jnp.tile(sin, (1, 2))
    first = (jnp.arange(LANES) % (2 * ROPE_PAIRS)) < ROPE_PAIRS
    return cos, jnp.where(first, -sin, 0.0), jnp.where(first, 0.0, sin)


def _reorder_w_in(w):
    d = w.shape[0]
    qa_ka_va, qb, kb_vb = w[:, :1536], w[:, 1536:2048], w[:, 2048:2304]
    qkvc, zc, ab, gl = w[:, 2304:3840], w[:, 3840:4352], w[:, 4352:4368], w[:, 4368:]
    qb = qb.reshape(d, WG_KV_HEADS, WG_GROUP, HEAD_DIM).transpose(0, 2, 1, 3).reshape(d, 512)
    pad = jnp.zeros((d, C_GL - C_AB - ab.shape[1]), w.dtype)
    return jnp.concatenate([qa_ka_va, qkvc, zc, qb, kb_vb, ab, pad, gl], axis=1).astype(BF16)


def kernel(x, c, ctx, c_ctx, norm1_g, norm2_g, w_ada, b_ada, w_in, b_gate, qk_norm_g, diff_lam,
           diff_subln_g, wg_sink, gd_conv_w, gd_a_log, gd_dt_bias, gd_norm_g, w_branch, w_out,
           peer_wq, peer_keys, peer_u, peer_v):
    bn, s_len, d = x.shape
    l_len = ctx.shape[1]
    t_len = l_len + s_len
    n_tok = bn * t_len
    depth = w_in.shape[0]
    cos, sa, sb = _rope_tables(s_len // GRID_W, l_len)

    xs = jnp.concatenate([ctx, x], axis=1).reshape(n_tok, d)
    cvec = jnp.concatenate([c, c_ctx[None]], axis=0)
    cact = jnp.pad(jax.nn.silu(cvec), ((0, 16 - (bn + 1) % 16), (0, 0))).astype(BF16)
    lane_pad = lambda a: jnp.pad(a.reshape(1, -1), ((0, 0), (0, LANES - a.size)))

    for l in range(depth):
        lam_init = 0.8 - 0.6 * math.exp(-0.3 * l)
        mod = (_mm(cact, w_ada[l].astype(BF16))[:bn + 1] + b_ada[l]).reshape((bn + 1) * 6, 1, d)

        proj = _inproj(xs, norm1_g[l].reshape(1, d), mod, _reorder_w_in(w_in[l]), bn, t_len, l_len)
        qa, ka, va, qb, kb, vb, gq, gk, gv, gb = _prep(
            proj, cos, sa, sb, jnp.tile(qk_norm_g[l], (1, 2)), gd_conv_w[l],
            lane_pad(gd_a_log[l]), lane_pad(gd_dt_bias[l]), t_len, l_len)
        b3 = lambda a: a.reshape(bn, t_len, a.shape[-1])

        lv = diff_lam[l]
        lam = (jnp.exp(jnp.sum(lv[0, 0] * lv[0, 1])) - jnp.exp(jnp.sum(lv[1, 0] * lv[1, 1])) + lam_init)
        oa = _diff_attn(b3(qa), b3(ka), b3(va), lam.reshape(1), diff_subln_g[l].reshape(1, DA_VDIM),
                        1.0 - lam_init, l_len)
        ob = _win_attn(b3(qb), b3(kb), b3(vb), wg_sink[l], l_len)
        og = _gdn(b3(gq), b3(gk), b3(gv), b3(gb), l_len)

        wb = w_branch[l]
        wb1 = wb[1].reshape(WG_KV_HEADS, WG_GROUP, HEAD_DIM, d).transpose(1, 0, 2, 3).reshape(BRANCH_W, d)
        wb = jnp.stack([wb[0], wb1, wb[2]]).astype(BF16)
        xs = _merge(xs, oa.reshape(n_tok, BRANCH_W), ob.reshape(n_tok, BRANCH_W), og, proj,
                    b_gate[l].reshape(1, -1), gd_norm_g[l].reshape(1, GD_DV), mod, wb,
                    w_out[l].astype(BF16), bn, t_len, l_len)

        hmod, cnt1, e1, rk2, e2 = _peer_topk(xs, norm2_g[l].reshape(1, d), mod, peer_wq[l].astype(BF16),
                                             peer_keys[l], bn, t_len, l_len)
        xs = _peer_dense(xs, hmod, peer_u[l].astype(BF16), peer_v[l].astype(BF16).T, cnt1, e1, rk2, e2,
                         mod, bn, t_len, l_len)

    return xs.reshape(bn, t_len, d)[:, l_len:]
```

```python
import functools
import math

import jax
import jax.numpy as jnp
from jax import lax
from jax.experimental import pallas as pl
from jax.experimental.pallas import tpu as pltpu

F32 = jnp.float32
BF16 = jnp.bfloat16

EPS = 1e-6
GRID_W = 64
HEAD_DIM = 64
ROPE_THETA = 10000.0
ROPE_PAIRS = HEAD_DIM // 4
DA_HEADS = 4
DA_VDIM = 2 * HEAD_DIM
WG_HEADS = 8
WG_KV_HEADS = 2
WG_GROUP = WG_HEADS // WG_KV_HEADS
WINDOW = 128
WG_BLOCK = 128
GD_HEADS = 4
GD_DK = 128
GD_DV = 128
GD_CONV = 5
GD_CHUNK = 64
GD_QKV_W = GD_HEADS * (2 * GD_DK + GD_DV)
N_BRANCH = 3
BRANCH_W = 512
PEER_HEADS = 8
PEER_HALF = 128
N_KEYS = 128
PEER_TOPK = 16

LANES = 128
SUBLANES = 8
VMEM_LIMIT = 56 * 1024 * 1024
NEG = -1e30
LOG2E = math.log2(math.e)

C_QA, C_KA, C_VA, C_QKVC, C_ZC, C_QB, C_KB, C_VB, C_AB, C_GL = (
    0, 512, 1024, 1536, 3072, 3584, 4096, 4224, 4352, 4608)
PROJ_W = C_GL + N_BRANCH * 1024
PREP_W = C_GL
CONV_BLK = 1536


def _pick(n, cands):
    for c in cands:
        if n % c == 0:
            return c
    raise ValueError(f"no tile in {cands} divides {n}")


def _params(sem):
    return pltpu.CompilerParams(dimension_semantics=sem, vmem_limit_bytes=VMEM_LIMIT)


def _split2(a):
    hi = a.astype(BF16)
    return hi, (a - hi.astype(F32)).astype(BF16)


def _dot1(a, b, dn=None):
    a = a.astype(BF16)
    b = b.astype(BF16)
    if dn is None:
        return jnp.dot(a, b, preferred_element_type=F32)
    return lax.dot_general(a, b, dn, preferred_element_type=F32)


def _dot3(a, b, dn=None):
    ah, al = _split2(a)
    bh, bl = _split2(b)
    return _dot1(ah, bh, dn) + (_dot1(ah, bl, dn) + _dot1(al, bh, dn))


_NT = (((1,), (1,)), ((), ()))


def _row_sel(i, tpb, lt, bn):
    return jnp.where(i % tpb < lt, bn, i // tpb)


def _mm_body(x_ref, w_ref, o_ref):
    o_ref[...] = jnp.dot(x_ref[...], w_ref[...], preferred_element_type=F32)


def _mm(x, w):
    m, k = x.shape
    n = w.shape[1]
    tn = _pick(n, (1536, 1024, 512, 256, 128))
    return pl.pallas_call(
        _mm_body,
        grid=(n // tn,),
        in_specs=[pl.BlockSpec((m, k), lambda j: (0, 0)),
                  pl.BlockSpec((k, tn), lambda j: (0, j))],
        out_specs=pl.BlockSpec((m, tn), lambda j: (0, j)),
        out_shape=jax.ShapeDtypeStruct((m, n), F32),
        compiler_params=_params(("parallel",)),
        name="ada_mm",
    )(x, w)


def _modulate(x, g, sh, sc):
    h = x * lax.rsqrt(jnp.mean(x * x, axis=-1, keepdims=True) + EPS) * g
    return h * (1.0 + sc) + sh


def _inproj_body(x_ref, g_ref, sh_ref, sc_ref, w_ref, o_ref):
    h = _modulate(x_ref[...], g_ref[...], sh_ref[0], sc_ref[0])
    o_ref[...] = jnp.dot(h.astype(BF16), w_ref[...], preferred_element_type=F32)


def _inproj(xs, g, mod, w, bn, t_len, l_len):
    n, d = xs.shape
    tm = _pick(math.gcd(l_len, t_len), (512, 256, 128))
    tn = w.shape[1] // 2
    tpb, lt = t_len // tm, l_len // tm
    sel = lambda j, i: _row_sel(i, tpb, lt, bn)
    return pl.pallas_call(
        _inproj_body,
        grid=(w.shape[1] // tn, n // tm),
        in_specs=[pl.BlockSpec((tm, d), lambda j, i: (i, 0)),
                  pl.BlockSpec((1, d), lambda j, i: (0, 0)),
                  pl.BlockSpec((1, 1, d), lambda j, i: (sel(j, i) * 6, 0, 0)),
                  pl.BlockSpec((1, 1, d), lambda j, i: (sel(j, i) * 6 + 1, 0, 0)),
                  pl.BlockSpec((d, tn), lambda j, i: (0, j))],
        out_specs=pl.BlockSpec((tm, tn), lambda j, i: (i, j)),
        out_shape=jax.ShapeDtypeStruct((n, w.shape[1]), F32),
        compiler_params=_params(("parallel", "parallel")),
        name="inproj",
    )(xs, g, mod, mod, w)


def _prep_body(p_ref, prev_ref, next_ref, cos_ref, sa_ref, sb_ref, qkg_ref, cw_ref, alog_ref, dtb_ref,
               qa_ref, ka_ref, va_ref, qb_ref, kb_ref, vb_ref, gq_ref, gk_ref, gv_ref, gb_ref,
               xe_ref, *, tpb, lt):
    tm = p_ref.shape[0]
    li = lax.broadcasted_iota(jnp.int32, (LANES, LANES), 0)
    lj = lax.broadcasted_iota(jnp.int32, (LANES, LANES), 1)
    seg = jnp.where(li // HEAD_DIM == lj // HEAD_DIM, 1.0 / HEAD_DIM, 0.0).astype(BF16)
    cos = cos_ref[...]
    sa = sa_ref[...]
    sb = sb_ref[...]

    def normrope(x, gain):
        yh, yl = _split2(x * x)
        ms = jnp.dot(yh, seg, preferred_element_type=F32) + jnp.dot(yl, seg, preferred_element_type=F32)
        xn = x * lax.rsqrt(ms + EPS) * gain
        return xn * cos + pltpu.roll(xn, LANES - ROPE_PAIRS, 1) * sa + pltpu.roll(xn, ROPE_PAIRS, 1) * sb

    scale = HEAD_DIM ** -0.5 * LOG2E
    for c in range(4):
        cs = slice(c * LANES, (c + 1) * LANES)
        qa_ref[:, cs] = (normrope(p_ref[:, C_QA + c * LANES:C_QA + (c + 1) * LANES], qkg_ref[0:1]) * scale).astype(BF16)
        ka_ref[:, cs] = normrope(p_ref[:, C_KA + c * LANES:C_KA + (c + 1) * LANES], qkg_ref[1:2]).astype(BF16)
        qb_ref[:, cs] = (normrope(p_ref[:, C_QB + c * LANES:C_QB + (c + 1) * LANES], qkg_ref[2:3]) * scale).astype(BF16)
    kb_ref[...] = normrope(p_ref[:, C_KB:C_KB + LANES], qkg_ref[3:4]).astype(BF16)
    va_ref[...] = p_ref[:, C_VA:C_VA + 512].astype(BF16)
    vb_ref[...] = p_ref[:, C_VB:C_VB + LANES].astype(BF16)

    ti = pl.program_id(0) % tpb
    at_start = (ti == 0) | (ti == lt)
    at_end = (ti == lt - 1) | (ti == tpb - 1)
    xe_ref[0:SUBLANES] = jnp.where(at_start, 0.0, prev_ref[...])
    xe_ref[SUBLANES:SUBLANES + tm] = p_ref[:, C_QKVC:C_QKVC + GD_QKV_W]
    xe_ref[SUBLANES + tm:2 * SUBLANES + tm] = jnp.where(at_end, 0.0, next_ref[...])
    half = GD_CONV // 2
    y = None
    for i in range(GD_CONV):
        t = xe_ref[pl.ds(SUBLANES - half + i, tm), :] * cw_ref[i:i + 1]
        y = t if y is None else y + t
    y = y * jax.nn.sigmoid(y)
    for h in range(GD_HEADS):
        hs = slice(h * LANES, (h + 1) * LANES)
        q = y[:, h * LANES:(h + 1) * LANES]
        k = y[:, 512 + h * LANES:512 + (h + 1) * LANES]
        gq_ref[:, hs] = q * lax.rsqrt(jnp.sum(q * q, axis=-1, keepdims=True) + EPS) * (GD_DK ** -0.5)
        gk_ref[:, hs] = k * lax.rsqrt(jnp.sum(k * k, axis=-1, keepdims=True) + EPS)
    gv_ref[...] = y[:, 1024:]

    ab = p_ref[:, C_AB:C_AB + LANES]
    lane = lax.broadcasted_iota(jnp.int32, ab.shape, 1)
    gdec = -jnp.exp(alog_ref[...]) * jnp.logaddexp(ab + dtb_ref[...], 0.0)
    gb_ref[...] = jnp.where(lane < 2 * GD_HEADS, gdec, jax.nn.sigmoid(ab))


def _prep(proj, cos, sa, sb, qkg, cw, alog, dtb, t_len, l_len):
    n = proj.shape[0]
    tm = _pick(math.gcd(l_len, t_len), (256, 128))
    tpb, lt = t_len // tm, l_len // tm
    r8 = tm // SUBLANES
    nb8 = n // SUBLANES
    tok = lambda w: pl.BlockSpec((tm, w), lambda i: (i, 0))
    rope = pl.BlockSpec((tm, LANES), lambda i: (i % tpb, 0))
    full = lambda a: pl.BlockSpec(a.shape, lambda i: (0,) * a.ndim)
    shp = lambda w, dt: jax.ShapeDtypeStruct((n, w), dt)
    return pl.pallas_call(
        functools.partial(_prep_body, tpb=tpb, lt=lt),
        grid=(n // tm,),
        in_specs=[pl.BlockSpec((tm, PREP_W), lambda i: (i, 0)),
                  pl.BlockSpec((SUBLANES, CONV_BLK), lambda i: (jnp.maximum(i * r8 - 1, 0), 1)),
                  pl.BlockSpec((SUBLANES, CONV_BLK), lambda i: (jnp.minimum((i + 1) * r8, nb8 - 1), 1)),
                  rope, rope, rope, full(qkg), full(cw), full(alog), full(dtb)],
        out_specs=[tok(512), tok(512), tok(512), tok(512), tok(LANES), tok(LANES),
                   tok(512), tok(512), tok(512), tok(LANES)],
        out_shape=[shp(512, BF16), shp(512, BF16), shp(512, BF16), shp(512, BF16), shp(LANES, BF16),
                   shp(LANES, BF16), shp(512, F32), shp(512, F32), shp(512, F32), shp(LANES, F32)],
        scratch_shapes=[pltpu.VMEM((tm + 2 * SUBLANES, GD_QKV_W), F32)],
        compiler_params=_params(("parallel",)),
        name="prep",
    )(proj, proj, proj, cos, sa, sb, qkg, cw, alog, dtb)


def _diff_body(lam_ref, q_ref, k_ref, v_ref, g_ref, o_ref, *, post, l_len, n_ctx):
    q = q_ref[0]
    lane = lax.broadcasted_iota(jnp.int32, q.shape, 1)
    zero = jnp.zeros_like(q)
    q1 = jnp.where(lane < HEAD_DIM, q, zero)
    q2 = jnp.where(lane >= HEAD_DIM, q, zero)

    def run(k, v):
        def attend(s):
            p = jnp.exp2(s - jnp.max(s, axis=-1, keepdims=True))
            l = jnp.sum(p, axis=-1, keepdims=True)
            return jnp.dot(p.astype(BF16), v, preferred_element_type=F32) * (1.0 / l)

        s1 = lax.dot_general(q1, k, _NT, preferred_element_type=F32)
        s2 = lax.dot_general(q2, k, _NT, preferred_element_type=F32)
        o = attend(s1) - lam_ref[0] * attend(s2)
        ms = jnp.mean(o * o, axis=-1, keepdims=True)
        o_ref[0] = o * lax.rsqrt(ms + EPS) * (g_ref[...] * post)

    i = pl.program_id(2)

    @pl.when(i < n_ctx)
    def _():
        run(k_ref[0, :l_len], v_ref[0, :l_len])

    @pl.when(i >= n_ctx)
    def _():
        run(k_ref[0], v_ref[0])


def _diff_attn(q, k, v, lam, subln_g, post, l_len):
    b, t, _ = q.shape
    tq = _pick(math.gcd(l_len, t), (256, 128))
    return pl.pallas_call(
        functools.partial(_diff_body, post=post, l_len=l_len, n_ctx=l_len // tq),
        grid=(b, DA_HEADS, t // tq),
        in_specs=[pl.BlockSpec(memory_space=pltpu.SMEM),
                  pl.BlockSpec((1, tq, LANES), lambda bi, h, i: (bi, i, h)),
                  pl.BlockSpec((1, t, LANES), lambda bi, h, i: (bi, 0, h)),
                  pl.BlockSpec((1, t, LANES), lambda bi, h, i: (bi, 0, h)),
                  pl.BlockSpec((1, LANES), lambda bi, h, i: (0, 0))],
        out_specs=pl.BlockSpec((1, tq, LANES), lambda bi, h, i: (bi, i, h)),
        out_shape=jax.ShapeDtypeStruct((b, t, DA_HEADS * DA_VDIM), F32),
        compiler_params=_params(("parallel", "parallel", "parallel")),
        name="diff_attn",
    )(lam, q, k, v, subln_g)


def _win_body(sink_ref, q_ref, kc_ref, vc_ref, kp_ref, kn_ref, kx_ref, vp_ref, vn_ref, vx_ref, o_ref,
              *, lo, s_len):
    l_len = kc_ref.shape[1]
    q = q_ref[0]
    lane = lax.broadcasted_iota(jnp.int32, (WG_BLOCK, LANES), 1)

    def run(kcat, vcat, valid):
        vlane = lax.broadcasted_iota(jnp.int32, vcat.shape, 1)
        vz = jnp.zeros_like(vcat)
        vhalf = [jnp.where(vlane < HEAD_DIM, vcat, vz), jnp.where(vlane >= HEAD_DIM, vcat, vz)]
        heads = [(g, kv) for g in range(WG_GROUP) for kv in range(WG_KV_HEADS)]
        zq = jnp.zeros((WG_BLOCK, LANES), q.dtype)
        half = [lane < HEAD_DIM, lane >= HEAD_DIM]
        qms = [jnp.where(half[kv], q[:, g * LANES:(g + 1) * LANES], zq) for g, kv in heads]
        ss = [lax.dot_general(qm, kcat, _NT, preferred_element_type=F32) for qm in qms]
        if valid is not None:
            ss = [jnp.where(valid, s, NEG) for s in ss]
        sks = [sink_ref[kv * WG_GROUP + g] * LOG2E for g, kv in heads]
        ms = [jnp.maximum(jnp.max(s, axis=-1, keepdims=True), sk) for s, sk in zip(ss, sks)]
        ps = [jnp.exp2(s - m) for s, m in zip(ss, ms)]
        dens = [jnp.sum(p, axis=-1, keepdims=True) + jnp.exp2(sk - m) for p, sk, m in zip(ps, sks, ms)]
        os = [jnp.dot(p.astype(BF16), vhalf[kv], preferred_element_type=F32) * (1.0 / den)
              for p, den, (g, kv) in zip(ps, dens, heads)]
        for g in range(WG_GROUP):
            o_ref[0, :, g * LANES:(g + 1) * LANES] = os[2 * g] + os[2 * g + 1]

    n = pl.program_id(1)

    @pl.when(n < lo)
    def _():
        run(kc_ref[0], vc_ref[0], None)

    @pl.when(n >= lo)
    def _():
        kcat = jnp.concatenate([kc_ref[0], kp_ref[0], kn_ref[0], kx_ref[0]], axis=0)
        vcat = jnp.concatenate([vc_ref[0], vp_ref[0], vn_ref[0], vx_ref[0]], axis=0)
        nk = kcat.shape[0]
        col = lax.broadcasted_iota(jnp.int32, (WG_BLOCK, nk), 1)
        row = lax.broadcasted_iota(jnp.int32, (WG_BLOCK, nk), 0)
        j = col - l_len
        rel = j - WG_BLOCK - row
        kpos = (n - lo) * WG_BLOCK + j - WG_BLOCK
        valid = (col < l_len) | ((jnp.abs(rel) <= WINDOW) & (kpos >= 0) & (kpos < s_len))
        run(kcat, vcat, valid)


def _win_attn(q, k, v, sink, l_len):
    b, t, _ = q.shape
    blk = WG_BLOCK
    lo, nb = l_len // blk, t // blk
    ctx_spec = pl.BlockSpec((1, l_len, LANES), lambda bi, n: (bi, 0, 0))
    win = [pl.BlockSpec((1, blk, LANES), lambda bi, n, d=d: (bi, jnp.clip(n + d, lo, nb - 1), 0))
           for d in (-1, 0, 1)]
    return pl.pallas_call(
        functools.partial(_win_body, lo=lo, s_len=t - l_len),
        grid=(b, nb),
        in_specs=[pl.BlockSpec(memory_space=pltpu.SMEM),
                  pl.BlockSpec((1, blk, 4 * LANES), lambda bi, n: (bi, n, 0)),
                  ctx_spec, ctx_spec] + win + win,
        out_specs=pl.BlockSpec((1, blk, 4 * LANES), lambda bi, n: (bi, n, 0)),
        out_shape=jax.ShapeDtypeStruct((b, t, 4 * LANES), F32),
        compiler_params=_params(("parallel", "parallel")),
        name="win_attn",
    )(sink, q, k, v, k, k, k, v, v, v)


GD_LOCAL_CHUNKS = 4


def _dot3s(x, r):
    m = x.shape[0]
    xh, xl = _split2(x)
    rh, rl = _split2(r)
    t = jnp.dot(jnp.concatenate([xh, xl], axis=0), rh, preferred_element_type=F32)
    return t[:m] + t[m:] + jnp.dot(xh, rl, preferred_element_type=F32)


def _gdn_local_body(q_ref, k_ref, v_ref, gb_ref, u_ref, w_ref, qg_ref, qk_ref, kdt_ref, egl_ref, *, nch):
    c = GD_CHUNK
    nh = GD_HEADS
    ii = lax.broadcasted_iota(jnp.int32, (c, nh * c), 0)
    jl = lax.broadcasted_iota(jnp.int32, (c, nh * c), 1)
    jj = jl % c
    blk = jl // c
    eye = ii == jj
    eyef = eye.astype(F32)
    tri = [(jj <= ii, jj < ii, jj >= ii), (jj >= ii, jj > ii, jj <= ii)]
    hl = lax.broadcasted_iota(jnp.int32, (c, nh * LANES), 1) // LANES
    e_i = lax.broadcasted_iota(jnp.int32, (LANES, LANES), 0)
    e_j = lax.broadcasted_iota(jnp.int32, (LANES, LANES), 1)
    eye_b = (e_i == e_j).astype(BF16)

    def cat(cols):
        out = jnp.zeros((c, nh * c), F32)
        for h in range(nh):
            out = jnp.where(blk == h, cols[h], out)
        return out

    def nat(cols):
        out = jnp.zeros((c, nh * LANES), F32)
        for h in range(nh):
            out = jnp.where(hl == h, cols[h], out)
        return out

    def bd_cat(p):
        return jnp.concatenate([jnp.where(blk == h, p, 0.0) for h in range(nh)], axis=0)

    def bd_nat(x):
        return jnp.concatenate([jnp.where(hl == h, x, 0.0) for h in range(nh)], axis=0)

    def setup(ch):
        rows = slice(ch * c, (ch + 1) * c)
        gb = gb_ref[0, rows, :]
        q = q_ref[0, rows, :]
        k = k_ref[0, rows, :]
        v = v_ref[0, rows, :]
        kdh, kdl = _split2(bd_nat(k))
        out = []
        for d in range(2):
            incl, strict, incl_t = tri[d]
            g_cols = [gb[:, d * nh + h:d * nh + h + 1] for h in range(nh)]
            b_cols = [gb[:, (2 + d) * nh + h:(2 + d) * nh + h + 1] for h in range(nh)]
            g_cat = cat(g_cols)
            g_row = jnp.sum(jnp.where(eye, g_cat, 0.0), axis=0, keepdims=True)
            gc_cols = [jnp.sum(jnp.where(incl & (blk == h), g_row, 0.0), axis=1, keepdims=True)
                       for h in range(nh)]
            grow = jnp.sum(jnp.where(incl_t, g_cat, 0.0), axis=0, keepdims=True)
            glast = [jnp.sum(g_cols[h], axis=0, keepdims=True) for h in range(nh)]
            decay = jnp.exp(jnp.where(incl, cat(gc_cols) - grow, NEG))
            b_nat = nat(b_cols)
            eg_nat = nat([jnp.exp(gc_cols[h]) for h in range(nh)])
            kb = k * b_nat
            kbh, kbl = _split2(kb)
            qg_ref[0, d, rows, :] = q * eg_nat
            kd = (k * nat([jnp.exp(glast[h] - gc_cols[h]) for h in range(nh)])).astype(BF16)
            egl_ref[0, d, ch] = nat([jnp.exp(glast[h]) for h in range(nh)])[0:1]
            out.append(dict(rows=rows, ch=ch, d=d, q=q, decay=decay, strict=strict, kbh=kbh, kbl=kbl,
                            kdh=kdh, kdl=kdl, kd=kd, vb=v * b_nat, kbeg=kb * eg_nat))
        return out

    st = [s for ch in range(nch) for s in setup(ch)]
    kk1 = [lax.dot_general(jnp.concatenate([s["kbh"], s["kbl"]], axis=0), s["kdh"], _NT,
                           preferred_element_type=F32) for s in st]
    kk2 = [lax.dot_general(s["kbh"], s["kdl"], _NT, preferred_element_type=F32) for s in st]
    a = [jnp.where(s["strict"], (x[:c] + x[c:] + y) * s["decay"], 0.0) for x, y, s in zip(kk1, kk2, st)]
    tinv = [eyef - x for x in a]
    p = [_dot3s(x, bd_cat(x)) for x in a]
    for _ in range(4):
        t = [_dot3s(jnp.concatenate([ti, pi], axis=0), bd_cat(pi)) for ti, pi in zip(tinv, p)]
        tinv = [ti + x[:c] for ti, x in zip(tinv, t)]
        p = [x[c:] for x in t]
    tinv = [ti + _dot3s(ti, bd_cat(pi)) for ti, pi in zip(tinv, p)]
    us = [_dot3s(ti, bd_nat(s["vb"])) for ti, s in zip(tinv, st)]
    ws = [_dot3s(ti, bd_nat(s["kbeg"])) for ti, s in zip(tinv, st)]
    qks = [lax.dot_general(s["q"].astype(BF16), s["kdh"], _NT, preferred_element_type=F32) for s in st]
    kdts = [jnp.concatenate([lax.dot_general(eye_b, s["kd"][:, h * LANES:(h + 1) * LANES], _NT,
                                             preferred_element_type=F32) for h in range(nh)], axis=0)
            for s in st]
    for s, u, w, qk, kdt in zip(st, us, ws, qks, kdts):
        u_ref[0, s["d"], s["rows"], :] = u
        w_ref[0, s["d"], s["rows"], :] = w
        qk_ref[0, s["d"], s["ch"]] = qk * s["decay"]
        kdt_ref[0, s["d"], s["ch"]] = kdt.astype(BF16)


def _gdn_scan_body(*refs):
    ins, o_refs, s_ref = (refs[:6], refs[6:12]), refs[12:14], refs[14]

    @pl.when(pl.program_id(1) == 0)
    def _():
        s_ref[...] = jnp.zeros_like(s_ref)

    c = GD_CHUNK
    heads = range(GD_HEADS)
    sls = [slice(h * LANES, (h + 1) * LANES) for h in heads]
    hl = lax.broadcasted_iota(jnp.int32, (c, GD_HEADS * LANES), 1) // LANES
    chains = [(d, h) for d in range(2) for h in heads]
    ns = ins[0][3].shape[2]
    ss = [s_ref[d, h] for d, h in chains]
    for step in range(ns):
        pos = [step, ns - 1 - step]
        rows = [slice(p * c, (p + 1) * c) for p in pos]
        rs = [_dot1(jnp.concatenate([ins[d][1][0, 0, rows[d], sls[h]], ins[d][2][0, 0, rows[d], sls[h]]],
                                    axis=0), s) for (d, h), s in zip(chains, ss)]
        v_news = [ins[d][0][0, 0, rows[d], sls[h]] - r[:c] for (d, h), r in zip(chains, rs)]
        upd = [_dot1(ins[d][4][0, 0, pos[d], sls[h], :], vn) for (d, h), vn in zip(chains, v_news)]
        for d in range(2):
            lo = d * GD_HEADS
            v_all = jnp.concatenate(v_news[lo:lo + GD_HEADS], axis=1)
            vbd = jnp.concatenate([jnp.where(hl == h, v_all, 0.0) for h in heads], axis=0)
            o_refs[d][0, rows[d], :] = (jnp.concatenate([r[c:] for r in rs[lo:lo + GD_HEADS]], axis=1)
                                        + _dot1(ins[d][3][0, 0, pos[d]], vbd))
        ss = [s * ins[d][5][0, 0, pos[d], :, sls[h]] + up for (d, h), s, up in zip(chains, ss, upd)]
    for (d, h), s in zip(chains, ss):
        s_ref[d, h] = s


def _gdn(q, k, v, gb, l_len):
    b, t, _ = q.shape
    c = GD_CHUNK
    nc, ncc = t // c, l_len // c
    nch = _pick(nc, (GD_LOCAL_CHUNKS, 1))
    h = GD_HEADS
    hw = h * LANES
    tokl = pl.BlockSpec((1, nch * c, hw), lambda bi, j: (bi, j, 0))
    tok4 = pl.BlockSpec((1, 2, nch * c, hw), lambda bi, j: (bi, 0, j, 0))
    ch5 = lambda r, w2: pl.BlockSpec((1, 2, nch, r, w2), lambda bi, j: (bi, 0, j, 0, 0))
    tok_shape = jax.ShapeDtypeStruct((b, 2, t, hw), F32)
    u, w, qg, qk, kdt, egl = pl.pallas_call(
        functools.partial(_gdn_local_body, nch=nch),
        grid=(b, nc // nch),
        in_specs=[tokl, tokl, tokl, pl.BlockSpec((1, nch * c, LANES), lambda bi, j: (bi, j, 0))],
        out_specs=[tok4, tok4, tok4, ch5(c, h * c), ch5(hw, c), ch5(1, hw)],
        out_shape=[tok_shape, tok_shape, tok_shape,
                   jax.ShapeDtypeStruct((b, 2, nc, c, h * c), F32),
                   jax.ShapeDtypeStruct((b, 2, nc, hw, c), BF16),
                   jax.ShapeDtypeStruct((b, 2, nc, 1, hw), F32)],
        compiler_params=_params(("parallel", "parallel")),
        name="gdn_local",
    )(q, k, v, gb)

    ns = _pick(math.gcd(ncc, nc), (4, 2, 1))

    def blk(d, j):
        first = j * ns
        last = jnp.where(first < ncc, ncc - 1 - first, nc - 1 - first + ncc) - (ns - 1)
        return j if d == 0 else last // ns

    def specs(d):
        stok = pl.BlockSpec((1, 1, ns * c, hw), lambda bi, j: (bi, d, blk(d, j), 0))
        sch = lambda r, w2: pl.BlockSpec((1, 1, ns, r, w2), lambda bi, j: (bi, d, blk(d, j), 0, 0))
        return [stok, stok, stok, sch(c, h * c), sch(hw, c), sch(1, hw)]

    out_spec = lambda d: pl.BlockSpec((1, ns * c, hw), lambda bi, j: (bi, blk(d, j), 0))
    out_shape = jax.ShapeDtypeStruct((b, t, hw), F32)
    return pl.pallas_call(
        _gdn_scan_body,
        grid=(b, nc // ns),
        in_specs=specs(0) + specs(1),
        out_specs=[out_spec(0), out_spec(1)],
        out_shape=[out_shape, out_shape],
        scratch_shapes=[pltpu.VMEM((2, h, GD_DK, GD_DV), F32)],
        compiler_params=_params(("parallel", "arbitrary")),
        name="gdn_scan",
    )(u, w, qg, qk, kdt, egl, u, w, qg, qk, kdt, egl)


def _merge_body(x_ref, oa_ref, ob_ref, og0_ref, og1_ref, z_ref, gl0_ref, gl1_ref, bg_ref, ng_ref,
                gt_ref, wb_ref, wo_ref, o_ref):
    d = x_ref.shape[1]
    parts = []
    for h in range(GD_HEADS):
        sl = slice(h * LANES, (h + 1) * LANES)
        o = og0_ref[:, sl] + og1_ref[:, sl]
        z = z_ref[:, sl]
        o = o * lax.rsqrt(jnp.mean(o * o, axis=-1, keepdims=True) + EPS) * ng_ref[...]
        parts.append(o * (z * jax.nn.sigmoid(z)))
    oc = jnp.concatenate(parts, axis=1)
    gl = jnp.concatenate([gl0_ref[...], gl1_ref[...]], axis=1)
    merged = None
    for r, o_r in enumerate((oa_ref[...], ob_ref[...], oc)):
        y = jnp.dot(o_r.astype(BF16), wb_ref[r], preferred_element_type=F32)
        t = jax.nn.sigmoid(gl[:, r * d:(r + 1) * d] + bg_ref[:, r * d:(r + 1) * d]) * y
        merged = t if merged is None else merged + t
    m = jnp.dot(merged.astype(BF16), wo_ref[...], preferred_element_type=F32)
    o_ref[...] = x_ref[...] + gt_ref[0] * m


def _merge(xs, oa, ob, og, proj, b_gate, norm_g, mod, wb, wo, bn, t_len, l_len):
    n, d = xs.shape
    tm = _pick(math.gcd(l_len, t_len), (256, 128))
    tpb, lt = t_len // tm, l_len // tm
    tok = lambda w: pl.BlockSpec((tm, w), lambda i: (i, 0))
    full = lambda a: pl.BlockSpec(a.shape, lambda i: (0,) * a.ndim)
    return pl.pallas_call(
        _merge_body,
        grid=(n // tm,),
        in_specs=[tok(d), tok(BRANCH_W), tok(BRANCH_W), tok(BRANCH_W), tok(BRANCH_W),
                  pl.BlockSpec((tm, BRANCH_W), lambda i: (i, C_ZC // BRANCH_W)),
                  pl.BlockSpec((tm, CONV_BLK), lambda i: (i, C_GL // CONV_BLK)),
                  pl.BlockSpec((tm, CONV_BLK), lambda i: (i, C_GL // CONV_BLK + 1)),
                  full(b_gate), full(norm_g),
                  pl.BlockSpec((1, 1, d), lambda i: (_row_sel(i, tpb, lt, bn) * 6 + 2, 0, 0)),
                  full(wb), full(wo)],
        out_specs=tok(d),
        out_shape=jax.ShapeDtypeStruct((n, d), F32),
        compiler_params=_params(("parallel",)),
        name="merge",
    )(xs, oa, ob, og[0].reshape(n, BRANCH_W), og[1].reshape(n, BRANCH_W), proj, proj, proj,
      b_gate, norm_g, mod, wb, wo)


_CAND = [(r1, r2) for r1 in range(1, PEER_TOPK + 1) for r2 in range(1, PEER_TOPK + 1)
         if r1 * r2 <= PEER_TOPK]


def _extract_top(cur, n, want_rank=False):
    rank = jnp.full(cur.shape, 99.0, F32) if want_rank else None
    vals = []
    for r in range(n):
        m = jnp.max(cur, axis=0, keepdims=True)
        hit = cur == m
        if want_rank:
            rank = jnp.where(hit, float(r + 1), rank)
        cur = jnp.where(hit, -jnp.inf, cur)
        vals.append(m)
    return vals, rank


def _topk_body(x_ref, g_ref, sh_ref, sc_ref, wq_ref, keys_ref, h_ref, cnt1_ref, e1_ref, rk2_ref, e2_ref):
    hm = _modulate(x_ref[...], g_ref[...], sh_ref[0], sc_ref[0])
    h_ref[...] = (hm * math.sqrt(0.5)).astype(BF16)
    hmod = hm.astype(BF16)
    nr = PEER_TOPK
    for h in range(PEER_HEADS):
        st = []
        for c in range(2):
            col = (2 * h + c) * LANES
            qs = jnp.dot(hmod, wq_ref[:, col:col + LANES], preferred_element_type=F32)
            st.append(_dot3(keys_ref[c], qs, _NT))
        a, _ = _extract_top(st[0], nr)
        b, rank2 = _extract_top(st[1], nr, want_rank=True)
        rows = [a[r1 - 1] + b[r2 - 1] for r1, r2 in _CAND]
        pad = (-len(rows)) % SUBLANES
        cand = jnp.concatenate(rows + [jnp.full_like(rows[0], -jnp.inf)] * pad, axis=0)
        top, _ = _extract_top(cand, nr)
        tau = top[PEER_TOPK - 1]
        z = jnp.sum(jnp.where(cand >= tau, jnp.exp(cand - top[0]), 0.0), axis=0, keepdims=True)
        cnt1 = jnp.zeros_like(st[0])
        for r1 in range(1, PEER_TOPK + 1):
            cnt = None
            for r2 in range(1, PEER_TOPK + 1):
                if r1 * r2 <= PEER_TOPK:
                    t = (a[r1 - 1] + b[r2 - 1] >= tau).astype(F32)
                    cnt = t if cnt is None else cnt + t
            cnt1 = jnp.where(st[0] == a[r1 - 1], cnt, cnt1)
        cnt1_ref[h] = cnt1
        e1_ref[h] = jnp.exp(st[0] - a[0])
        rk2_ref[h] = rank2.astype(BF16)
        e2_ref[h] = (jnp.exp(st[1] - b[0]) * (1.0 / z)).astype(BF16)


def _peer_topk(xs, g, mod, wq, keys, bn, t_len, l_len):
    n, d = xs.shape
    tm = _pick(math.gcd(l_len, t_len), (256, 128))
    tpb, lt = t_len // tm, l_len // tm
    sel = lambda i: _row_sel(i, tpb, lt, bn)
    out = pl.BlockSpec((PEER_HEADS, N_KEYS, tm), lambda i: (0, 0, i))
    shp = lambda dt: jax.ShapeDtypeStruct((PEER_HEADS, N_KEYS, n), dt)
    return pl.pallas_call(
        _topk_body,
        grid=(n // tm,),
        in_specs=[pl.BlockSpec((tm, d), lambda i: (i, 0)),
                  pl.BlockSpec((1, d), lambda i: (0, 0)),
                  pl.BlockSpec((1, 1, d), lambda i: (sel(i) * 6 + 3, 0, 0)),
                  pl.BlockSpec((1, 1, d), lambda i: (sel(i) * 6 + 4, 0, 0)),
                  pl.BlockSpec(wq.shape, lambda i: (0, 0)),
                  pl.BlockSpec(keys.shape, lambda i: (0, 0, 0))],
        out_specs=[pl.BlockSpec((tm, d), lambda i: (i, 0)), out, out, out, out],
        out_shape=[jax.ShapeDtypeStruct((n, d), BF16), shp(F32), shp(F32), shp(BF16), shp(BF16)],
        compiler_params=_params(("parallel",)),
        name="peer_topk",
    )(xs, g, mod, mod, wq, keys)


DENSE_CHUNK = 256
DENSE_KDEPTH = 1024
DENSE_AHEAD = 3


def _dense_body(x_ref, h_ref, u_ref, vt_ref, cnt1_ref, e1_ref, rk2_ref, e2_ref, *rest, te):
    gt_refs, o_ref, acc_ref = rest[:-2], rest[-2], rest[-1]
    e = pl.program_id(1)

    @pl.when(e == 0)
    def _():
        acc_ref[...] = jnp.zeros_like(acc_ref)

    hs = h_ref[...]
    zero = jnp.zeros((N_KEYS, hs.shape[0]), BF16)
    ck = DENSE_CHUNK
    nck = te // ck

    def mm1(c):
        return lax.dot_general(u_ref[c * ck:(c + 1) * ck, :], hs, _NT, preferred_element_type=F32)

    def gate(c):
        parts = []
        for j in range(c * ck // N_KEYS, (c + 1) * ck // N_KEYS):
            i1 = e * (te // N_KEYS) + j
            g = zero
            for h in range(PEER_HEADS):
                cnt = cnt1_ref[h, pl.ds(i1, 1), :].astype(BF16)
                w1 = e1_ref[h, pl.ds(i1, 1), :].astype(BF16)
                g = g + jnp.where(rk2_ref[h] <= cnt, e2_ref[h], zero) * w1
            parts.append(g)
        return jnp.concatenate(parts, axis=0)

    def mm2(c0, c1):
        hid = jnp.concatenate(hids[c0:c1], axis=0)
        acc_ref[...] += jnp.dot(vt_ref[:, c0 * ck:c1 * ck], hid, preferred_element_type=F32)

    grp = max(1, DENSE_KDEPTH // ck)
    acts = [mm1(c) for c in range(min(DENSE_AHEAD, nck))]
    hids = []
    pending = None
    for c in range(nck):
        if c + DENSE_AHEAD < nck:
            acts.append(mm1(c + DENSE_AHEAD))
        y = acts[c]
        hids.append((y * (1.0 + lax.erf(y))).astype(BF16) * gate(c))
        if pending is not None:
            mm2(*pending)
            pending = None
        if c % grp == grp - 1:
            pending = (c + 1 - grp, c + 1)
    mm2(*pending)

    @pl.when(e == pl.num_programs(1) - 1)
    def _():
        base = acc_ref.shape[1] // len(gt_refs)
        for k, gt_ref in enumerate(gt_refs):
            rs = slice(k * base, (k + 1) * base)
            o_ref[rs] = x_ref[rs] + (gt_ref[0] * math.sqrt(0.5)) * acc_ref[:, rs].T


def _peer_dense(xs, hmod, u, vt, cnt1, e1, rk2, e2, mod, bn, t_len, l_len):
    n, d = xs.shape
    ne = u.shape[0]
    base = _pick(math.gcd(l_len, t_len), (256, 128))
    nsub = 2 if n % (2 * base) == 0 else 1
    tm = nsub * base
    te = _pick(ne, (2048, 1024, 512))
    tpb, lt = t_len // base, l_len // base
    gts = [pl.BlockSpec((1, 1, d), lambda i, e, k=k: (_row_sel(i * nsub + k, tpb, lt, bn) * 6 + 5, 0, 0))
           for k in range(nsub)]
    sel = pl.BlockSpec((PEER_HEADS, N_KEYS, tm), lambda i, e: (0, 0, i))
    tok = pl.BlockSpec((tm, d), lambda i, e: (i, 0))
    tab = pl.BlockSpec((te, d), lambda i, e: (e, 0))
    return pl.pallas_call(
        functools.partial(_dense_body, te=te),
        grid=(n // tm, ne // te),
        in_specs=[tok, tok, tab, pl.BlockSpec((d, te), lambda i, e: (0, e)), sel, sel, sel, sel] + gts,
        out_specs=tok,
        out_shape=jax.ShapeDtypeStruct((n, d), F32),
        scratch_shapes=[pltpu.VMEM((d, tm), F32)],
        compiler_params=_params(("parallel", "arbitrary")),
        name="peer_dense",
    )(xs, hmod, u, vt, cnt1, e1, rk2, e2, *([mod] * nsub))


def _rope_tables(rows, l_len):
    row = jnp.repeat(jnp.arange(rows), GRID_W).astype(F32)
    col = jnp.tile(jnp.arange(GRID_W), rows).astype(F32)
    inv = jnp.power(ROPE_THETA, -jnp.arange(ROPE_PAIRS, dtype=F32) / ROPE_PAIRS)
    ar = row[:, None] * inv
    ac = col[:, None] * inv
    ang = jnp.concatenate([ar, ar, ac, ac], axis=-1)
    cos = jnp.concatenate([jnp.ones((l_len, HEAD_DIM), F32), jnp.cos(ang)], axis=0)
    sin = jnp.concatenate([jnp.zeros((l_len, HEAD_DIM), F32), jnp.sin(ang)], axis=0)
    cos = jnp.tile(cos, (1, 2))
    sin = jnp.tile(sin, (1, 2))
    first = (jnp.arange(LANES) % (2 * ROPE_PAIRS)) < ROPE_PAIRS
    return cos, jnp.where(first, -sin, 0.0), jnp.where(first, 0.0, sin)


def _reorder_w_in(w):
    d = w.shape[0]
    qa_ka_va, qb, kb_vb = w[:, :1536], w[:, 1536:2048], w[:, 2048:2304]
    qkvc, zc, ab, gl = w[:, 2304:3840], w[:, 3840:4352], w[:, 4352:4368], w[:, 4368:]
    qb = qb.reshape(d, WG_KV_HEADS, WG_GROUP, HEAD_DIM).transpose(0, 2, 1, 3).reshape(d, 512)
    pad = jnp.zeros((d, C_GL - C_AB - ab.shape[1]), w.dtype)
    return jnp.concatenate([qa_ka_va, qkvc, zc, qb, kb_vb, ab, pad, gl], axis=1).astype(BF16)


def kernel(x, c, ctx, c_ctx, norm1_g, norm2_g, w_ada, b_ada, w_in, b_gate, qk_norm_g, diff_lam,
           diff_subln_g, wg_sink, gd_conv_w, gd_a_log, gd_dt_bias, gd_norm_g, w_branch, w_out,
           peer_wq, peer_keys, peer_u, peer_v):
    bn, s_len, d = x.shape
    l_len = ctx.shape[1]
    t_len = l_len + s_len
    n_tok = bn * t_len
    depth = w_in.shape[0]
    cos, sa, sb = _rope_tables(s_len // GRID_W, l_len)

    xs = jnp.concatenate([ctx, x], axis=1).reshape(n_tok, d)
    cvec = jnp.concatenate([c, c_ctx[None]], axis=0)
    cact = jnp.pad(jax.nn.silu(cvec), ((0, 16 - (bn + 1) % 16), (0, 0))).astype(BF16)
    lane_pad = lambda a: jnp.pad(a.reshape(1, -1), ((0, 0), (0, LANES - a.size)))

    for l in range(depth):
        lam_init = 0.8 - 0.6 * math.exp(-0.3 * l)
        mod = (_mm(cact, w_ada[l].astype(BF16))[:bn + 1] + b_ada[l]).reshape((bn + 1) * 6, 1, d)

        proj = _inproj(xs, norm1_g[l].reshape(1, d), mod, _reorder_w_in(w_in[l]), bn, t_len, l_len)
        qa, ka, va, qb, kb, vb, gq, gk, gv, gb = _prep(
            proj, cos, sa, sb, jnp.tile(qk_norm_g[l], (1, 2)), gd_conv_w[l],
            lane_pad(gd_a_log[l]), lane_pad(gd_dt_bias[l]), t_len, l_len)
        b3 = lambda a: a.reshape(bn, t_len, a.shape[-1])

        lv = diff_lam[l]
        lam = (jnp.exp(jnp.sum(lv[0, 0] * lv[0, 1])) - jnp.exp(jnp.sum(lv[1, 0] * lv[1, 1])) + lam_init)
        oa = _diff_attn(b3(qa), b3(ka), b3(va), lam.reshape(1), diff_subln_g[l].reshape(1, DA_VDIM),
                        1.0 - lam_init, l_len)
        ob = _win_attn(b3(qb), b3(kb), b3(vb), wg_sink[l], l_len)
        og = _gdn(b3(gq), b3(gk), b3(gv), b3(gb), l_len)

        wb = w_branch[l]
        wb1 = wb[1].reshape(WG_KV_HEADS, WG_GROUP, HEAD_DIM, d).transpose(1, 0, 2, 3).reshape(BRANCH_W, d)
        wb = jnp.stack([wb[0], wb1, wb[2]]).astype(BF16)
        xs = _merge(xs, oa.reshape(n_tok, BRANCH_W), ob.reshape(n_tok, BRANCH_W), og, proj,
                    b_gate[l].reshape(1, -1), gd_norm_g[l].reshape(1, GD_DV), mod, wb,
                    w_out[l].astype(BF16), bn, t_len, l_len)

        hmod, cnt1, e1, rk2, e2 = _peer_topk(xs, norm2_g[l].reshape(1, d), mod, peer_wq[l].astype(BF16),
                                             peer_keys[l], bn, t_len, l_len)
        xs = _peer_dense(xs, hmod, peer_u[l].astype(BF16), peer_v[l].astype(BF16).T, cnt1, e1, rk2, e2,
                         mod, bn, t_len, l_len)

    return xs.reshape(bn, t_len, d)[:, l_len:]
```

```python
import functools
import math

import jax
import jax.numpy as jnp
from jax import lax
from jax.experimental import pallas as pl
from jax.experimental.pallas import tpu as pltpu

F32 = jnp.float32
BF16 = jnp.bfloat16

EPS = 1e-6
GRID_W = 64
HEAD_DIM = 64
ROPE_THETA = 10000.0
ROPE_PAIRS = HEAD_DIM // 4
DA_HEADS = 4
DA_VDIM = 2 * HEAD_DIM
WG_HEADS = 8
WG_KV_HEADS = 2
WG_GROUP = WG_HEADS // WG_KV_HEADS
WINDOW = 128
WG_BLOCK = 128
GD_HEADS = 4
GD_DK = 128
GD_DV = 128
GD_CONV = 5
GD_CHUNK = 64
GD_QKV_W = GD_HEADS * (2 * GD_DK + GD_DV)
N_BRANCH = 3
BRANCH_W = 512
PEER_HEADS = 8
PEER_HALF = 128
N_KEYS = 128
PEER_TOPK = 16

LANES = 128
SUBLANES = 8
VMEM_LIMIT = 56 * 1024 * 1024
NEG = -1e30
LOG2E = math.log2(math.e)

C_QA, C_KA, C_VA, C_QKVC, C_ZC, C_QB, C_KB, C_VB, C_AB, C_GL = (
    0, 512, 1024, 1536, 3072, 3584, 4096, 4224, 4352, 4608)
PROJ_W = C_GL + N_BRANCH * 1024
PREP_W = C_GL
CONV_BLK = 1536


def _pick(n, cands):
    for c in cands:
        if n % c == 0:
            return c
    raise ValueError(f"no tile in {cands} divides {n}")


def _params(sem):
    return pltpu.CompilerParams(dimension_semantics=sem, vmem_limit_bytes=VMEM_LIMIT)


def _split2(a):
    hi = a.astype(BF16)
    return hi, (a - hi.astype(F32)).astype(BF16)


def _dot1(a, b, dn=None):
    a = a.astype(BF16)
    b = b.astype(BF16)
    if dn is None:
        return jnp.dot(a, b, preferred_element_type=F32)
    return lax.dot_general(a, b, dn, preferred_element_type=F32)


def _dot3(a, b, dn=None):
    ah, al = _split2(a)
    bh, bl = _split2(b)
    return _dot1(ah, bh, dn) + (_dot1(ah, bl, dn) + _dot1(al, bh, dn))


_NT = (((1,), (1,)), ((), ()))


def _row_sel(i, tpb, lt, bn):
    return jnp.where(i % tpb < lt, bn, i // tpb)


def _mm_body(x_ref, w_ref, o_ref):
    o_ref[...] = jnp.dot(x_ref[...], w_ref[...], preferred_element_type=F32)


def _mm(x, w, l):
    m, k = x.shape
    n = w.shape[2]
    tn = _pick(n, (1536, 1024, 512, 256, 128))
    return pl.pallas_call(
        _mm_body,
        grid=(n // tn,),
        in_specs=[pl.BlockSpec((m, k), lambda j: (0, 0)),
                  pl.BlockSpec((None, k, tn), lambda j: (l, 0, j))],
        out_specs=pl.BlockSpec((m, tn), lambda j: (0, j)),
        out_shape=jax.ShapeDtypeStruct((m, n), F32),
        compiler_params=_params(("parallel",)),
        name="ada_mm",
    )(x, w)


def _modulate(x, g, sh, sc):
    h = x * lax.rsqrt(jnp.mean(x * x, axis=-1, keepdims=True) + EPS) * g
    return h * (1.0 + sc) + sh


def _inproj_body(x_ref, g_ref, sh_ref, sc_ref, w_ref, o_ref):
    h = _modulate(x_ref[...], g_ref[...], sh_ref[0], sc_ref[0])
    o_ref[...] = jnp.dot(h.astype(BF16), w_ref[...], preferred_element_type=F32)


def _inproj(xs, g, mod, w, l, bn, t_len, l_len):
    n, d = xs.shape
    tm = _pick(math.gcd(l_len, t_len), (512, 256, 128))
    nw = w.shape[2]
    tn = nw // 2
    tpb, lt = t_len // tm, l_len // tm
    sel = lambda j, i: _row_sel(i, tpb, lt, bn)
    return pl.pallas_call(
        _inproj_body,
        grid=(nw // tn, n // tm),
        in_specs=[pl.BlockSpec((tm, d), lambda j, i: (i, 0)),
                  pl.BlockSpec((1, d), lambda j, i: (0, 0)),
                  pl.BlockSpec((1, 1, d), lambda j, i: (sel(j, i) * 6, 0, 0)),
                  pl.BlockSpec((1, 1, d), lambda j, i: (sel(j, i) * 6 + 1, 0, 0)),
                  pl.BlockSpec((None, d, tn), lambda j, i: (l, 0, j))],
        out_specs=pl.BlockSpec((tm, tn), lambda j, i: (i, j)),
        out_shape=jax.ShapeDtypeStruct((n, nw), F32),
        compiler_params=_params(("parallel", "parallel")),
        name="inproj",
    )(xs, g, mod, mod, w)


def _prep_body(p_ref, prev_ref, next_ref, cos_ref, sa_ref, sb_ref, qkg_ref, cw_ref, alog_ref, dtb_ref,
               qa_ref, ka_ref, va_ref, qb_ref, kb_ref, vb_ref, gq_ref, gk_ref, gv_ref, gb_ref,
               xe_ref, *, tpb, lt):
    tm = p_ref.shape[0]
    li = lax.broadcasted_iota(jnp.int32, (LANES, LANES), 0)
    lj = lax.broadcasted_iota(jnp.int32, (LANES, LANES), 1)
    seg = jnp.where(li // HEAD_DIM == lj // HEAD_DIM, 1.0 / HEAD_DIM, 0.0).astype(BF16)
    cos = cos_ref[...]
    sa = sa_ref[...]
    sb = sb_ref[...]

    def normrope(x, gain):
        yh, yl = _split2(x * x)
        ms = jnp.dot(yh, seg, preferred_element_type=F32) + jnp.dot(yl, seg, preferred_element_type=F32)
        xn = x * lax.rsqrt(ms + EPS) * gain
        return xn * cos + pltpu.roll(xn, LANES - ROPE_PAIRS, 1) * sa + pltpu.roll(xn, ROPE_PAIRS, 1) * sb

    scale = HEAD_DIM ** -0.5 * LOG2E
    for c in range(4):
        cs = slice(c * LANES, (c + 1) * LANES)
        qa_ref[:, cs] = (normrope(p_ref[:, C_QA + c * LANES:C_QA + (c + 1) * LANES], qkg_ref[0:1]) * scale).astype(BF16)
        ka_ref[:, cs] = normrope(p_ref[:, C_KA + c * LANES:C_KA + (c + 1) * LANES], qkg_ref[1:2]).astype(BF16)
        qb_ref[:, cs] = (normrope(p_ref[:, C_QB + c * LANES:C_QB + (c + 1) * LANES], qkg_ref[2:3]) * scale).astype(BF16)
    kb_ref[...] = normrope(p_ref[:, C_KB:C_KB + LANES], qkg_ref[3:4]).astype(BF16)
    va_ref[...] = p_ref[:, C_VA:C_VA + 512].astype(BF16)
    vb_ref[...] = p_ref[:, C_VB:C_VB + LANES].astype(BF16)

    ti = pl.program_id(0) % tpb
    at_start = (ti == 0) | (ti == lt)
    at_end = (ti == lt - 1) | (ti == tpb - 1)
    xe_ref[0:SUBLANES] = jnp.where(at_start, 0.0, prev_ref[...])
    xe_ref[SUBLANES:SUBLANES + tm] = p_ref[:, C_QKVC:C_QKVC + GD_QKV_W]
    xe_ref[SUBLANES + tm:2 * SUBLANES + tm] = jnp.where(at_end, 0.0, next_ref[...])
    half = GD_CONV // 2
    y = None
    for i in range(GD_CONV):
        t = xe_ref[pl.ds(SUBLANES - half + i, tm), :] * cw_ref[i:i + 1]
        y = t if y is None else y + t
    y = y * jax.nn.sigmoid(y)
    for h in range(GD_HEADS):
        hs = slice(h * LANES, (h + 1) * LANES)
        q = y[:, h * LANES:(h + 1) * LANES]
        k = y[:, 512 + h * LANES:512 + (h + 1) * LANES]
        gq_ref[:, hs] = q * lax.rsqrt(jnp.sum(q * q, axis=-1, keepdims=True) + EPS) * (GD_DK ** -0.5)
        gk_ref[:, hs] = k * lax.rsqrt(jnp.sum(k * k, axis=-1, keepdims=True) + EPS)
    gv_ref[...] = y[:, 1024:]

    ab = p_ref[:, C_AB:C_AB + LANES]
    lane = lax.broadcasted_iota(jnp.int32, ab.shape, 1)
    gdec = -jnp.exp(alog_ref[...]) * jnp.logaddexp(ab + dtb_ref[...], 0.0)
    gb_ref[...] = jnp.where(lane < 2 * GD_HEADS, gdec, jax.nn.sigmoid(ab))


def _prep(proj, cos, sa, sb, qkg, cw, alog, dtb, t_len, l_len):
    n = proj.shape[0]
    tm = _pick(math.gcd(l_len, t_len), (256, 128))
    tpb, lt = t_len // tm, l_len // tm
    r8 = tm // SUBLANES
    nb8 = n // SUBLANES
    tok = lambda w: pl.BlockSpec((tm, w), lambda i: (i, 0))
    rope = pl.BlockSpec((tm, LANES), lambda i: (i % tpb, 0))
    full = lambda a: pl.BlockSpec(a.shape, lambda i: (0,) * a.ndim)
    shp = lambda w, dt: jax.ShapeDtypeStruct((n, w), dt)
    return pl.pallas_call(
        functools.partial(_prep_body, tpb=tpb, lt=lt),
        grid=(n // tm,),
        in_specs=[pl.BlockSpec((tm, PREP_W), lambda i: (i, 0)),
                  pl.BlockSpec((SUBLANES, CONV_BLK), lambda i: (jnp.maximum(i * r8 - 1, 0), 1)),
                  pl.BlockSpec((SUBLANES, CONV_BLK), lambda i: (jnp.minimum((i + 1) * r8, nb8 - 1), 1)),
                  rope, rope, rope, full(qkg), full(cw), full(alog), full(dtb)],
        out_specs=[tok(512), tok(512), tok(512), tok(512), tok(LANES), tok(LANES),
                   tok(512), tok(512), tok(512), tok(LANES)],
        out_shape=[shp(512, BF16), shp(512, BF16), shp(512, BF16), shp(512, BF16), shp(LANES, BF16),
                   shp(LANES, BF16), shp(512, F32), shp(512, F32), shp(512, F32), shp(LANES, F32)],
        scratch_shapes=[pltpu.VMEM((tm + 2 * SUBLANES, GD_QKV_W), F32)],
        compiler_params=_params(("parallel",)),
        name="prep",
    )(proj, proj, proj, cos, sa, sb, qkg, cw, alog, dtb)


def _diff_body(lam_ref, q_ref, k_ref, v_ref, g_ref, o_ref, *, post, l_len, n_ctx):
    q = q_ref[0]
    lane = lax.broadcasted_iota(jnp.int32, q.shape, 1)
    zero = jnp.zeros_like(q)
    q1 = jnp.where(lane < HEAD_DIM, q, zero)
    q2 = jnp.where(lane >= HEAD_DIM, q, zero)

    def run(k, v):
        def attend(s):
            p = jnp.exp2(s - jnp.max(s, axis=-1, keepdims=True))
            l = jnp.sum(p, axis=-1, keepdims=True)
            return jnp.dot(p.astype(BF16), v, preferred_element_type=F32) * (1.0 / l)

        s1 = lax.dot_general(q1, k, _NT, preferred_element_type=F32)
        s2 = lax.dot_general(q2, k, _NT, preferred_element_type=F32)
        o = attend(s1) - lam_ref[0] * attend(s2)
        ms = jnp.mean(o * o, axis=-1, keepdims=True)
        o_ref[0] = o * lax.rsqrt(ms + EPS) * (g_ref[...] * post)

    i = pl.program_id(2)

    @pl.when(i < n_ctx)
    def _():
        run(k_ref[0, :l_len], v_ref[0, :l_len])

    @pl.when(i >= n_ctx)
    def _():
        run(k_ref[0], v_ref[0])


def _diff_attn(q, k, v, lam, subln_g, post, l_len):
    b, t, _ = q.shape
    tq = _pick(math.gcd(l_len, t), (256, 128))
    return pl.pallas_call(
        functools.partial(_diff_body, post=post, l_len=l_len, n_ctx=l_len // tq),
        grid=(b, DA_HEADS, t // tq),
        in_specs=[pl.BlockSpec(memory_space=pltpu.SMEM),
                  pl.BlockSpec((1, tq, LANES), lambda bi, h, i: (bi, i, h)),
                  pl.BlockSpec((1, t, LANES), lambda bi, h, i: (bi, 0, h)),
                  pl.BlockSpec((1, t, LANES), lambda bi, h, i: (bi, 0, h)),
                  pl.BlockSpec((1, LANES), lambda bi, h, i: (0, 0))],
        out_specs=pl.BlockSpec((1, tq, LANES), lambda bi, h, i: (bi, i, h)),
        out_shape=jax.ShapeDtypeStruct((b, t, DA_HEADS * DA_VDIM), F32),
        compiler_params=_params(("parallel", "parallel", "parallel")),
        name="diff_attn",
    )(lam, q, k, v, subln_g)


def _win_body(sink_ref, q_ref, kc_ref, vc_ref, kp_ref, kn_ref, kx_ref, vp_ref, vn_ref, vx_ref, o_ref,
              *, lo, s_len):
    l_len = kc_ref.shape[1]
    q = q_ref[0]
    lane = lax.broadcasted_iota(jnp.int32, (WG_BLOCK, LANES), 1)

    def run(kcat, vcat, valid):
        vlane = lax.broadcasted_iota(jnp.int32, vcat.shape, 1)
        vz = jnp.zeros_like(vcat)
        vhalf = [jnp.where(vlane < HEAD_DIM, vcat, vz), jnp.where(vlane >= HEAD_DIM, vcat, vz)]
        heads = [(g, kv) for g in range(WG_GROUP) for kv in range(WG_KV_HEADS)]
        zq = jnp.zeros((WG_BLOCK, LANES), q.dtype)
        half = [lane < HEAD_DIM, lane >= HEAD_DIM]
        qms = [jnp.where(half[kv], q[:, g * LANES:(g + 1) * LANES], zq) for g, kv in heads]
        ss = [lax.dot_general(qm, kcat, _NT, preferred_element_type=F32) for qm in qms]
        if valid is not None:
            ss = [jnp.where(valid, s, NEG) for s in ss]
        sks = [sink_ref[kv * WG_GROUP + g] * LOG2E for g, kv in heads]
        ms = [jnp.maximum(jnp.max(s, axis=-1, keepdims=True), sk) for s, sk in zip(ss, sks)]
        ps = [jnp.exp2(s - m) for s, m in zip(ss, ms)]
        dens = [jnp.sum(p, axis=-1, keepdims=True) + jnp.exp2(sk - m) for p, sk, m in zip(ps, sks, ms)]
        os = [jnp.dot(p.astype(BF16), vhalf[kv], preferred_element_type=F32) * (1.0 / den)
              for p, den, (g, kv) in zip(ps, dens, heads)]
        for g in range(WG_GROUP):
            o_ref[0, :, g * LANES:(g + 1) * LANES] = os[2 * g] + os[2 * g + 1]

    n = pl.program_id(1)

    @pl.when(n < lo)
    def _():
        run(kc_ref[0], vc_ref[0], None)

    @pl.when(n >= lo)
    def _():
        kcat = jnp.concatenate([kc_ref[0], kp_ref[0], kn_ref[0], kx_ref[0]], axis=0)
        vcat = jnp.concatenate([vc_ref[0], vp_ref[0], vn_ref[0], vx_ref[0]], axis=0)
        nk = kcat.shape[0]
        col = lax.broadcasted_iota(jnp.int32, (WG_BLOCK, nk), 1)
        row = lax.broadcasted_iota(jnp.int32, (WG_BLOCK, nk), 0)
        j = col - l_len
        rel = j - WG_BLOCK - row
        kpos = (n - lo) * WG_BLOCK + j - WG_BLOCK
        valid = (col < l_len) | ((jnp.abs(rel) <= WINDOW) & (kpos >= 0) & (kpos < s_len))
        run(kcat, vcat, valid)


def _win_attn(q, k, v, sink, l_len):
    b, t, _ = q.shape
    blk = WG_BLOCK
    lo, nb = l_len // blk, t // blk
    ctx_spec = pl.BlockSpec((1, l_len, LANES), lambda bi, n: (bi, 0, 0))
    win = [pl.BlockSpec((1, blk, LANES), lambda bi, n, d=d: (bi, jnp.clip(n + d, lo, nb - 1), 0))
           for d in (-1, 0, 1)]
    return pl.pallas_call(
        functools.partial(_win_body, lo=lo, s_len=t - l_len),
        grid=(b, nb),
        in_specs=[pl.BlockSpec(memory_space=pltpu.SMEM),
                  pl.BlockSpec((1, blk, 4 * LANES), lambda bi, n: (bi, n, 0)),
                  ctx_spec, ctx_spec] + win + win,
        out_specs=pl.BlockSpec((1, blk, 4 * LANES), lambda bi, n: (bi, n, 0)),
        out_shape=jax.ShapeDtypeStruct((b, t, 4 * LANES), F32),
        compiler_params=_params(("parallel", "parallel")),
        name="win_attn",
    )(sink, q, k, v, k, k, k, v, v, v)


GD_LOCAL_CHUNKS = 4


def _dot3s(x, r):
    m = x.shape[0]
    xh, xl = _split2(x)
    rh, rl = _split2(r)
    t = jnp.dot(jnp.concatenate([xh, xl], axis=0), rh, preferred_element_type=F32)
    return t[:m] + t[m:] + jnp.dot(xh, rl, preferred_element_type=F32)


def _gdn_local_body(q_ref, k_ref, v_ref, gb_ref, u_ref, w_ref, qg_ref, qk_ref, kdt_ref, egl_ref, *, nch):
    c = GD_CHUNK
    nh = GD_HEADS
    ii = lax.broadcasted_iota(jnp.int32, (c, nh * c), 0)
    jl = lax.broadcasted_iota(jnp.int32, (c, nh * c), 1)
    jj = jl % c
    blk = jl // c
    eye = ii == jj
    eyef = eye.astype(F32)
    tri = [(jj <= ii, jj < ii, jj >= ii), (jj >= ii, jj > ii, jj <= ii)]
    hl = lax.broadcasted_iota(jnp.int32, (c, nh * LANES), 1) // LANES
    e_i = lax.broadcasted_iota(jnp.int32, (LANES, LANES), 0)
    e_j = lax.broadcasted_iota(jnp.int32, (LANES, LANES), 1)
    eye_b = (e_i == e_j).astype(BF16)

    def cat(cols):
        out = jnp.zeros((c, nh * c), F32)
        for h in range(nh):
            out = jnp.where(blk == h, cols[h], out)
        return out

    def nat(cols):
        out = jnp.zeros((c, nh * LANES), F32)
        for h in range(nh):
            out = jnp.where(hl == h, cols[h], out)
        return out

    def bd_cat(p):
        return jnp.concatenate([jnp.where(blk == h, p, 0.0) for h in range(nh)], axis=0)

    def bd_nat(x):
        return jnp.concatenate([jnp.where(hl == h, x, 0.0) for h in range(nh)], axis=0)

    def setup(ch):
        rows = slice(ch * c, (ch + 1) * c)
        gb = gb_ref[0, rows, :]
        q = q_ref[0, rows, :]
        k = k_ref[0, rows, :]
        v = v_ref[0, rows, :]
        kdh, kdl = _split2(bd_nat(k))
        out = []
        for d in range(2):
            incl, strict, incl_t = tri[d]
            g_cols = [gb[:, d * nh + h:d * nh + h + 1] for h in range(nh)]
            b_cols = [gb[:, (2 + d) * nh + h:(2 + d) * nh + h + 1] for h in range(nh)]
            g_cat = cat(g_cols)
            g_row = jnp.sum(jnp.where(eye, g_cat, 0.0), axis=0, keepdims=True)
            gc_cols = [jnp.sum(jnp.where(incl & (blk == h), g_row, 0.0), axis=1, keepdims=True)
                       for h in range(nh)]
            grow = jnp.sum(jnp.where(incl_t, g_cat, 0.0), axis=0, keepdims=True)
            glast = [jnp.sum(g_cols[h], axis=0, keepdims=True) for h in range(nh)]
            decay = jnp.exp(jnp.where(incl, cat(gc_cols) - grow, NEG))
            b_nat = nat(b_cols)
            eg_nat = nat([jnp.exp(gc_cols[h]) for h in range(nh)])
            kb = k * b_nat
            kbh, kbl = _split2(kb)
            qg_ref[0, d, rows, :] = q * eg_nat
            kd = (k * nat([jnp.exp(glast[h] - gc_cols[h]) for h in range(nh)])).astype(BF16)
            egl_ref[0, d, ch] = nat([jnp.exp(glast[h]) for h in range(nh)])[0:1]
            out.append(dict(rows=rows, ch=ch, d=d, q=q, decay=decay, strict=strict, kbh=kbh, kbl=kbl,
                            kdh=kdh, kdl=kdl, kd=kd, vb=v * b_nat, kbeg=kb * eg_nat))
        return out

    st = [s for ch in range(nch) for s in setup(ch)]
    kk1 = [lax.dot_general(jnp.concatenate([s["kbh"], s["kbl"]], axis=0), s["kdh"], _NT,
                           preferred_element_type=F32) for s in st]
    kk2 = [lax.dot_general(s["kbh"], s["kdl"], _NT, preferred_element_type=F32) for s in st]
    a = [jnp.where(s["strict"], (x[:c] + x[c:] + y) * s["decay"], 0.0) for x, y, s in zip(kk1, kk2, st)]
    tinv = [eyef - x for x in a]
    p = [_dot3s(x, bd_cat(x)) for x in a]
    for _ in range(4):
        t = [_dot3s(jnp.concatenate([ti, pi], axis=0), bd_cat(pi)) for ti, pi in zip(tinv, p)]
        tinv = [ti + x[:c] for ti, x in zip(tinv, t)]
        p = [x[c:] for x in t]
    tinv = [ti + _dot3s(ti, bd_cat(pi)) for ti, pi in zip(tinv, p)]
    us = [_dot3s(ti, bd_nat(s["vb"])) for ti, s in zip(tinv, st)]
    ws = [_dot3s(ti, bd_nat(s["kbeg"])) for ti, s in zip(tinv, st)]
    qks = [lax.dot_general(s["q"].astype(BF16), s["kdh"], _NT, preferred_element_type=F32) for s in st]
    kdts = [jnp.concatenate([lax.dot_general(eye_b, s["kd"][:, h * LANES:(h + 1) * LANES], _NT,
                                             preferred_element_type=F32) for h in range(nh)], axis=0)
            for s in st]
    for s, u, w, qk, kdt in zip(st, us, ws, qks, kdts):
        u_ref[0, s["d"], s["rows"], :] = u
        w_ref[0, s["d"], s["rows"], :] = w
        qk_ref[0, s["d"], s["ch"]] = qk * s["decay"]
        kdt_ref[0, s["d"], s["ch"]] = kdt.astype(BF16)


def _gdn_scan_body(*refs):
    ins, o_refs, s_ref = (refs[:6], refs[6:12]), refs[12:14], refs[14]

    @pl.when(pl.program_id(1) == 0)
    def _():
        s_ref[...] = jnp.zeros_like(s_ref)

    c = GD_CHUNK
    heads = range(GD_HEADS)
    sls = [slice(h * LANES, (h + 1) * LANES) for h in heads]
    hl = lax.broadcasted_iota(jnp.int32, (c, GD_HEADS * LANES), 1) // LANES
    chains = [(d, h) for d in range(2) for h in heads]
    ns = ins[0][3].shape[2]
    ss = [s_ref[d, h] for d, h in chains]
    for step in range(ns):
        pos = [step, ns - 1 - step]
        rows = [slice(p * c, (p + 1) * c) for p in pos]
        rs = [_dot1(jnp.concatenate([ins[d][1][0, 0, rows[d], sls[h]], ins[d][2][0, 0, rows[d], sls[h]]],
                                    axis=0), s) for (d, h), s in zip(chains, ss)]
        v_news = [ins[d][0][0, 0, rows[d], sls[h]] - r[:c] for (d, h), r in zip(chains, rs)]
        upd = [_dot1(ins[d][4][0, 0, pos[d], sls[h], :], vn) for (d, h), vn in zip(chains, v_news)]
        for d in range(2):
            lo = d * GD_HEADS
            v_all = jnp.concatenate(v_news[lo:lo + GD_HEADS], axis=1)
            vbd = jnp.concatenate([jnp.where(hl == h, v_all, 0.0) for h in heads], axis=0)
            o_refs[d][0, rows[d], :] = (jnp.concatenate([r[c:] for r in rs[lo:lo + GD_HEADS]], axis=1)
                                        + _dot1(ins[d][3][0, 0, pos[d]], vbd))
        ss = [s * ins[d][5][0, 0, pos[d], :, sls[h]] + up for (d, h), s, up in zip(chains, ss, upd)]
    for (d, h), s in zip(chains, ss):
        s_ref[d, h] = s


def _gdn(q, k, v, gb, l_len):
    b, t, _ = q.shape
    c = GD_CHUNK
    nc, ncc = t // c, l_len // c
    nch = _pick(nc, (GD_LOCAL_CHUNKS, 1))
    h = GD_HEADS
    hw = h * LANES
    tokl = pl.BlockSpec((1, nch * c, hw), lambda bi, j: (bi, j, 0))
    tok4 = pl.BlockSpec((1, 2, nch * c, hw), lambda bi, j: (bi, 0, j, 0))
    ch5 = lambda r, w2: pl.BlockSpec((1, 2, nch, r, w2), lambda bi, j: (bi, 0, j, 0, 0))
    tok_shape = jax.ShapeDtypeStruct((b, 2, t, hw), F32)
    u, w, qg, qk, kdt, egl = pl.pallas_call(
        functools.partial(_gdn_local_body, nch=nch),
        grid=(b, nc // nch),
        in_specs=[tokl, tokl, tokl, pl.BlockSpec((1, nch * c, LANES), lambda bi, j: (bi, j, 0))],
        out_specs=[tok4, tok4, tok4, ch5(c, h * c), ch5(hw, c), ch5(1, hw)],
        out_shape=[tok_shape, tok_shape, tok_shape,
                   jax.ShapeDtypeStruct((b, 2, nc, c, h * c), F32),
                   jax.ShapeDtypeStruct((b, 2, nc, hw, c), BF16),
                   jax.ShapeDtypeStruct((b, 2, nc, 1, hw), F32)],
        compiler_params=_params(("parallel", "parallel")),
        name="gdn_local",
    )(q, k, v, gb)

    ns = _pick(math.gcd(ncc, nc), (4, 2, 1))

    def blk(d, j):
        first = j * ns
        last = jnp.where(first < ncc, ncc - 1 - first, nc - 1 - first + ncc) - (ns - 1)
        return j if d == 0 else last // ns

    def specs(d):
        stok = pl.BlockSpec((1, 1, ns * c, hw), lambda bi, j: (bi, d, blk(d, j), 0))
        sch = lambda r, w2: pl.BlockSpec((1, 1, ns, r, w2), lambda bi, j: (bi, d, blk(d, j), 0, 0))
        return [stok, stok, stok, sch(c, h * c), sch(hw, c), sch(1, hw)]

    out_spec = lambda d: pl.BlockSpec((1, ns * c, hw), lambda bi, j: (bi, blk(d, j), 0))
    out_shape = jax.ShapeDtypeStruct((b, t, hw), F32)
    return pl.pallas_call(
        _gdn_scan_body,
        grid=(b, nc // ns),
        in_specs=specs(0) + specs(1),
        out_specs=[out_spec(0), out_spec(1)],
        out_shape=[out_shape, out_shape],
        scratch_shapes=[pltpu.VMEM((2, h, GD_DK, GD_DV), F32)],
        compiler_params=_params(("parallel", "arbitrary")),
        name="gdn_scan",
    )(u, w, qg, qk, kdt, egl, u, w, qg, qk, kdt, egl)


def _merge_body(x_ref, oa_ref, ob_ref, og0_ref, og1_ref, z_ref, gl0_ref, gl1_ref, bg_ref, ng_ref,
                gt_ref, wb_ref, wo_ref, o_ref):
    d = x_ref.shape[1]
    parts = []
    for h in range(GD_HEADS):
        sl = slice(h * LANES, (h + 1) * LANES)
        o = og0_ref[:, sl] + og1_ref[:, sl]
        z = z_ref[:, sl]
        o = o * lax.rsqrt(jnp.mean(o * o, axis=-1, keepdims=True) + EPS) * ng_ref[...]
        parts.append(o * (z * jax.nn.sigmoid(z)))
    oc = jnp.concatenate(parts, axis=1)
    gl = jnp.concatenate([gl0_ref[...], gl1_ref[...]], axis=1)
    merged = None
    for r, o_r in enumerate((oa_ref[...], ob_ref[...], oc)):
        y = jnp.dot(o_r.astype(BF16), wb_ref[r], preferred_element_type=F32)
        t = jax.nn.sigmoid(gl[:, r * d:(r + 1) * d] + bg_ref[:, r * d:(r + 1) * d]) * y
        merged = t if merged is None else merged + t
    m = jnp.dot(merged.astype(BF16), wo_ref[...], preferred_element_type=F32)
    o_ref[...] = x_ref[...] + gt_ref[0] * m


def _merge(xs, oa, ob, og, proj, b_gate, norm_g, mod, wb, wo, l, bn, t_len, l_len):
    n, d = xs.shape
    tm = _pick(math.gcd(l_len, t_len), (256, 128))
    tpb, lt = t_len // tm, l_len // tm
    tok = lambda w: pl.BlockSpec((tm, w), lambda i: (i, 0))
    full = lambda a: pl.BlockSpec(a.shape, lambda i: (0,) * a.ndim)
    layer = lambda a: pl.BlockSpec((None,) + a.shape[1:], lambda i: (l,) + (0,) * (a.ndim - 1))
    return pl.pallas_call(
        _merge_body,
        grid=(n // tm,),
        in_specs=[tok(d), tok(BRANCH_W), tok(BRANCH_W), tok(BRANCH_W), tok(BRANCH_W),
                  pl.BlockSpec((tm, BRANCH_W), lambda i: (i, C_ZC // BRANCH_W)),
                  pl.BlockSpec((tm, CONV_BLK), lambda i: (i, C_GL // CONV_BLK)),
                  pl.BlockSpec((tm, CONV_BLK), lambda i: (i, C_GL // CONV_BLK + 1)),
                  full(b_gate), full(norm_g),
                  pl.BlockSpec((1, 1, d), lambda i: (_row_sel(i, tpb, lt, bn) * 6 + 2, 0, 0)),
                  layer(wb), layer(wo)],
        out_specs=tok(d),
        out_shape=jax.ShapeDtypeStruct((n, d), F32),
        compiler_params=_params(("parallel",)),
        name="merge",
    )(xs, oa, ob, og[0].reshape(n, BRANCH_W), og[1].reshape(n, BRANCH_W), proj, proj, proj,
      b_gate, norm_g, mod, wb, wo)


_CAND = [(r1, r2) for r1 in range(1, PEER_TOPK + 1) for r2 in range(1, PEER_TOPK + 1)
         if r1 * r2 <= PEER_TOPK]


def _extract_top(cur, n, want_rank=False):
    rank = jnp.full(cur.shape, 99.0, F32) if want_rank else None
    vals = []
    for r in range(n):
        m = jnp.max(cur, axis=0, keepdims=True)
        hit = cur == m
        if want_rank:
            rank = jnp.where(hit, float(r + 1), rank)
        cur = jnp.where(hit, -jnp.inf, cur)
        vals.append(m)
    return vals, rank


def _topk_body(x_ref, g_ref, sh_ref, sc_ref, wq_ref, keys_ref, h_ref, cnt1_ref, e1_ref, rk2_ref, e2_ref):
    hm = _modulate(x_ref[...], g_ref[...], sh_ref[0], sc_ref[0])
    h_ref[...] = (hm * math.sqrt(0.5)).astype(BF16)
    hmod = hm.astype(BF16)
    nr = PEER_TOPK
    for h in range(PEER_HEADS):
        st = []
        for c in range(2):
            col = (2 * h + c) * LANES
            qs = jnp.dot(hmod, wq_ref[:, col:col + LANES], preferred_element_type=F32)
            st.append(_dot3(keys_ref[c], qs, _NT))
        a, _ = _extract_top(st[0], nr)
        b, rank2 = _extract_top(st[1], nr, want_rank=True)
        rows = [a[r1 - 1] + b[r2 - 1] for r1, r2 in _CAND]
        pad = (-len(rows)) % SUBLANES
        cand = jnp.concatenate(rows + [jnp.full_like(rows[0], -jnp.inf)] * pad, axis=0)
        top, _ = _extract_top(cand, nr)
        tau = top[PEER_TOPK - 1]
        z = jnp.sum(jnp.where(cand >= tau, jnp.exp(cand - top[0]), 0.0), axis=0, keepdims=True)
        cnt1 = jnp.zeros_like(st[0])
        for r1 in range(1, PEER_TOPK + 1):
            cnt = None
            for r2 in range(1, PEER_TOPK + 1):
                if r1 * r2 <= PEER_TOPK:
                    t = (a[r1 - 1] + b[r2 - 1] >= tau).astype(F32)
                    cnt = t if cnt is None else cnt + t
            cnt1 = jnp.where(st[0] == a[r1 - 1], cnt, cnt1)
        cnt1_ref[h] = cnt1
        e1_ref[h] = jnp.exp(st[0] - a[0])
        rk2_ref[h] = rank2.astype(BF16)
        e2_ref[h] = (jnp.exp(st[1] - b[0]) * (1.0 / z)).astype(BF16)


def _peer_topk(xs, g, mod, wq, keys, l, bn, t_len, l_len):
    n, d = xs.shape
    tm = _pick(math.gcd(l_len, t_len), (256, 128))
    tpb, lt = t_len // tm, l_len // tm
    sel = lambda i: _row_sel(i, tpb, lt, bn)
    out = pl.BlockSpec((PEER_HEADS, N_KEYS, tm), lambda i: (0, 0, i))
    shp = lambda dt: jax.ShapeDtypeStruct((PEER_HEADS, N_KEYS, n), dt)
    return pl.pallas_call(
        _topk_body,
        grid=(n // tm,),
        in_specs=[pl.BlockSpec((tm, d), lambda i: (i, 0)),
                  pl.BlockSpec((1, d), lambda i: (0, 0)),
                  pl.BlockSpec((1, 1, d), lambda i: (sel(i) * 6 + 3, 0, 0)),
                  pl.BlockSpec((1, 1, d), lambda i: (sel(i) * 6 + 4, 0, 0)),
                  pl.BlockSpec((None,) + wq.shape[1:], lambda i: (l, 0, 0)),
                  pl.BlockSpec((None,) + keys.shape[1:], lambda i: (l, 0, 0, 0))],
        out_specs=[pl.BlockSpec((tm, d), lambda i: (i, 0)), out, out, out, out],
        out_shape=[jax.ShapeDtypeStruct((n, d), BF16), shp(F32), shp(F32), shp(BF16), shp(BF16)],
        compiler_params=_params(("parallel",)),
        name="peer_topk",
    )(xs, g, mod, mod, wq, keys)


DENSE_CHUNK = 256
DENSE_KDEPTH = 1024
DENSE_AHEAD = 3


def _dense_body(x_ref, h_ref, u_ref, vt_ref, cnt1_ref, e1_ref, rk2_ref, e2_ref, *rest, te):
    gt_refs, o_ref, acc_ref = rest[:-2], rest[-2], rest[-1]
    e = pl.program_id(1)

    @pl.when(e == 0)
    def _():
        acc_ref[...] = jnp.zeros_like(acc_ref)

    hs = h_ref[...]
    zero = jnp.zeros((N_KEYS, hs.shape[0]), BF16)
    ck = DENSE_CHUNK
    nck = te // ck

    def mm1(c):
        return lax.dot_general(u_ref[c * ck:(c + 1) * ck, :], hs, _NT, preferred_element_type=F32)

    def gate(c):
        parts = []
        for j in range(c * ck // N_KEYS, (c + 1) * ck // N_KEYS):
            i1 = e * (te // N_KEYS) + j
            g = zero
            for h in range(PEER_HEADS):
                cnt = cnt1_ref[h, pl.ds(i1, 1), :].astype(BF16)
                w1 = e1_ref[h, pl.ds(i1, 1), :].astype(BF16)
                g = g + jnp.where(rk2_ref[h] <= cnt, e2_ref[h], zero) * w1
            parts.append(g)
        return jnp.concatenate(parts, axis=0)

    def mm2(c0, c1):
        hid = jnp.concatenate(hids[c0:c1], axis=0)
        acc_ref[...] += jnp.dot(vt_ref[:, c0 * ck:c1 * ck], hid, preferred_element_type=F32)

    grp = max(1, DENSE_KDEPTH // ck)
    acts = [mm1(c) for c in range(min(DENSE_AHEAD, nck))]
    hids = []
    pending = None
    for c in range(nck):
        if c + DENSE_AHEAD < nck:
            acts.append(mm1(c + DENSE_AHEAD))
        y = acts[c]
        hids.append((y * (1.0 + lax.erf(y))).astype(BF16) * gate(c))
        if pending is not None:
            mm2(*pending)
            pending = None
        if c % grp == grp - 1:
            pending = (c + 1 - grp, c + 1)
    mm2(*pending)

    @pl.when(e == pl.num_programs(1) - 1)
    def _():
        base = acc_ref.shape[1] // len(gt_refs)
        for k, gt_ref in enumerate(gt_refs):
            rs = slice(k * base, (k + 1) * base)
            o_ref[rs] = x_ref[rs] + (gt_ref[0] * math.sqrt(0.5)) * acc_ref[:, rs].T


def _peer_dense(xs, hmod, u, vt, cnt1, e1, rk2, e2, mod, l, bn, t_len, l_len):
    n, d = xs.shape
    ne = u.shape[1]
    base = _pick(math.gcd(l_len, t_len), (256, 128))
    nsub = 2 if n % (2 * base) == 0 else 1
    tm = nsub * base
    te = _pick(ne, (2048, 1024, 512))
    tpb, lt = t_len // base, l_len // base
    gts = [pl.BlockSpec((1, 1, d), lambda i, e, k=k: (_row_sel(i * nsub + k, tpb, lt, bn) * 6 + 5, 0, 0))
           for k in range(nsub)]
    sel = pl.BlockSpec((PEER_HEADS, N_KEYS, tm), lambda i, e: (0, 0, i))
    tok = pl.BlockSpec((tm, d), lambda i, e: (i, 0))
    return pl.pallas_call(
        functools.partial(_dense_body, te=te),
        grid=(n // tm, ne // te),
        in_specs=[tok, tok,
                  pl.BlockSpec((None, te, d), lambda i, e: (l, e, 0)),
                  pl.BlockSpec((None, d, te), lambda i, e: (l, 0, e)),
                  sel, sel, sel, sel] + gts,
        out_specs=tok,
        out_shape=jax.ShapeDtypeStruct((n, d), F32),
        scratch_shapes=[pltpu.VMEM((d, tm), F32)],
        compiler_params=_params(("parallel", "arbitrary")),
        name="peer_dense",
    )(xs, hmod, u, vt, cnt1, e1, rk2, e2, *([mod] * nsub))


def _rope_tables(rows, l_len):
    row = jnp.repeat(jnp.arange(rows), GRID_W).astype(F32)
    col = jnp.tile(jnp.arange(GRID_W), rows).astype(F32)
    inv = jnp.power(ROPE_THETA, -jnp.arange(ROPE_PAIRS, dtype=F32) / ROPE_PAIRS)
    ar = row[:, None] * inv
    ac = col[:, None] * inv
    ang = jnp.concatenate([ar, ar, ac, ac], axis=-1)
    cos = jnp.concatenate([jnp.ones((l_len, HEAD_DIM), F32), jnp.cos(ang)], axis=0)
    sin = jnp.concatenate([jnp.zeros((l_len, HEAD_DIM), F32), jnp.sin(ang)], axis=0)
    cos = jnp.tile(cos, (1, 2))
    sin = jnp.tile(sin, (1, 2))
    first = (jnp.arange(LANES) % (2 * ROPE_PAIRS)) < ROPE_PAIRS
    return cos, jnp.where(first, -sin, 0.0), jnp.where(first, 0.0, sin)


def _reorder_w_in(w):
    nl, d = w.shape[:2]
    qa_ka_va, qb, kb_vb = w[..., :1536], w[..., 1536:2048], w[..., 2048:2304]
    qkvc, zc, ab, gl = w[..., 2304:3840], w[..., 3840:4352], w[..., 4352:4368], w[..., 4368:]
    qb = qb.reshape(nl, d, WG_KV_HEADS, WG_GROUP, HEAD_DIM).transpose(0, 1, 3, 2, 4).reshape(nl, d, 512)
    pad = jnp.zeros((nl, d, C_GL - C_AB - ab.shape[-1]), w.dtype)
    return jnp.concatenate([qa_ka_va, qkvc, zc, qb, kb_vb, ab, pad, gl], axis=-1).astype(BF16)


def _reorder_w_branch(wb):
    nl, _, _, d = wb.shape
    wb1 = wb[:, 1].reshape(nl, WG_KV_HEADS, WG_GROUP, HEAD_DIM, d).transpose(0, 2, 1, 3, 4)
    return jnp.stack([wb[:, 0], wb1.reshape(nl, BRANCH_W, d), wb[:, 2]], axis=1).astype(BF16)


def kernel(x, c, ctx, c_ctx, norm1_g, norm2_g, w_ada, b_ada, w_in, b_gate, qk_norm_g, diff_lam,
           diff_subln_g, wg_sink, gd_conv_w, gd_a_log, gd_dt_bias, gd_norm_g, w_branch, w_out,
           peer_wq, peer_keys, peer_u, peer_v):
    bn, s_len, d = x.shape
    l_len = ctx.shape[1]
    t_len = l_len + s_len
    n_tok = bn * t_len
    depth = w_in.shape[0]
    cos, sa, sb = _rope_tables(s_len // GRID_W, l_len)

    xs = jnp.concatenate([ctx, x], axis=1).reshape(n_tok, d)
    cvec = jnp.concatenate([c, c_ctx[None]], axis=0)
    cact = jnp.pad(jax.nn.silu(cvec), ((0, 16 - (bn + 1) % 16), (0, 0))).astype(BF16)
    lane_pad = lambda a: jnp.pad(a.reshape(1, -1), ((0, 0), (0, LANES - a.size)))

    w_ada_b = w_ada.astype(BF16)
    w_in_b = _reorder_w_in(w_in)
    w_branch_b = _reorder_w_branch(w_branch)
    w_out_b = w_out.astype(BF16)
    wq_b = peer_wq.astype(BF16)
    u_b = peer_u.astype(BF16)
    vt_b = jnp.swapaxes(peer_v.astype(BF16), 1, 2)

    for l in range(depth):
        lam_init = 0.8 - 0.6 * math.exp(-0.3 * l)
        mod = (_mm(cact, w_ada_b, l)[:bn + 1] + b_ada[l]).reshape((bn + 1) * 6, 1, d)

        proj = _inproj(xs, norm1_g[l].reshape(1, d), mod, w_in_b, l, bn, t_len, l_len)
        qa, ka, va, qb, kb, vb, gq, gk, gv, gb = _prep(
            proj, cos, sa, sb, jnp.tile(qk_norm_g[l], (1, 2)), gd_conv_w[l],
            lane_pad(gd_a_log[l]), lane_pad(gd_dt_bias[l]), t_len, l_len)
        b3 = lambda a: a.reshape(bn, t_len, a.shape[-1])

        lv = diff_lam[l]
        lam = (jnp.exp(jnp.sum(lv[0, 0] * lv[0, 1])) - jnp.exp(jnp.sum(lv[1, 0] * lv[1, 1])) + lam_init)
        oa = _diff_attn(b3(qa), b3(ka), b3(va), lam.reshape(1), diff_subln_g[l].reshape(1, DA_VDIM),
                        1.0 - lam_init, l_len)
        ob = _win_attn(b3(qb), b3(kb), b3(vb), wg_sink[l], l_len)
        og = _gdn(b3(gq), b3(gk), b3(gv), b3(gb), l_len)

        xs = _merge(xs, oa.reshape(n_tok, BRANCH_W), ob.reshape(n_tok, BRANCH_W), og, proj,
                    b_gate[l].reshape(1, -1), gd_norm_g[l].reshape(1, GD_DV), mod, w_branch_b, w_out_b,
                    l, bn, t_len, l_len)

        hmod, cnt1, e1, rk2, e2 = _peer_topk(xs, norm2_g[l].reshape(1, d), mod, wq_b, peer_keys,
                                             l, bn, t_len, l_len)
        xs = _peer_dense(xs, hmod, u_b, vt_b, cnt1, e1, rk2, e2, mod, l, bn, t_len, l_len)

    return xs.reshape(bn, t_len, d)[:, l_len:]
```

```python
import functools
import math

import jax
import jax.numpy as jnp
from jax import lax
from jax.experimental import pallas as pl
from jax.experimental.pallas import tpu as pltpu

F32 = jnp.float32
BF16 = jnp.bfloat16

EPS = 1e-6
GRID_W = 64
HEAD_DIM = 64
ROPE_THETA = 10000.0
ROPE_PAIRS = HEAD_DIM // 4
DA_HEADS = 4
DA_VDIM = 2 * HEAD_DIM
WG_HEADS = 8
WG_KV_HEADS = 2
WG_GROUP = WG_HEADS // WG_KV_HEADS
WINDOW = 128
WG_BLOCK = 128
GD_HEADS = 4
GD_DK = 128
GD_DV = 128
GD_CONV = 5
GD_CHUNK = 64
GD_QKV_W = GD_HEADS * (2 * GD_DK + GD_DV)
N_BRANCH = 3
BRANCH_W = 512
PEER_HEADS = 8
PEER_HALF = 128
N_KEYS = 128
PEER_TOPK = 16

LANES = 128
SUBLANES = 8
VMEM_LIMIT = 56 * 1024 * 1024
NEG = -1e30
LOG2E = math.log2(math.e)

C_QA, C_KA, C_VA, C_QKVC, C_ZC, C_QB, C_KB, C_VB, C_AB, C_GL = (
    0, 512, 1024, 1536, 3072, 3584, 4096, 4224, 4352, 4608)
PROJ_W = C_GL + N_BRANCH * 1024
PREP_W = C_GL
CONV_BLK = 1536


def _pick(n, cands):
    for c in cands:
        if n % c == 0:
            return c
    raise ValueError(f"no tile in {cands} divides {n}")


def _params(sem):
    return pltpu.CompilerParams(dimension_semantics=sem, vmem_limit_bytes=VMEM_LIMIT)


def _split2(a):
    hi = a.astype(BF16)
    return hi, (a - hi.astype(F32)).astype(BF16)


def _dot1(a, b, dn=None):
    a = a.astype(BF16)
    b = b.astype(BF16)
    if dn is None:
        return jnp.dot(a, b, preferred_element_type=F32)
    return lax.dot_general(a, b, dn, preferred_element_type=F32)


def _dot3(a, b, dn=None):
    ah, al = _split2(a)
    bh, bl = _split2(b)
    return _dot1(ah, bh, dn) + (_dot1(ah, bl, dn) + _dot1(al, bh, dn))


_NT = (((1,), (1,)), ((), ()))


def _row_sel(i, tpb, lt, bn):
    return jnp.where(i % tpb < lt, bn, i // tpb)


def _mm_body(x_ref, w_ref, o_ref):
    o_ref[...] = jnp.dot(x_ref[...], w_ref[...], preferred_element_type=F32)


def _mm(x, w, l):
    m, k = x.shape
    n = w.shape[2]
    tn = _pick(n, (1536, 1024, 512, 256, 128))
    return pl.pallas_call(
        _mm_body,
        grid=(n // tn,),
        in_specs=[pl.BlockSpec((m, k), lambda j: (0, 0)),
                  pl.BlockSpec((None, k, tn), lambda j: (l, 0, j))],
        out_specs=pl.BlockSpec((m, tn), lambda j: (0, j)),
        out_shape=jax.ShapeDtypeStruct((m, n), F32),
        compiler_params=_params(("parallel",)),
        name="ada_mm",
    )(x, w)


def _modulate(x, g, sh, sc):
    h = x * lax.rsqrt(jnp.mean(x * x, axis=-1, keepdims=True) + EPS) * g
    return h * (1.0 + sc) + sh


def _inproj_body(x_ref, g_ref, sh_ref, sc_ref, w_ref, o_ref):
    h = _modulate(x_ref[...], g_ref[...], sh_ref[0], sc_ref[0])
    o_ref[...] = jnp.dot(h.astype(BF16), w_ref[...], preferred_element_type=F32)


def _inproj(xs, g, mod, w, l, bn, t_len, l_len):
    n, d = xs.shape
    tm = _pick(math.gcd(l_len, t_len), (512, 256, 128))
    nw = w.shape[2]
    tn = nw // 2
    tpb, lt = t_len // tm, l_len // tm
    sel = lambda j, i: _row_sel(i, tpb, lt, bn)
    return pl.pallas_call(
        _inproj_body,
        grid=(nw // tn, n // tm),
        in_specs=[pl.BlockSpec((tm, d), lambda j, i: (i, 0)),
                  pl.BlockSpec((1, d), lambda j, i: (0, 0)),
                  pl.BlockSpec((1, 1, d), lambda j, i: (sel(j, i) * 6, 0, 0)),
                  pl.BlockSpec((1, 1, d), lambda j, i: (sel(j, i) * 6 + 1, 0, 0)),
                  pl.BlockSpec((None, d, tn), lambda j, i: (l, 0, j))],
        out_specs=pl.BlockSpec((tm, tn), lambda j, i: (i, j)),
        out_shape=jax.ShapeDtypeStruct((n, nw), F32),
        compiler_params=_params(("parallel", "parallel")),
        name="inproj",
    )(xs, g, mod, mod, w)


def _prep_body(p_ref, prev_ref, next_ref, cos_ref, sa_ref, sb_ref, qkg_ref, cw_ref, alog_ref, dtb_ref,
               qa_ref, ka_ref, va_ref, qb_ref, kb_ref, vb_ref, gq_ref, gk_ref, gv_ref, gb_ref,
               xe_ref, *, tpb, lt):
    tm = p_ref.shape[0]
    li = lax.broadcasted_iota(jnp.int32, (LANES, LANES), 0)
    lj = lax.broadcasted_iota(jnp.int32, (LANES, LANES), 1)
    seg = jnp.where(li // HEAD_DIM == lj // HEAD_DIM, 1.0 / HEAD_DIM, 0.0).astype(BF16)
    cos = cos_ref[...]
    sa = sa_ref[...]
    sb = sb_ref[...]

    def normrope(x, gain):
        yh, yl = _split2(x * x)
        ms = jnp.dot(yh, seg, preferred_element_type=F32) + jnp.dot(yl, seg, preferred_element_type=F32)
        xn = x * lax.rsqrt(ms + EPS) * gain
        return xn * cos + pltpu.roll(xn, LANES - ROPE_PAIRS, 1) * sa + pltpu.roll(xn, ROPE_PAIRS, 1) * sb

    scale = HEAD_DIM ** -0.5 * LOG2E
    for c in range(4):
        cs = slice(c * LANES, (c + 1) * LANES)
        qa_ref[:, cs] = (normrope(p_ref[:, C_QA + c * LANES:C_QA + (c + 1) * LANES], qkg_ref[0:1]) * scale).astype(BF16)
        ka_ref[:, cs] = normrope(p_ref[:, C_KA + c * LANES:C_KA + (c + 1) * LANES], qkg_ref[1:2]).astype(BF16)
        qb_ref[:, cs] = (normrope(p_ref[:, C_QB + c * LANES:C_QB + (c + 1) * LANES], qkg_ref[2:3]) * scale).astype(BF16)
    kb_ref[...] = normrope(p_ref[:, C_KB:C_KB + LANES], qkg_ref[3:4]).astype(BF16)
    va_ref[...] = p_ref[:, C_VA:C_VA + 512].astype(BF16)
    vb_ref[...] = p_ref[:, C_VB:C_VB + LANES].astype(BF16)

    ti = pl.program_id(0) % tpb
    at_start = (ti == 0) | (ti == lt)
    at_end = (ti == lt - 1) | (ti == tpb - 1)
    xe_ref[0:SUBLANES] = jnp.where(at_start, 0.0, prev_ref[...])
    xe_ref[SUBLANES:SUBLANES + tm] = p_ref[:, C_QKVC:C_QKVC + GD_QKV_W]
    xe_ref[SUBLANES + tm:2 * SUBLANES + tm] = jnp.where(at_end, 0.0, next_ref[...])
    half = GD_CONV // 2
    y = None
    for i in range(GD_CONV):
        t = xe_ref[pl.ds(SUBLANES - half + i, tm), :] * cw_ref[i:i + 1]
        y = t if y is None else y + t
    y = y * jax.nn.sigmoid(y)
    for h in range(GD_HEADS):
        hs = slice(h * LANES, (h + 1) * LANES)
        q = y[:, h * LANES:(h + 1) * LANES]
        k = y[:, 512 + h * LANES:512 + (h + 1) * LANES]
        gq_ref[:, hs] = q * lax.rsqrt(jnp.sum(q * q, axis=-1, keepdims=True) + EPS) * (GD_DK ** -0.5)
        gk_ref[:, hs] = k * lax.rsqrt(jnp.sum(k * k, axis=-1, keepdims=True) + EPS)
    gv_ref[...] = y[:, 1024:]

    ab = p_ref[:, C_AB:C_AB + LANES]
    lane = lax.broadcasted_iota(jnp.int32, ab.shape, 1)
    gdec = -jnp.exp(alog_ref[...]) * jnp.logaddexp(ab + dtb_ref[...], 0.0)
    gb_ref[...] = jnp.where(lane < 2 * GD_HEADS, gdec, jax.nn.sigmoid(ab))


def _prep(proj, cos, sa, sb, qkg, cw, alog, dtb, t_len, l_len):
    n = proj.shape[0]
    tm = _pick(math.gcd(l_len, t_len), (256, 128))
    tpb, lt = t_len // tm, l_len // tm
    r8 = tm // SUBLANES
    nb8 = n // SUBLANES
    tok = lambda w: pl.BlockSpec((tm, w), lambda i: (i, 0))
    rope = pl.BlockSpec((tm, LANES), lambda i: (i % tpb, 0))
    full = lambda a: pl.BlockSpec(a.shape, lambda i: (0,) * a.ndim)
    shp = lambda w, dt: jax.ShapeDtypeStruct((n, w), dt)
    return pl.pallas_call(
        functools.partial(_prep_body, tpb=tpb, lt=lt),
        grid=(n // tm,),
        in_specs=[pl.BlockSpec((tm, PREP_W), lambda i: (i, 0)),
                  pl.BlockSpec((SUBLANES, CONV_BLK), lambda i: (jnp.maximum(i * r8 - 1, 0), 1)),
                  pl.BlockSpec((SUBLANES, CONV_BLK), lambda i: (jnp.minimum((i + 1) * r8, nb8 - 1), 1)),
                  rope, rope, rope, full(qkg), full(cw), full(alog), full(dtb)],
        out_specs=[tok(512), tok(512), tok(512), tok(512), tok(LANES), tok(LANES),
                   tok(512), tok(512), tok(512), tok(LANES)],
        out_shape=[shp(512, BF16), shp(512, BF16), shp(512, BF16), shp(512, BF16), shp(LANES, BF16),
                   shp(LANES, BF16), shp(512, F32), shp(512, F32), shp(512, F32), shp(LANES, F32)],
        scratch_shapes=[pltpu.VMEM((tm + 2 * SUBLANES, GD_QKV_W), F32)],
        compiler_params=_params(("parallel",)),
        name="prep",
    )(proj, proj, proj, cos, sa, sb, qkg, cw, alog, dtb)


DA_PAIR = 2


def _diff_body(lam_ref, q_ref, k_ref, v_ref, g_ref, o_ref, *, post, l_len, n_ctx):
    heads = [slice(h * LANES, (h + 1) * LANES) for h in range(DA_PAIR)]
    lane = lax.broadcasted_iota(jnp.int32, (q_ref.shape[1], LANES), 1)
    halves = [lane < HEAD_DIM, lane >= HEAD_DIM]

    def run(nk):
        def attend(s, v):
            p = jnp.exp2(s - jnp.max(s, axis=-1, keepdims=True))
            l = jnp.sum(p, axis=-1, keepdims=True)
            return jnp.dot(p.astype(BF16), v, preferred_element_type=F32) * (1.0 / l)

        ss = []
        for hs in heads:
            q = q_ref[0, :, hs]
            k = k_ref[0, :nk, hs]
            ss.append([lax.dot_general(jnp.where(m, q, jnp.zeros_like(q)), k, _NT,
                                       preferred_element_type=F32) for m in halves])
        for hs, (s1, s2) in zip(heads, ss):
            v = v_ref[0, :nk, hs]
            o = attend(s1, v) - lam_ref[0] * attend(s2, v)
            ms = jnp.mean(o * o, axis=-1, keepdims=True)
            o_ref[0, :, hs] = o * lax.rsqrt(ms + EPS) * (g_ref[...] * post)

    i = pl.program_id(2)

    @pl.when(i < n_ctx)
    def _():
        run(l_len)

    @pl.when(i >= n_ctx)
    def _():
        run(k_ref.shape[1])


def _diff_attn(q, k, v, lam, subln_g, post, l_len):
    b, t, _ = q.shape
    tq = _pick(math.gcd(l_len, t), (256, 128))
    pw = DA_PAIR * LANES
    return pl.pallas_call(
        functools.partial(_diff_body, post=post, l_len=l_len, n_ctx=l_len // tq),
        grid=(b, DA_HEADS // DA_PAIR, t // tq),
        in_specs=[pl.BlockSpec(memory_space=pltpu.SMEM),
                  pl.BlockSpec((1, tq, pw), lambda bi, h, i: (bi, i, h)),
                  pl.BlockSpec((1, t, pw), lambda bi, h, i: (bi, 0, h)),
                  pl.BlockSpec((1, t, pw), lambda bi, h, i: (bi, 0, h)),
                  pl.BlockSpec((1, LANES), lambda bi, h, i: (0, 0))],
        out_specs=pl.BlockSpec((1, tq, pw), lambda bi, h, i: (bi, i, h)),
        out_shape=jax.ShapeDtypeStruct((b, t, DA_HEADS * DA_VDIM), F32),
        compiler_params=_params(("parallel", "parallel", "parallel")),
        name="diff_attn",
    )(lam, q, k, v, subln_g)


def _win_body(sink_ref, q_ref, kc_ref, vc_ref, kp_ref, kn_ref, kx_ref, vp_ref, vn_ref, vx_ref, o_ref,
              *, lo, s_len):
    l_len = kc_ref.shape[1]
    q = q_ref[0]
    lane = lax.broadcasted_iota(jnp.int32, (WG_BLOCK, LANES), 1)

    def run(kcat, vcat, valid):
        vlane = lax.broadcasted_iota(jnp.int32, vcat.shape, 1)
        vz = jnp.zeros_like(vcat)
        vhalf = [jnp.where(vlane < HEAD_DIM, vcat, vz), jnp.where(vlane >= HEAD_DIM, vcat, vz)]
        heads = [(g, kv) for g in range(WG_GROUP) for kv in range(WG_KV_HEADS)]
        zq = jnp.zeros((WG_BLOCK, LANES), q.dtype)
        half = [lane < HEAD_DIM, lane >= HEAD_DIM]
        qms = [jnp.where(half[kv], q[:, g * LANES:(g + 1) * LANES], zq) for g, kv in heads]
        ss = [lax.dot_general(qm, kcat, _NT, preferred_element_type=F32) for qm in qms]
        if valid is not None:
            ss = [jnp.where(valid, s, NEG) for s in ss]
        sks = [sink_ref[kv * WG_GROUP + g] * LOG2E for g, kv in heads]
        ms = [jnp.maximum(jnp.max(s, axis=-1, keepdims=True), sk) for s, sk in zip(ss, sks)]
        ps = [jnp.exp2(s - m) for s, m in zip(ss, ms)]
        dens = [jnp.sum(p, axis=-1, keepdims=True) + jnp.exp2(sk - m) for p, sk, m in zip(ps, sks, ms)]
        os = [jnp.dot(p.astype(BF16), vhalf[kv], preferred_element_type=F32) * (1.0 / den)
              for p, den, (g, kv) in zip(ps, dens, heads)]
        for g in range(WG_GROUP):
            o_ref[0, :, g * LANES:(g + 1) * LANES] = os[2 * g] + os[2 * g + 1]

    n = pl.program_id(1)

    @pl.when(n < lo)
    def _():
        run(kc_ref[0], vc_ref[0], None)

    @pl.when(n >= lo)
    def _():
        kcat = jnp.concatenate([kc_ref[0], kp_ref[0], kn_ref[0], kx_ref[0]], axis=0)
        vcat = jnp.concatenate([vc_ref[0], vp_ref[0], vn_ref[0], vx_ref[0]], axis=0)
        nk = kcat.shape[0]
        col = lax.broadcasted_iota(jnp.int32, (WG_BLOCK, nk), 1)
        row = lax.broadcasted_iota(jnp.int32, (WG_BLOCK, nk), 0)
        j = col - l_len
        rel = j - WG_BLOCK - row
        kpos = (n - lo) * WG_BLOCK + j - WG_BLOCK
        valid = (col < l_len) | ((jnp.abs(rel) <= WINDOW) & (kpos >= 0) & (kpos < s_len))
        run(kcat, vcat, valid)


def _win_attn(q, k, v, sink, l_len):
    b, t, _ = q.shape
    blk = WG_BLOCK
    lo, nb = l_len // blk, t // blk
    ctx_spec = pl.BlockSpec((1, l_len, LANES), lambda bi, n: (bi, 0, 0))
    win = [pl.BlockSpec((1, blk, LANES), lambda bi, n, d=d: (bi, jnp.clip(n + d, lo, nb - 1), 0))
           for d in (-1, 0, 1)]
    return pl.pallas_call(
        functools.partial(_win_body, lo=lo, s_len=t - l_len),
        grid=(b, nb),
        in_specs=[pl.BlockSpec(memory_space=pltpu.SMEM),
                  pl.BlockSpec((1, blk, 4 * LANES), lambda bi, n: (bi, n, 0)),
                  ctx_spec, ctx_spec] + win + win,
        out_specs=pl.BlockSpec((1, blk, 4 * LANES), lambda bi, n: (bi, n, 0)),
        out_shape=jax.ShapeDtypeStruct((b, t, 4 * LANES), F32),
        compiler_params=_params(("parallel", "parallel")),
        name="win_attn",
    )(sink, q, k, v, k, k, k, v, v, v)


GD_LOCAL_CHUNKS = 4


def _dot3s(x, r):
    m = x.shape[0]
    xh, xl = _split2(x)
    rh, rl = _split2(r)
    t = jnp.dot(jnp.concatenate([xh, xl], axis=0), rh, preferred_element_type=F32)
    return t[:m] + t[m:] + jnp.dot(xh, rl, preferred_element_type=F32)


def _gdn_local_body(q_ref, k_ref, v_ref, gb_ref, u_ref, w_ref, qg_ref, qk_ref, kdt_ref, egl_ref, *, nch):
    c = GD_CHUNK
    nh = GD_HEADS
    ii = lax.broadcasted_iota(jnp.int32, (c, nh * c), 0)
    jl = lax.broadcasted_iota(jnp.int32, (c, nh * c), 1)
    jj = jl % c
    blk = jl // c
    eye = ii == jj
    eyef = eye.astype(F32)
    tri = [(jj <= ii, jj < ii, jj >= ii), (jj >= ii, jj > ii, jj <= ii)]
    hl = lax.broadcasted_iota(jnp.int32, (c, nh * LANES), 1) // LANES
    e_i = lax.broadcasted_iota(jnp.int32, (LANES, LANES), 0)
    e_j = lax.broadcasted_iota(jnp.int32, (LANES, LANES), 1)
    eye_b = (e_i == e_j).astype(BF16)

    def cat(cols):
        out = jnp.zeros((c, nh * c), F32)
        for h in range(nh):
            out = jnp.where(blk == h, cols[h], out)
        return out

    def nat(cols):
        out = jnp.zeros((c, nh * LANES), F32)
        for h in range(nh):
            out = jnp.where(hl == h, cols[h], out)
        return out

    def bd_cat(p):
        return jnp.concatenate([jnp.where(blk == h, p, 0.0) for h in range(nh)], axis=0)

    def bd_nat(x):
        return jnp.concatenate([jnp.where(hl == h, x, 0.0) for h in range(nh)], axis=0)

    def setup(ch):
        rows = slice(ch * c, (ch + 1) * c)
        gb = gb_ref[0, rows, :]
        q = q_ref[0, rows, :]
        k = k_ref[0, rows, :]
        v = v_ref[0, rows, :]
        kdh, kdl = _split2(bd_nat(k))
        out = []
        for d in range(2):
            incl, strict, incl_t = tri[d]
            g_cols = [gb[:, d * nh + h:d * nh + h + 1] for h in range(nh)]
            b_cols = [gb[:, (2 + d) * nh + h:(2 + d) * nh + h + 1] for h in range(nh)]
            g_cat = cat(g_cols)
            g_row = jnp.sum(jnp.where(eye, g_cat, 0.0), axis=0, keepdims=True)
            gc_cols = [jnp.sum(jnp.where(incl & (blk == h), g_row, 0.0), axis=1, keepdims=True)
                       for h in range(nh)]
            grow = jnp.sum(jnp.where(incl_t, g_cat, 0.0), axis=0, keepdims=True)
            glast = [jnp.sum(g_cols[h], axis=0, keepdims=True) for h in range(nh)]
            decay = jnp.exp(jnp.where(incl, cat(gc_cols) - grow, NEG))
            b_nat = nat(b_cols)
            eg_nat = nat([jnp.exp(gc_cols[h]) for h in range(nh)])
            kb = k * b_nat
            kbh, kbl = _split2(kb)
            qg_ref[0, d, rows, :] = q * eg_nat
            kd = (k * nat([jnp.exp(glast[h] - gc_cols[h]) for h in range(nh)])).astype(BF16)
            egl_ref[0, d, ch] = nat([jnp.exp(glast[h]) for h in range(nh)])[0:1]
            out.append(dict(rows=rows, ch=ch, d=d, q=q, decay=decay, strict=strict, kbh=kbh, kbl=kbl,
                            kdh=kdh, kdl=kdl, kd=kd, vb=v * b_nat, kbeg=kb * eg_nat))
        return out

    st = [s for ch in range(nch) for s in setup(ch)]
    kk1 = [lax.dot_general(jnp.concatenate([s["kbh"], s["kbl"]], axis=0), s["kdh"], _NT,
                           preferred_element_type=F32) for s in st]
    kk2 = [lax.dot_general(s["kbh"], s["kdl"], _NT, preferred_element_type=F32) for s in st]
    a = [jnp.where(s["strict"], (x[:c] + x[c:] + y) * s["decay"], 0.0) for x, y, s in zip(kk1, kk2, st)]
    tinv = [eyef - x for x in a]
    p = [_dot3s(x, bd_cat(x)) for x in a]
    for _ in range(4):
        t = [_dot3s(jnp.concatenate([ti, pi], axis=0), bd_cat(pi)) for ti, pi in zip(tinv, p)]
        tinv = [ti + x[:c] for ti, x in zip(tinv, t)]
        p = [x[c:] for x in t]
    tinv = [ti + _dot3s(ti, bd_cat(pi)) for ti, pi in zip(tinv, p)]
    us = [_dot3s(ti, bd_nat(s["vb"])) for ti, s in zip(tinv, st)]
    ws = [_dot3s(ti, bd_nat(s["kbeg"])) for ti, s in zip(tinv, st)]
    qks = [lax.dot_general(s["q"].astype(BF16), s["kdh"], _NT, preferred_element_type=F32) for s in st]
    kdts = [jnp.concatenate([lax.dot_general(eye_b, s["kd"][:, h * LANES:(h + 1) * LANES], _NT,
                                             preferred_element_type=F32) for h in range(nh)], axis=0)
            for s in st]
    for s, u, w, qk, kdt in zip(st, us, ws, qks, kdts):
        u_ref[0, s["d"], s["rows"], :] = u
        w_ref[0, s["d"], s["rows"], :] = w
        qk_ref[0, s["d"], s["ch"]] = qk * s["decay"]
        kdt_ref[0, s["d"], s["ch"]] = kdt.astype(BF16)


def _gdn_scan_body(*refs):
    ins, o_refs, s_ref = (refs[:6], refs[6:12]), refs[12:14], refs[14]

    @pl.when(pl.program_id(1) == 0)
    def _():
        s_ref[...] = jnp.zeros_like(s_ref)

    c = GD_CHUNK
    heads = range(GD_HEADS)
    sls = [slice(h * LANES, (h + 1) * LANES) for h in heads]
    hl = lax.broadcasted_iota(jnp.int32, (c, GD_HEADS * LANES), 1) // LANES
    chains = [(d, h) for d in range(2) for h in heads]
    ns = ins[0][3].shape[2]
    ss = [s_ref[d, h] for d, h in chains]
    for step in range(ns):
        pos = [step, ns - 1 - step]
        rows = [slice(p * c, (p + 1) * c) for p in pos]
        rs = [_dot1(jnp.concatenate([ins[d][1][0, 0, rows[d], sls[h]], ins[d][2][0, 0, rows[d], sls[h]]],
                                    axis=0), s) for (d, h), s in zip(chains, ss)]
        v_news = [ins[d][0][0, 0, rows[d], sls[h]] - r[:c] for (d, h), r in zip(chains, rs)]
        upd = [_dot1(ins[d][4][0, 0, pos[d], sls[h], :], vn) for (d, h), vn in zip(chains, v_news)]
        for d in range(2):
            lo = d * GD_HEADS
            v_all = jnp.concatenate(v_news[lo:lo + GD_HEADS], axis=1)
            vbd = jnp.concatenate([jnp.where(hl == h, v_all, 0.0) for h in heads], axis=0)
            o_refs[d][0, rows[d], :] = (jnp.concatenate([r[c:] for r in rs[lo:lo + GD_HEADS]], axis=1)
                                        + _dot1(ins[d][3][0, 0, pos[d]], vbd))
        ss = [s * ins[d][5][0, 0, pos[d], :, sls[h]] + up for (d, h), s, up in zip(chains, ss, upd)]
    for (d, h), s in zip(chains, ss):
        s_ref[d, h] = s


def _gdn(q, k, v, gb, l_len):
    b, t, _ = q.shape
    c = GD_CHUNK
    nc, ncc = t // c, l_len // c
    nch = _pick(nc, (GD_LOCAL_CHUNKS, 1))
    h = GD_HEADS
    hw = h * LANES
    tokl = pl.BlockSpec((1, nch * c, hw), lambda bi, j: (bi, j, 0))
    tok4 = pl.BlockSpec((1, 2, nch * c, hw), lambda bi, j: (bi, 0, j, 0))
    ch5 = lambda r, w2: pl.BlockSpec((1, 2, nch, r, w2), lambda bi, j: (bi, 0, j, 0, 0))
    tok_shape = jax.ShapeDtypeStruct((b, 2, t, hw), F32)
    u, w, qg, qk, kdt, egl = pl.pallas_call(
        functools.partial(_gdn_local_body, nch=nch),
        grid=(b, nc // nch),
        in_specs=[tokl, tokl, tokl, pl.BlockSpec((1, nch * c, LANES), lambda bi, j: (bi, j, 0))],
        out_specs=[tok4, tok4, tok4, ch5(c, h * c), ch5(hw, c), ch5(1, hw)],
        out_shape=[tok_shape, tok_shape, tok_shape,
                   jax.ShapeDtypeStruct((b, 2, nc, c, h * c), F32),
                   jax.ShapeDtypeStruct((b, 2, nc, hw, c), BF16),
                   jax.ShapeDtypeStruct((b, 2, nc, 1, hw), F32)],
        compiler_params=_params(("parallel", "parallel")),
        name="gdn_local",
    )(q, k, v, gb)

    ns = _pick(math.gcd(ncc, nc), (4, 2, 1))

    def blk(d, j):
        first = j * ns
        last = jnp.where(first < ncc, ncc - 1 - first, nc - 1 - first + ncc) - (ns - 1)
        return j if d == 0 else last // ns

    def specs(d):
        stok = pl.BlockSpec((1, 1, ns * c, hw), lambda bi, j: (bi, d, blk(d, j), 0))
        sch = lambda r, w2: pl.BlockSpec((1, 1, ns, r, w2), lambda bi, j: (bi, d, blk(d, j), 0, 0))
        return [stok, stok, stok, sch(c, h * c), sch(hw, c), sch(1, hw)]

    out_spec = lambda d: pl.BlockSpec((1, ns * c, hw), lambda bi, j: (bi, blk(d, j), 0))
    out_shape = jax.ShapeDtypeStruct((b, t, hw), F32)
    return pl.pallas_call(
        _gdn_scan_body,
        grid=(b, nc // ns),
        in_specs=specs(0) + specs(1),
        out_specs=[out_spec(0), out_spec(1)],
        out_shape=[out_shape, out_shape],
        scratch_shapes=[pltpu.VMEM((2, h, GD_DK, GD_DV), F32)],
        compiler_params=_params(("parallel", "arbitrary")),
        name="gdn_scan",
    )(u, w, qg, qk, kdt, egl, u, w, qg, qk, kdt, egl)


def _merge_body(x_ref, oa_ref, ob_ref, og0_ref, og1_ref, z_ref, gl0_ref, gl1_ref, bg_ref, ng_ref,
                gt_ref, wb_ref, wo_ref, o_ref):
    d = x_ref.shape[1]
    parts = []
    for h in range(GD_HEADS):
        sl = slice(h * LANES, (h + 1) * LANES)
        o = og0_ref[:, sl] + og1_ref[:, sl]
        z = z_ref[:, sl]
        o = o * lax.rsqrt(jnp.mean(o * o, axis=-1, keepdims=True) + EPS) * ng_ref[...]
        parts.append(o * (z * jax.nn.sigmoid(z)))
    oc = jnp.concatenate(parts, axis=1)
    gl = jnp.concatenate([gl0_ref[...], gl1_ref[...]], axis=1)
    merged = None
    for r, o_r in enumerate((oa_ref[...], ob_ref[...], oc)):
        y = jnp.dot(o_r.astype(BF16), wb_ref[r], preferred_element_type=F32)
        t = jax.nn.sigmoid(gl[:, r * d:(r + 1) * d] + bg_ref[:, r * d:(r + 1) * d]) * y
        merged = t if merged is None else merged + t
    m = jnp.dot(merged.astype(BF16), wo_ref[...], preferred_element_type=F32)
    o_ref[...] = x_ref[...] + gt_ref[0] * m


def _merge(xs, oa, ob, og, proj, b_gate, norm_g, mod, wb, wo, l, bn, t_len, l_len):
    n, d = xs.shape
    tm = _pick(math.gcd(l_len, t_len), (256, 128))
    tpb, lt = t_len // tm, l_len // tm
    tok = lambda w: pl.BlockSpec((tm, w), lambda i: (i, 0))
    full = lambda a: pl.BlockSpec(a.shape, lambda i: (0,) * a.ndim)
    layer = lambda a: pl.BlockSpec((None,) + a.shape[1:], lambda i: (l,) + (0,) * (a.ndim - 1))
    return pl.pallas_call(
        _merge_body,
        grid=(n // tm,),
        in_specs=[tok(d), tok(BRANCH_W), tok(BRANCH_W), tok(BRANCH_W), tok(BRANCH_W),
                  pl.BlockSpec((tm, BRANCH_W), lambda i: (i, C_ZC // BRANCH_W)),
                  pl.BlockSpec((tm, CONV_BLK), lambda i: (i, C_GL // CONV_BLK)),
                  pl.BlockSpec((tm, CONV_BLK), lambda i: (i, C_GL // CONV_BLK + 1)),
                  full(b_gate), full(norm_g),
                  pl.BlockSpec((1, 1, d), lambda i: (_row_sel(i, tpb, lt, bn) * 6 + 2, 0, 0)),
                  layer(wb), layer(wo)],
        out_specs=tok(d),
        out_shape=jax.ShapeDtypeStruct((n, d), F32),
        compiler_params=_params(("parallel",)),
        name="merge",
    )(xs, oa, ob, og[0].reshape(n, BRANCH_W), og[1].reshape(n, BRANCH_W), proj, proj, proj,
      b_gate, norm_g, mod, wb, wo)


_CAND = [(r1, r2) for r1 in range(1, PEER_TOPK + 1) for r2 in range(1, PEER_TOPK + 1)
         if r1 * r2 <= PEER_TOPK]


def _extract_top(cur, n, want_rank=False):
    rank = jnp.full(cur.shape, 99.0, F32) if want_rank else None
    vals = []
    for r in range(n):
        m = jnp.max(cur, axis=0, keepdims=True)
        hit = cur == m
        if want_rank:
            rank = jnp.where(hit, float(r + 1), rank)
        cur = jnp.where(hit, -jnp.inf, cur)
        vals.append(m)
    return vals, rank


def _topk_body(x_ref, g_ref, sh_ref, sc_ref, wq_ref, keys_ref, h_ref, cnt1_ref, e1_ref, rk2_ref, e2_ref):
    hm = _modulate(x_ref[...], g_ref[...], sh_ref[0], sc_ref[0])
    h_ref[...] = (hm * math.sqrt(0.5)).astype(BF16)
    hmod = hm.astype(BF16)
    nr = PEER_TOPK
    for h in range(PEER_HEADS):
        st = []
        for c in range(2):
            col = (2 * h + c) * LANES
            qs = jnp.dot(hmod, wq_ref[:, col:col + LANES], preferred_element_type=F32)
            st.append(_dot3(keys_ref[c], qs, _NT))
        a, _ = _extract_top(st[0], nr)
        b, rank2 = _extract_top(st[1], nr, want_rank=True)
        rows = [a[r1 - 1] + b[r2 - 1] for r1, r2 in _CAND]
        pad = (-len(rows)) % SUBLANES
        cand = jnp.concatenate(rows + [jnp.full_like(rows[0], -jnp.inf)] * pad, axis=0)
        top, _ = _extract_top(cand, nr)
        tau = top[PEER_TOPK - 1]
        z = jnp.sum(jnp.where(cand >= tau, jnp.exp(cand - top[0]), 0.0), axis=0, keepdims=True)
        cnt1 = jnp.zeros_like(st[0])
        for r1 in range(1, PEER_TOPK + 1):
            cnt = None
            for r2 in range(1, PEER_TOPK + 1):
                if r1 * r2 <= PEER_TOPK:
                    t = (a[r1 - 1] + b[r2 - 1] >= tau).astype(F32)
                    cnt = t if cnt is None else cnt + t
            cnt1 = jnp.where(st[0] == a[r1 - 1], cnt, cnt1)
        cnt1_ref[h] = cnt1
        e1_ref[h] = jnp.exp(st[0] - a[0])
        rk2_ref[h] = rank2.astype(BF16)
        e2_ref[h] = (jnp.exp(st[1] - b[0]) * (1.0 / z)).astype(BF16)


def _peer_topk(xs, g, mod, wq, keys, l, bn, t_len, l_len):
    n, d = xs.shape
    tm = _pick(math.gcd(l_len, t_len), (256, 128))
    tpb, lt = t_len // tm, l_len // tm
    sel = lambda i: _row_sel(i, tpb, lt, bn)
    out = pl.BlockSpec((PEER_HEADS, N_KEYS, tm), lambda i: (0, 0, i))
    shp = lambda dt: jax.ShapeDtypeStruct((PEER_HEADS, N_KEYS, n), dt)
    return pl.pallas_call(
        _topk_body,
        grid=(n // tm,),
        in_specs=[pl.BlockSpec((tm, d), lambda i: (i, 0)),
                  pl.BlockSpec((1, d), lambda i: (0, 0)),
                  pl.BlockSpec((1, 1, d), lambda i: (sel(i) * 6 + 3, 0, 0)),
                  pl.BlockSpec((1, 1, d), lambda i: (sel(i) * 6 + 4, 0, 0)),
                  pl.BlockSpec((None,) + wq.shape[1:], lambda i: (l, 0, 0)),
                  pl.BlockSpec((None,) + keys.shape[1:], lambda i: (l, 0, 0, 0))],
        out_specs=[pl.BlockSpec((tm, d), lambda i: (i, 0)), out, out, out, out],
        out_shape=[jax.ShapeDtypeStruct((n, d), BF16), shp(F32), shp(F32), shp(BF16), shp(BF16)],
        compiler_params=_params(("parallel",)),
        name="peer_topk",
    )(xs, g, mod, mod, wq, keys)


DENSE_CHUNK = 256
DENSE_KDEPTH = 1024
DENSE_AHEAD = 3


def _dense_body(x_ref, h_ref, u_ref, vt_ref, cnt1_ref, e1_ref, rk2_ref, e2_ref, *rest, te):
    gt_refs, o_ref, acc_ref = rest[:-2], rest[-2], rest[-1]
    e = pl.program_id(1)

    @pl.when(e == 0)
    def _():
        acc_ref[...] = jnp.zeros_like(acc_ref)

    hs = h_ref[...]
    zero = jnp.zeros((N_KEYS, hs.shape[0]), BF16)
    ck = DENSE_CHUNK
    nck = te // ck

    def mm1(c):
        return lax.dot_general(u_ref[c * ck:(c + 1) * ck, :], hs, _NT, preferred_element_type=F32)

    def gate(c):
        parts = []
        for j in range(c * ck // N_KEYS, (c + 1) * ck // N_KEYS):
            i1 = e * (te // N_KEYS) + j
            g = zero
            for h in range(PEER_HEADS):
                cnt = cnt1_ref[h, pl.ds(i1, 1), :].astype(BF16)
                w1 = e1_ref[h, pl.ds(i1, 1), :].astype(BF16)
                g = g + jnp.where(rk2_ref[h] <= cnt, e2_ref[h], zero) * w1
            parts.append(g)
        return jnp.concatenate(parts, axis=0)

    def mm2(c0, c1):
        hid = jnp.concatenate(hids[c0:c1], axis=0)
        acc_ref[...] += jnp.dot(vt_ref[:, c0 * ck:c1 * ck], hid, preferred_element_type=F32)

    grp = max(1, DENSE_KDEPTH // ck)
    acts = [mm1(c) for c in range(min(DENSE_AHEAD, nck))]
    hids = []
    pending = None
    for c in range(nck):
        if c + DENSE_AHEAD < nck:
            acts.append(mm1(c + DENSE_AHEAD))
        y = acts[c]
        hids.append((y * (1.0 + lax.erf(y))).astype(BF16) * gate(c))
        if pending is not None:
            mm2(*pending)
            pending = None
        if c % grp == grp - 1:
            pending = (c + 1 - grp, c + 1)
    mm2(*pending)

    @pl.when(e == pl.num_programs(1) - 1)
    def _():
        base = acc_ref.shape[1] // len(gt_refs)
        for k, gt_ref in enumerate(gt_refs):
            rs = slice(k * base, (k + 1) * base)
            o_ref[rs] = x_ref[rs] + (gt_ref[0] * math.sqrt(0.5)) * acc_ref[:, rs].T


def _peer_dense(xs, hmod, u, vt, cnt1, e1, rk2, e2, mod, l, bn, t_len, l_len):
    n, d = xs.shape
    ne = u.shape[1]
    base = _pick(math.gcd(l_len, t_len), (256, 128))
    nsub = 2 if n % (2 * base) == 0 else 1
    tm = nsub * base
    te = _pick(ne, (2048, 1024, 512))
    tpb, lt = t_len // base, l_len // base
    gts = [pl.BlockSpec((1, 1, d), lambda i, e, k=k: (_row_sel(i * nsub + k, tpb, lt, bn) * 6 + 5, 0, 0))
           for k in range(nsub)]
    sel = pl.BlockSpec((PEER_HEADS, N_KEYS, tm), lambda i, e: (0, 0, i))
    tok = pl.BlockSpec((tm, d), lambda i, e: (i, 0))
    return pl.pallas_call(
        functools.partial(_dense_body, te=te),
        grid=(n // tm, ne // te),
        in_specs=[tok, tok,
                  pl.BlockSpec((None, te, d), lambda i, e: (l, e, 0)),
                  pl.BlockSpec((None, d, te), lambda i, e: (l, 0, e)),
                  sel, sel, sel, sel] + gts,
        out_specs=tok,
        out_shape=jax.ShapeDtypeStruct((n, d), F32),
        scratch_shapes=[pltpu.VMEM((d, tm), F32)],
        compiler_params=_params(("parallel", "arbitrary")),
        name="peer_dense",
    )(xs, hmod, u, vt, cnt1, e1, rk2, e2, *([mod] * nsub))


def _rope_tables(rows, l_len):
    row = jnp.repeat(jnp.arange(rows), GRID_W).astype(F32)
    col = jnp.tile(jnp.arange(GRID_W), rows).astype(F32)
    inv = jnp.power(ROPE_THETA, -jnp.arange(ROPE_PAIRS, dtype=F32) / ROPE_PAIRS)
    ar = row[:, None] * inv
    ac = col[:, None] * inv
    ang = jnp.concatenate([ar, ar, ac, ac], axis=-1)
    cos = jnp.concatenate([jnp.ones((l_len, HEAD_DIM), F32), jnp.cos(ang)], axis=0)
    sin = jnp.concatenate([jnp.zeros((l_len, HEAD_DIM), F32), jnp.sin(ang)], axis=0)
    cos = jnp.tile(cos, (1, 2))
    sin = jnp.tile(sin, (1, 2))
    first = (jnp.arange(LANES) % (2 * ROPE_PAIRS)) < ROPE_PAIRS
    return cos, jnp.where(first, -sin, 0.0), jnp.where(first, 0.0, sin)


def _reorder_w_in(w):
    nl, d = w.shape[:2]
    qa_ka_va, qb, kb_vb = w[..., :1536], w[..., 1536:2048], w[..., 2048:2304]
    qkvc, zc, ab, gl = w[..., 2304:3840], w[..., 3840:4352], w[..., 4352:4368], w[..., 4368:]
    qb = qb.reshape(nl, d, WG_KV_HEADS, WG_GROUP, HEAD_DIM).transpose(0, 1, 3, 2, 4).reshape(nl, d, 512)
    pad = jnp.zeros((nl, d, C_GL - C_AB - ab.shape[-1]), w.dtype)
    return jnp.concatenate([qa_ka_va, qkvc, zc, qb, kb_vb, ab, pad, gl], axis=-1).astype(BF16)


def _reorder_w_branch(wb):
    nl, _, _, d = wb.shape
    wb1 = wb[:, 1].reshape(nl, WG_KV_HEADS, WG_GROUP, HEAD_DIM, d).transpose(0, 2, 1, 3, 4)
    return jnp.stack([wb[:, 0], wb1.reshape(nl, BRANCH_W, d), wb[:, 2]], axis=1).astype(BF16)


def kernel(x, c, ctx, c_ctx, norm1_g, norm2_g, w_ada, b_ada, w_in, b_gate, qk_norm_g, diff_lam,
           diff_subln_g, wg_sink, gd_conv_w, gd_a_log, gd_dt_bias, gd_norm_g, w_branch, w_out,
           peer_wq, peer_keys, peer_u, peer_v):
    bn, s_len, d = x.shape
    l_len = ctx.shape[1]
    t_len = l_len + s_len
    n_tok = bn * t_len
    depth = w_in.shape[0]
    cos, sa, sb = _rope_tables(s_len // GRID_W, l_len)

    xs = jnp.concatenate([ctx, x], axis=1).reshape(n_tok, d)
    cvec = jnp.concatenate([c, c_ctx[None]], axis=0)
    cact = jnp.pad(jax.nn.silu(cvec), ((0, 16 - (bn + 1) % 16), (0, 0))).astype(BF16)
    lane_pad = lambda a: jnp.pad(a.reshape(1, -1), ((0, 0), (0, LANES - a.size)))

    w_ada_b = w_ada.astype(BF16)
    w_in_b = _reorder_w_in(w_in)
    w_branch_b = _reorder_w_branch(w_branch)
    w_out_b = w_out.astype(BF16)
    wq_b = peer_wq.astype(BF16)
    u_b = peer_u.astype(BF16)
    vt_b = jnp.swapaxes(peer_v.astype(BF16), 1, 2)

    for l in range(depth):
        lam_init = 0.8 - 0.6 * math.exp(-0.3 * l)
        mod = (_mm(cact, w_ada_b, l)[:bn + 1] + b_ada[l]).reshape((bn + 1) * 6, 1, d)

        proj = _inproj(xs, norm1_g[l].reshape(1, d), mod, w_in_b, l, bn, t_len, l_len)
        qa, ka, va, qb, kb, vb, gq, gk, gv, gb = _prep(
            proj, cos, sa, sb, jnp.tile(qk_norm_g[l], (1, 2)), gd_conv_w[l],
            lane_pad(gd_a_log[l]), lane_pad(gd_dt_bias[l]), t_len, l_len)
        b3 = lambda a: a.reshape(bn, t_len, a.shape[-1])

        lv = diff_lam[l]
        lam = (jnp.exp(jnp.sum(lv[0, 0] * lv[0, 1])) - jnp.exp(jnp.sum(lv[1, 0] * lv[1, 1])) + lam_init)
        oa = _diff_attn(b3(qa), b3(ka), b3(va), lam.reshape(1), diff_subln_g[l].reshape(1, DA_VDIM),
                        1.0 - lam_init, l_len)
        ob = _win_attn(b3(qb), b3(kb), b3(vb), wg_sink[l], l_len)
        og = _gdn(b3(gq), b3(gk), b3(gv), b3(gb), l_len)

        xs = _merge(xs, oa.reshape(n_tok, BRANCH_W), ob.reshape(n_tok, BRANCH_W), og, proj,
                    b_gate[l].reshape(1, -1), gd_norm_g[l].reshape(1, GD_DV), mod, w_branch_b, w_out_b,
                    l, bn, t_len, l_len)

        hmod, cnt1, e1, rk2, e2 = _peer_topk(xs, norm2_g[l].reshape(1, d), mod, wq_b, peer_keys,
                                             l, bn, t_len, l_len)
        xs = _peer_dense(xs, hmod, u_b, vt_b, cnt1, e1, rk2, e2, mod, l, bn, t_len, l_len)

    return xs.reshape(bn, t_len, d)[:, l_len:]
```

```python
import functools
import math

import jax
import jax.numpy as jnp
from jax import lax
from jax.experimental import pallas as pl
from jax.experimental.pallas import tpu as pltpu

F32 = jnp.float32
BF16 = jnp.bfloat16

EPS = 1e-6
GRID_W = 64
HEAD_DIM = 64
ROPE_THETA = 10000.0
ROPE_PAIRS = HEAD_DIM // 4
DA_HEADS = 4
DA_VDIM = 2 * HEAD_DIM
WG_HEADS = 8
WG_KV_HEADS = 2
WG_GROUP = WG_HEADS // WG_KV_HEADS
WINDOW = 128
WG_BLOCK = 128
GD_HEADS = 4
GD_DK = 128
GD_DV = 128
GD_CONV = 5
GD_CHUNK = 64
GD_QKV_W = GD_HEADS * (2 * GD_DK + GD_DV)
N_BRANCH = 3
BRANCH_W = 512
PEER_HEADS = 8
PEER_HALF = 128
N_KEYS = 128
PEER_TOPK = 16

LANES = 128
SUBLANES = 8
VMEM_LIMIT = 56 * 1024 * 1024
NEG = -1e30
LOG2E = math.log2(math.e)

C_QA, C_KA, C_VA, C_QKVC, C_ZC, C_QB, C_KB, C_VB, C_AB, C_GL = (
    0, 512, 1024, 1536, 3072, 3584, 4096, 4224, 4352, 4608)
PROJ_W = C_GL + N_BRANCH * 1024
PREP_W = C_GL
CONV_BLK = 1536


def _pick(n, cands):
    for c in cands:
        if n % c == 0:
            return c
    raise ValueError(f"no tile in {cands} divides {n}")


def _params(sem):
    return pltpu.CompilerParams(dimension_semantics=sem, vmem_limit_bytes=VMEM_LIMIT)


def _split2(a):
    hi = a.astype(BF16)
    return hi, (a - hi.astype(F32)).astype(BF16)


def _dot1(a, b, dn=None):
    a = a.astype(BF16)
    b = b.astype(BF16)
    if dn is None:
        return jnp.dot(a, b, preferred_element_type=F32)
    return lax.dot_general(a, b, dn, preferred_element_type=F32)


def _dot3(a, b, dn=None):
    ah, al = _split2(a)
    bh, bl = _split2(b)
    return _dot1(ah, bh, dn) + (_dot1(ah, bl, dn) + _dot1(al, bh, dn))


_NT = (((1,), (1,)), ((), ()))


def _row_sel(i, tpb, lt, bn):
    return jnp.where(i % tpb < lt, bn, i // tpb)


def _mm_body(x_ref, w_ref, o_ref):
    o_ref[...] = jnp.dot(x_ref[...], w_ref[...], preferred_element_type=F32)


def _mm(x, w, l):
    m, k = x.shape
    n = w.shape[2]
    tn = _pick(n, (1536, 1024, 512, 256, 128))
    return pl.pallas_call(
        _mm_body,
        grid=(n // tn,),
        in_specs=[pl.BlockSpec((m, k), lambda j: (0, 0)),
                  pl.BlockSpec((None, k, tn), lambda j: (l, 0, j))],
        out_specs=pl.BlockSpec((m, tn), lambda j: (0, j)),
        out_shape=jax.ShapeDtypeStruct((m, n), F32),
        compiler_params=_params(("parallel",)),
        name="ada_mm",
    )(x, w)


def _modulate(x, g, sh, sc):
    h = x * lax.rsqrt(jnp.mean(x * x, axis=-1, keepdims=True) + EPS) * g
    return h * (1.0 + sc) + sh


def _inproj_body(x_ref, g_ref, sh_ref, sc_ref, w_ref, o_ref):
    h = _modulate(x_ref[...], g_ref[...], sh_ref[0], sc_ref[0])
    o_ref[...] = jnp.dot(h.astype(BF16), w_ref[...], preferred_element_type=F32)


def _inproj(xs, g, mod, w, l, bn, t_len, l_len):
    n, d = xs.shape
    tm = _pick(math.gcd(l_len, t_len), (512, 256, 128))
    nw = w.shape[2]
    tn = nw // 2
    tpb, lt = t_len // tm, l_len // tm
    sel = lambda j, i: _row_sel(i, tpb, lt, bn)
    return pl.pallas_call(
        _inproj_body,
        grid=(nw // tn, n // tm),
        in_specs=[pl.BlockSpec((tm, d), lambda j, i: (i, 0)),
                  pl.BlockSpec((1, d), lambda j, i: (0, 0)),
                  pl.BlockSpec((1, 1, d), lambda j, i: (sel(j, i) * 6, 0, 0)),
                  pl.BlockSpec((1, 1, d), lambda j, i: (sel(j, i) * 6 + 1, 0, 0)),
                  pl.BlockSpec((None, d, tn), lambda j, i: (l, 0, j))],
        out_specs=pl.BlockSpec((tm, tn), lambda j, i: (i, j)),
        out_shape=jax.ShapeDtypeStruct((n, nw), F32),
        compiler_params=_params(("parallel", "parallel")),
        name="inproj",
    )(xs, g, mod, mod, w)


def _prep_body(p_ref, prev_ref, next_ref, cos_ref, sa_ref, sb_ref, qkg_ref, cw_ref, alog_ref, dtb_ref,
               qa_ref, ka_ref, va_ref, qb_ref, kb_ref, vb_ref, gq_ref, gk_ref, gv_ref, gb_ref,
               xe_ref, *, tpb, lt):
    tm = p_ref.shape[0]
    li = lax.broadcasted_iota(jnp.int32, (LANES, LANES), 0)
    lj = lax.broadcasted_iota(jnp.int32, (LANES, LANES), 1)
    seg = jnp.where(li // HEAD_DIM == lj // HEAD_DIM, 1.0 / HEAD_DIM, 0.0).astype(BF16)
    cos = cos_ref[...]
    sa = sa_ref[...]
    sb = sb_ref[...]

    def normrope(x, gain):
        yh, yl = _split2(x * x)
        ms = jnp.dot(yh, seg, preferred_element_type=F32) + jnp.dot(yl, seg, preferred_element_type=F32)
        xn = x * lax.rsqrt(ms + EPS) * gain
        return xn * cos + pltpu.roll(xn, LANES - ROPE_PAIRS, 1) * sa + pltpu.roll(xn, ROPE_PAIRS, 1) * sb

    scale = HEAD_DIM ** -0.5 * LOG2E
    for c in range(4):
        cs = slice(c * LANES, (c + 1) * LANES)
        qa_ref[:, cs] = (normrope(p_ref[:, C_QA + c * LANES:C_QA + (c + 1) * LANES], qkg_ref[0:1]) * scale).astype(BF16)
        ka_ref[:, cs] = normrope(p_ref[:, C_KA + c * LANES:C_KA + (c + 1) * LANES], qkg_ref[1:2]).astype(BF16)
        qb_ref[:, cs] = (normrope(p_ref[:, C_QB + c * LANES:C_QB + (c + 1) * LANES], qkg_ref[2:3]) * scale).astype(BF16)
    kb_ref[...] = normrope(p_ref[:, C_KB:C_KB + LANES], qkg_ref[3:4]).astype(BF16)
    va_ref[...] = p_ref[:, C_VA:C_VA + 512].astype(BF16)
    vb_ref[...] = p_ref[:, C_VB:C_VB + LANES].astype(BF16)

    ti = pl.program_id(0) % tpb
    at_start = (ti == 0) | (ti == lt)
    at_end = (ti == lt - 1) | (ti == tpb - 1)
    xe_ref[0:SUBLANES] = jnp.where(at_start, 0.0, prev_ref[...])
    xe_ref[SUBLANES:SUBLANES + tm] = p_ref[:, C_QKVC:C_QKVC + GD_QKV_W]
    xe_ref[SUBLANES + tm:2 * SUBLANES + tm] = jnp.where(at_end, 0.0, next_ref[...])
    half = GD_CONV // 2
    y = None
    for i in range(GD_CONV):
        t = xe_ref[pl.ds(SUBLANES - half + i, tm), :] * cw_ref[i:i + 1]
        y = t if y is None else y + t
    y = y * jax.nn.sigmoid(y)
    for h in range(GD_HEADS):
        hs = slice(h * LANES, (h + 1) * LANES)
        q = y[:, h * LANES:(h + 1) * LANES]
        k = y[:, 512 + h * LANES:512 + (h + 1) * LANES]
        gq_ref[:, hs] = q * lax.rsqrt(jnp.sum(q * q, axis=-1, keepdims=True) + EPS) * (GD_DK ** -0.5)
        gk_ref[:, hs] = k * lax.rsqrt(jnp.sum(k * k, axis=-1, keepdims=True) + EPS)
    gv_ref[...] = y[:, 1024:]

    ab = p_ref[:, C_AB:C_AB + LANES]
    lane = lax.broadcasted_iota(jnp.int32, ab.shape, 1)
    gdec = -jnp.exp(alog_ref[...]) * jnp.logaddexp(ab + dtb_ref[...], 0.0)
    gb_ref[...] = jnp.where(lane < 2 * GD_HEADS, gdec, jax.nn.sigmoid(ab))


def _prep(proj, cos, sa, sb, qkg, cw, alog, dtb, t_len, l_len):
    n = proj.shape[0]
    tm = _pick(math.gcd(l_len, t_len), (256, 128))
    tpb, lt = t_len // tm, l_len // tm
    r8 = tm // SUBLANES
    nb8 = n // SUBLANES
    tok = lambda w: pl.BlockSpec((tm, w), lambda i: (i, 0))
    rope = pl.BlockSpec((tm, LANES), lambda i: (i % tpb, 0))
    full = lambda a: pl.BlockSpec(a.shape, lambda i: (0,) * a.ndim)
    shp = lambda w, dt: jax.ShapeDtypeStruct((n, w), dt)
    return pl.pallas_call(
        functools.partial(_prep_body, tpb=tpb, lt=lt),
        grid=(n // tm,),
        in_specs=[pl.BlockSpec((tm, PREP_W), lambda i: (i, 0)),
                  pl.BlockSpec((SUBLANES, CONV_BLK), lambda i: (jnp.maximum(i * r8 - 1, 0), 1)),
                  pl.BlockSpec((SUBLANES, CONV_BLK), lambda i: (jnp.minimum((i + 1) * r8, nb8 - 1), 1)),
                  rope, rope, rope, full(qkg), full(cw), full(alog), full(dtb)],
        out_specs=[tok(512), tok(512), tok(512), tok(512), tok(LANES), tok(LANES),
                   tok(512), tok(512), tok(512), tok(LANES)],
        out_shape=[shp(512, BF16), shp(512, BF16), shp(512, BF16), shp(512, BF16), shp(LANES, BF16),
                   shp(LANES, BF16), shp(512, F32), shp(512, F32), shp(512, F32), shp(LANES, F32)],
        scratch_shapes=[pltpu.VMEM((tm + 2 * SUBLANES, GD_QKV_W), F32)],
        compiler_params=_params(("parallel",)),
        name="prep",
    )(proj, proj, proj, cos, sa, sb, qkg, cw, alog, dtb)


DA_PAIR = 2


def _diff_body(lam_ref, q_ref, k_ref, v_ref, g_ref, o_ref, *, post, l_len, n_ctx):
    heads = [slice(h * LANES, (h + 1) * LANES) for h in range(DA_PAIR)]
    lane = lax.broadcasted_iota(jnp.int32, (q_ref.shape[1], LANES), 1)
    halves = [lane < HEAD_DIM, lane >= HEAD_DIM]

    def run(nk):
        def attend(s, v):
            p = jnp.exp2(s - jnp.max(s, axis=-1, keepdims=True))
            l = jnp.sum(p, axis=-1, keepdims=True)
            return jnp.dot(p.astype(BF16), v, preferred_element_type=F32) * (1.0 / l)

        ss = []
        for hs in heads:
            q = q_ref[0, :, hs]
            k = k_ref[0, :nk, hs]
            ss.append([lax.dot_general(jnp.where(m, q, jnp.zeros_like(q)), k, _NT,
                                       preferred_element_type=F32) for m in halves])
        for hs, (s1, s2) in zip(heads, ss):
            v = v_ref[0, :nk, hs]
            o = attend(s1, v) - lam_ref[0] * attend(s2, v)
            ms = jnp.mean(o * o, axis=-1, keepdims=True)
            o_ref[0, :, hs] = (o * lax.rsqrt(ms + EPS) * (g_ref[...] * post)).astype(o_ref.dtype)

    i = pl.program_id(2)

    @pl.when(i < n_ctx)
    def _():
        run(l_len)

    @pl.when(i >= n_ctx)
    def _():
        run(k_ref.shape[1])


def _diff_attn(q, k, v, lam, subln_g, post, l_len):
    b, t, _ = q.shape
    tq = _pick(math.gcd(l_len, t), (256, 128))
    pw = DA_PAIR * LANES
    return pl.pallas_call(
        functools.partial(_diff_body, post=post, l_len=l_len, n_ctx=l_len // tq),
        grid=(b, DA_HEADS // DA_PAIR, t // tq),
        in_specs=[pl.BlockSpec(memory_space=pltpu.SMEM),
                  pl.BlockSpec((1, tq, pw), lambda bi, h, i: (bi, i, h)),
                  pl.BlockSpec((1, t, pw), lambda bi, h, i: (bi, 0, h)),
                  pl.BlockSpec((1, t, pw), lambda bi, h, i: (bi, 0, h)),
                  pl.BlockSpec((1, LANES), lambda bi, h, i: (0, 0))],
        out_specs=pl.BlockSpec((1, tq, pw), lambda bi, h, i: (bi, i, h)),
        out_shape=jax.ShapeDtypeStruct((b, t, DA_HEADS * DA_VDIM), BF16),
        compiler_params=_params(("parallel", "parallel", "parallel")),
        name="diff_attn",
    )(lam, q, k, v, subln_g)


def _win_body(sink_ref, q_ref, kc_ref, vc_ref, kp_ref, kn_ref, kx_ref, vp_ref, vn_ref, vx_ref, o_ref,
              *, lo, s_len):
    l_len = kc_ref.shape[1]
    q = q_ref[0]
    lane = lax.broadcasted_iota(jnp.int32, (WG_BLOCK, LANES), 1)

    def run(kcat, vcat, valid):
        vlane = lax.broadcasted_iota(jnp.int32, vcat.shape, 1)
        vz = jnp.zeros_like(vcat)
        vhalf = [jnp.where(vlane < HEAD_DIM, vcat, vz), jnp.where(vlane >= HEAD_DIM, vcat, vz)]
        heads = [(g, kv) for g in range(WG_GROUP) for kv in range(WG_KV_HEADS)]
        zq = jnp.zeros((WG_BLOCK, LANES), q.dtype)
        half = [lane < HEAD_DIM, lane >= HEAD_DIM]
        qms = [jnp.where(half[kv], q[:, g * LANES:(g + 1) * LANES], zq) for g, kv in heads]
        ss = [lax.dot_general(qm, kcat, _NT, preferred_element_type=F32) for qm in qms]
        if valid is not None:
            ss = [jnp.where(valid, s, NEG) for s in ss]
        sks = [sink_ref[kv * WG_GROUP + g] * LOG2E for g, kv in heads]
        ms = [jnp.maximum(jnp.max(s, axis=-1, keepdims=True), sk) for s, sk in zip(ss, sks)]
        ps = [jnp.exp2(s - m) for s, m in zip(ss, ms)]
        dens = [jnp.sum(p, axis=-1, keepdims=True) + jnp.exp2(sk - m) for p, sk, m in zip(ps, sks, ms)]
        os = [jnp.dot(p.astype(BF16), vhalf[kv], preferred_element_type=F32) * (1.0 / den)
              for p, den, (g, kv) in zip(ps, dens, heads)]
        for g in range(WG_GROUP):
            o_ref[0, :, g * LANES:(g + 1) * LANES] = (os[2 * g] + os[2 * g + 1]).astype(o_ref.dtype)

    n = pl.program_id(1)

    @pl.when(n < lo)
    def _():
        run(kc_ref[0], vc_ref[0], None)

    @pl.when(n >= lo)
    def _():
        kcat = jnp.concatenate([kc_ref[0], kp_ref[0], kn_ref[0], kx_ref[0]], axis=0)
        vcat = jnp.concatenate([vc_ref[0], vp_ref[0], vn_ref[0], vx_ref[0]], axis=0)
        nk = kcat.shape[0]
        col = lax.broadcasted_iota(jnp.int32, (WG_BLOCK, nk), 1)
        row = lax.broadcasted_iota(jnp.int32, (WG_BLOCK, nk), 0)
        j = col - l_len
        rel = j - WG_BLOCK - row
        kpos = (n - lo) * WG_BLOCK + j - WG_BLOCK
        valid = (col < l_len) | ((jnp.abs(rel) <= WINDOW) & (kpos >= 0) & (kpos < s_len))
        run(kcat, vcat, valid)


def _win_attn(q, k, v, sink, l_len):
    b, t, _ = q.shape
    blk = WG_BLOCK
    lo, nb = l_len // blk, t // blk
    ctx_spec = pl.BlockSpec((1, l_len, LANES), lambda bi, n: (bi, 0, 0))
    win = [pl.BlockSpec((1, blk, LANES), lambda bi, n, d=d: (bi, jnp.clip(n + d, lo, nb - 1), 0))
           for d in (-1, 0, 1)]
    return pl.pallas_call(
        functools.partial(_win_body, lo=lo, s_len=t - l_len),
        grid=(b, nb),
        in_specs=[pl.BlockSpec(memory_space=pltpu.SMEM),
                  pl.BlockSpec((1, blk, 4 * LANES), lambda bi, n: (bi, n, 0)),
                  ctx_spec, ctx_spec] + win + win,
        out_specs=pl.BlockSpec((1, blk, 4 * LANES), lambda bi, n: (bi, n, 0)),
        out_shape=jax.ShapeDtypeStruct((b, t, 4 * LANES), BF16),
        compiler_params=_params(("parallel", "parallel")),
        name="win_attn",
    )(sink, q, k, v, k, k, k, v, v, v)


GD_LOCAL_CHUNKS = 4


def _dot3s(x, r):
    m = x.shape[0]
    xh, xl = _split2(x)
    rh, rl = _split2(r)
    t = jnp.dot(jnp.concatenate([xh, xl], axis=0), rh, preferred_element_type=F32)
    return t[:m] + t[m:] + jnp.dot(xh, rl, preferred_element_type=F32)


def _gdn_local_body(q_ref, k_ref, v_ref, gb_ref, u_ref, w_ref, qg_ref, qk_ref, kdt_ref, egl_ref, *, nch):
    c = GD_CHUNK
    nh = GD_HEADS
    ii = lax.broadcasted_iota(jnp.int32, (c, nh * c), 0)
    jl = lax.broadcasted_iota(jnp.int32, (c, nh * c), 1)
    jj = jl % c
    blk = jl // c
    eye = ii == jj
    eyef = eye.astype(F32)
    tri = [(jj <= ii, jj < ii, jj >= ii), (jj >= ii, jj > ii, jj <= ii)]
    hl = lax.broadcasted_iota(jnp.int32, (c, nh * LANES), 1) // LANES
    e_i = lax.broadcasted_iota(jnp.int32, (LANES, LANES), 0)
    e_j = lax.broadcasted_iota(jnp.int32, (LANES, LANES), 1)
    eye_b = (e_i == e_j).astype(BF16)

    def cat(cols):
        out = jnp.zeros((c, nh * c), F32)
        for h in range(nh):
            out = jnp.where(blk == h, cols[h], out)
        return out

    def nat(cols):
        out = jnp.zeros((c, nh * LANES), F32)
        for h in range(nh):
            out = jnp.where(hl == h, cols[h], out)
        return out

    def bd_cat(p):
        return jnp.concatenate([jnp.where(blk == h, p, 0.0) for h in range(nh)], axis=0)

    def bd_nat(x):
        return jnp.concatenate([jnp.where(hl == h, x, 0.0) for h in range(nh)], axis=0)

    def setup(ch):
        rows = slice(ch * c, (ch + 1) * c)
        gb = gb_ref[0, rows, :]
        q = q_ref[0, rows, :]
        k = k_ref[0, rows, :]
        v = v_ref[0, rows, :]
        kdh, kdl = _split2(bd_nat(k))
        out = []
        for d in range(2):
            incl, strict, incl_t = tri[d]
            g_cols = [gb[:, d * nh + h:d * nh + h + 1] for h in range(nh)]
            b_cols = [gb[:, (2 + d) * nh + h:(2 + d) * nh + h + 1] for h in range(nh)]
            g_cat = cat(g_cols)
            g_row = jnp.sum(jnp.where(eye, g_cat, 0.0), axis=0, keepdims=True)
            gc_cols = [jnp.sum(jnp.where(incl & (blk == h), g_row, 0.0), axis=1, keepdims=True)
                       for h in range(nh)]
            grow = jnp.sum(jnp.where(incl_t, g_cat, 0.0), axis=0, keepdims=True)
            glast = [jnp.sum(g_cols[h], axis=0, keepdims=True) for h in range(nh)]
            decay = jnp.exp(jnp.where(incl, cat(gc_cols) - grow, NEG))
            b_nat = nat(b_cols)
            eg_nat = nat([jnp.exp(gc_cols[h]) for h in range(nh)])
            kb = k * b_nat
            kbh, kbl = _split2(kb)
            qg_ref[0, d, rows, :] = q * eg_nat
            kd = (k * nat([jnp.exp(glast[h] - gc_cols[h]) for h in range(nh)])).astype(BF16)
            egl_ref[0, d, ch] = nat([jnp.exp(glast[h]) for h in range(nh)])[0:1]
            out.append(dict(rows=rows, ch=ch, d=d, q=q, decay=decay, strict=strict, kbh=kbh, kbl=kbl,
                            kdh=kdh, kdl=kdl, kd=kd, vb=v * b_nat, kbeg=kb * eg_nat))
        return out

    st = [s for ch in range(nch) for s in setup(ch)]
    kk1 = [lax.dot_general(jnp.concatenate([s["kbh"], s["kbl"]], axis=0), s["kdh"], _NT,
                           preferred_element_type=F32) for s in st]
    kk2 = [lax.dot_general(s["kbh"], s["kdl"], _NT, preferred_element_type=F32) for s in st]
    a = [jnp.where(s["strict"], (x[:c] + x[c:] + y) * s["decay"], 0.0) for x, y, s in zip(kk1, kk2, st)]
    tinv = [eyef - x for x in a]
    p = [_dot3s(x, bd_cat(x)) for x in a]
    for _ in range(4):
        t = [_dot3s(jnp.concatenate([ti, pi], axis=0), bd_cat(pi)) for ti, pi in zip(tinv, p)]
        tinv = [ti + x[:c] for ti, x in zip(tinv, t)]
        p = [x[c:] for x in t]
    tinv = [ti + _dot3s(ti, bd_cat(pi)) for ti, pi in zip(tinv, p)]
    us = [_dot3s(ti, bd_nat(s["vb"])) for ti, s in zip(tinv, st)]
    ws = [_dot3s(ti, bd_nat(s["kbeg"])) for ti, s in zip(tinv, st)]
    qks = [lax.dot_general(s["q"].astype(BF16), s["kdh"], _NT, preferred_element_type=F32) for s in st]
    kdts = [jnp.concatenate([lax.dot_general(eye_b, s["kd"][:, h * LANES:(h + 1) * LANES], _NT,
                                             preferred_element_type=F32) for h in range(nh)], axis=0)
            for s in st]
    for s, u, w, qk, kdt in zip(st, us, ws, qks, kdts):
        u_ref[0, s["d"], s["rows"], :] = u
        w_ref[0, s["d"], s["rows"], :] = w
        qk_ref[0, s["d"], s["ch"]] = qk * s["decay"]
        kdt_ref[0, s["d"], s["ch"]] = kdt.astype(BF16)


def _gdn_scan_body(*refs):
    ins, o_refs, s_ref = (refs[:6], refs[6:12]), refs[12:14], refs[14]

    @pl.when(pl.program_id(1) == 0)
    def _():
        s_ref[...] = jnp.zeros_like(s_ref)

    c = GD_CHUNK
    heads = range(GD_HEADS)
    sls = [slice(h * LANES, (h + 1) * LANES) for h in heads]
    hl = lax.broadcasted_iota(jnp.int32, (c, GD_HEADS * LANES), 1) // LANES
    chains = [(d, h) for d in range(2) for h in heads]
    ns = ins[0][3].shape[2]
    ss = [s_ref[d, h] for d, h in chains]
    for step in range(ns):
        pos = [step, ns - 1 - step]
        rows = [slice(p * c, (p + 1) * c) for p in pos]
        rs = [_dot1(jnp.concatenate([ins[d][1][0, 0, rows[d], sls[h]], ins[d][2][0, 0, rows[d], sls[h]]],
                                    axis=0), s) for (d, h), s in zip(chains, ss)]
        v_news = [ins[d][0][0, 0, rows[d], sls[h]] - r[:c] for (d, h), r in zip(chains, rs)]
        upd = [_dot1(ins[d][4][0, 0, pos[d], sls[h], :], vn) for (d, h), vn in zip(chains, v_news)]
        for d in range(2):
            lo = d * GD_HEADS
            v_all = jnp.concatenate(v_news[lo:lo + GD_HEADS], axis=1)
            vbd = jnp.concatenate([jnp.where(hl == h, v_all, 0.0) for h in heads], axis=0)
            o_refs[d][0, rows[d], :] = (jnp.concatenate([r[c:] for r in rs[lo:lo + GD_HEADS]], axis=1)
                                        + _dot1(ins[d][3][0, 0, pos[d]], vbd))
        ss = [s * ins[d][5][0, 0, pos[d], :, sls[h]] + up for (d, h), s, up in zip(chains, ss, upd)]
    for (d, h), s in zip(chains, ss):
        s_ref[d, h] = s


def _gdn(q, k, v, gb, l_len):
    b, t, _ = q.shape
    c = GD_CHUNK
    nc, ncc = t // c, l_len // c
    nch = _pick(nc, (GD_LOCAL_CHUNKS, 1))
    h = GD_HEADS
    hw = h * LANES
    tokl = pl.BlockSpec((1, nch * c, hw), lambda bi, j: (bi, j, 0))
    tok4 = pl.BlockSpec((1, 2, nch * c, hw), lambda bi, j: (bi, 0, j, 0))
    ch5 = lambda r, w2: pl.BlockSpec((1, 2, nch, r, w2), lambda bi, j: (bi, 0, j, 0, 0))
    tok_shape = jax.ShapeDtypeStruct((b, 2, t, hw), F32)
    u, w, qg, qk, kdt, egl = pl.pallas_call(
        functools.partial(_gdn_local_body, nch=nch),
        grid=(b, nc // nch),
        in_specs=[tokl, tokl, tokl, pl.BlockSpec((1, nch * c, LANES), lambda bi, j: (bi, j, 0))],
        out_specs=[tok4, tok4, tok4, ch5(c, h * c), ch5(hw, c), ch5(1, hw)],
        out_shape=[tok_shape, tok_shape, tok_shape,
                   jax.ShapeDtypeStruct((b, 2, nc, c, h * c), F32),
                   jax.ShapeDtypeStruct((b, 2, nc, hw, c), BF16),
                   jax.ShapeDtypeStruct((b, 2, nc, 1, hw), F32)],
        compiler_params=_params(("parallel", "parallel")),
        name="gdn_local",
    )(q, k, v, gb)

    ns = _pick(math.gcd(ncc, nc), (4, 2, 1))

    def blk(d, j):
        first = j * ns
        last = jnp.where(first < ncc, ncc - 1 - first, nc - 1 - first + ncc) - (ns - 1)
        return j if d == 0 else last // ns

    def specs(d):
        stok = pl.BlockSpec((1, 1, ns * c, hw), lambda bi, j: (bi, d, blk(d, j), 0))
        sch = lambda r, w2: pl.BlockSpec((1, 1, ns, r, w2), lambda bi, j: (bi, d, blk(d, j), 0, 0))
        return [stok, stok, stok, sch(c, h * c), sch(hw, c), sch(1, hw)]

    out_spec = lambda d: pl.BlockSpec((1, ns * c, hw), lambda bi, j: (bi, blk(d, j), 0))
    out_shape = jax.ShapeDtypeStruct((b, t, hw), F32)
    return pl.pallas_call(
        _gdn_scan_body,
        grid=(b, nc // ns),
        in_specs=specs(0) + specs(1),
        out_specs=[out_spec(0), out_spec(1)],
        out_shape=[out_shape, out_shape],
        scratch_shapes=[pltpu.VMEM((2, h, GD_DK, GD_DV), F32)],
        compiler_params=_params(("parallel", "arbitrary")),
        name="gdn_scan",
    )(u, w, qg, qk, kdt, egl, u, w, qg, qk, kdt, egl)


def _merge_body(x_ref, oa_ref, ob_ref, og0_ref, og1_ref, z_ref, gl0_ref, gl1_ref, bg_ref, ng_ref,
                gt_ref, wb_ref, wo_ref, o_ref):
    d = x_ref.shape[1]
    parts = []
    for h in range(GD_HEADS):
        sl = slice(h * LANES, (h + 1) * LANES)
        o = og0_ref[:, sl] + og1_ref[:, sl]
        z = z_ref[:, sl]
        o = o * lax.rsqrt(jnp.mean(o * o, axis=-1, keepdims=True) + EPS) * ng_ref[...]
        parts.append(o * (z * jax.nn.sigmoid(z)))
    oc = jnp.concatenate(parts, axis=1)
    gl = jnp.concatenate([gl0_ref[...], gl1_ref[...]], axis=1)
    merged = None
    for r, o_r in enumerate((oa_ref[...], ob_ref[...], oc)):
        y = jnp.dot(o_r.astype(BF16), wb_ref[r], preferred_element_type=F32)
        t = jax.nn.sigmoid(gl[:, r * d:(r + 1) * d] + bg_ref[:, r * d:(r + 1) * d]) * y
        merged = t if merged is None else merged + t
    m = jnp.dot(merged.astype(BF16), wo_ref[...], preferred_element_type=F32)
    o_ref[...] = x_ref[...] + gt_ref[0] * m


def _merge(xs, oa, ob, og, proj, b_gate, norm_g, mod, wb, wo, l, bn, t_len, l_len):
    n, d = xs.shape
    tm = _pick(math.gcd(l_len, t_len), (256, 128))
    tpb, lt = t_len // tm, l_len // tm
    tok = lambda w: pl.BlockSpec((tm, w), lambda i: (i, 0))
    full = lambda a: pl.BlockSpec(a.shape, lambda i: (0,) * a.ndim)
    layer = lambda a: pl.BlockSpec((None,) + a.shape[1:], lambda i: (l,) + (0,) * (a.ndim - 1))
    return pl.pallas_call(
        _merge_body,
        grid=(n // tm,),
        in_specs=[tok(d), tok(BRANCH_W), tok(BRANCH_W), tok(BRANCH_W), tok(BRANCH_W),
                  pl.BlockSpec((tm, BRANCH_W), lambda i: (i, C_ZC // BRANCH_W)),
                  pl.BlockSpec((tm, CONV_BLK), lambda i: (i, C_GL // CONV_BLK)),
                  pl.BlockSpec((tm, CONV_BLK), lambda i: (i, C_GL // CONV_BLK + 1)),
                  full(b_gate), full(norm_g),
                  pl.BlockSpec((1, 1, d), lambda i: (_row_sel(i, tpb, lt, bn) * 6 + 2, 0, 0)),
                  layer(wb), layer(wo)],
        out_specs=tok(d),
        out_shape=jax.ShapeDtypeStruct((n, d), F32),
        compiler_params=_params(("parallel",)),
        name="merge",
    )(xs, oa, ob, og[0].reshape(n, BRANCH_W), og[1].reshape(n, BRANCH_W), proj, proj, proj,
      b_gate, norm_g, mod, wb, wo)


_CAND = [(r1, r2) for r1 in range(1, PEER_TOPK + 1) for r2 in range(1, PEER_TOPK + 1)
         if r1 * r2 <= PEER_TOPK]


def _extract_top(cur, n, want_rank=False):
    rank = jnp.full(cur.shape, 99.0, F32) if want_rank else None
    vals = []
    for r in range(n):
        m = jnp.max(cur, axis=0, keepdims=True)
        hit = cur == m
        if want_rank:
            rank = jnp.where(hit, float(r + 1), rank)
        cur = jnp.where(hit, -jnp.inf, cur)
        vals.append(m)
    return vals, rank


def _topk_body(x_ref, g_ref, sh_ref, sc_ref, wq_ref, keys_ref, h_ref, cnt1_ref, e1_ref, rk2_ref, e2_ref):
    hm = _modulate(x_ref[...], g_ref[...], sh_ref[0], sc_ref[0])
    h_ref[...] = (hm * math.sqrt(0.5)).astype(BF16)
    hmod = hm.astype(BF16)
    nr = PEER_TOPK
    for h in range(PEER_HEADS):
        st = []
        for c in range(2):
            col = (2 * h + c) * LANES
            qs = jnp.dot(hmod, wq_ref[:, col:col + LANES], preferred_element_type=F32)
            st.append(_dot3(keys_ref[c], qs, _NT))
        a, _ = _extract_top(st[0], nr)
        b, rank2 = _extract_top(st[1], nr, want_rank=True)
        rows = [a[r1 - 1] + b[r2 - 1] for r1, r2 in _CAND]
        pad = (-len(rows)) % SUBLANES
        cand = jnp.concatenate(rows + [jnp.full_like(rows[0], -jnp.inf)] * pad, axis=0)
        top, _ = _extract_top(cand, nr)
        tau = top[PEER_TOPK - 1]
        z = jnp.sum(jnp.where(cand >= tau, jnp.exp(cand - top[0]), 0.0), axis=0, keepdims=True)
        cnt1 = jnp.zeros_like(st[0])
        for r1 in range(1, PEER_TOPK + 1):
            cnt = None
            for r2 in range(1, PEER_TOPK + 1):
                if r1 * r2 <= PEER_TOPK:
                    t = (a[r1 - 1] + b[r2 - 1] >= tau).astype(F32)
                    cnt = t if cnt is None else cnt + t
            cnt1 = jnp.where(st[0] == a[r1 - 1], cnt, cnt1)
        cnt1_ref[h] = cnt1
        e1_ref[h] = jnp.exp(st[0] - a[0])
        rk2_ref[h] = rank2.astype(BF16)
        e2_ref[h] = (jnp.exp(st[1] - b[0]) * (1.0 / z)).astype(BF16)


def _peer_topk(xs, g, mod, wq, keys, l, bn, t_len, l_len):
    n, d = xs.shape
    tm = _pick(math.gcd(l_len, t_len), (256, 128))
    tpb, lt = t_len // tm, l_len // tm
    sel = lambda i: _row_sel(i, tpb, lt, bn)
    out = pl.BlockSpec((PEER_HEADS, N_KEYS, tm), lambda i: (0, 0, i))
    shp = lambda dt: jax.ShapeDtypeStruct((PEER_HEADS, N_KEYS, n), dt)
    return pl.pallas_call(
        _topk_body,
        grid=(n // tm,),
        in_specs=[pl.BlockSpec((tm, d), lambda i: (i, 0)),
                  pl.BlockSpec((1, d), lambda i: (0, 0)),
                  pl.BlockSpec((1, 1, d), lambda i: (sel(i) * 6 + 3, 0, 0)),
                  pl.BlockSpec((1, 1, d), lambda i: (sel(i) * 6 + 4, 0, 0)),
                  pl.BlockSpec((None,) + wq.shape[1:], lambda i: (l, 0, 0)),
                  pl.BlockSpec((None,) + keys.shape[1:], lambda i: (l, 0, 0, 0))],
        out_specs=[pl.BlockSpec((tm, d), lambda i: (i, 0)), out, out, out, out],
        out_shape=[jax.ShapeDtypeStruct((n, d), BF16), shp(F32), shp(F32), shp(BF16), shp(BF16)],
        compiler_params=_params(("parallel",)),
        name="peer_topk",
    )(xs, g, mod, mod, wq, keys)


DENSE_CHUNK = 256
DENSE_KDEPTH = 1024
DENSE_AHEAD = 3


def _dense_body(x_ref, h_ref, u_ref, vt_ref, cnt1_ref, e1_ref, rk2_ref, e2_ref, *rest, te):
    gt_refs, o_ref, acc_ref = rest[:-2], rest[-2], rest[-1]
    e = pl.program_id(1)

    @pl.when(e == 0)
    def _():
        acc_ref[...] = jnp.zeros_like(acc_ref)

    hs = h_ref[...]
    zero = jnp.zeros((N_KEYS, hs.shape[0]), BF16)
    ck = DENSE_CHUNK
    nck = te // ck

    def mm1(c):
        return lax.dot_general(u_ref[c * ck:(c + 1) * ck, :], hs, _NT, preferred_element_type=F32)

    def gate(c):
        parts = []
        for j in range(c * ck // N_KEYS, (c + 1) * ck // N_KEYS):
            i1 = e * (te // N_KEYS) + j
            g = zero
            for h in range(PEER_HEADS):
                cnt = cnt1_ref[h, pl.ds(i1, 1), :].astype(BF16)
                w1 = e1_ref[h, pl.ds(i1, 1), :].astype(BF16)
                g = g + jnp.where(rk2_ref[h] <= cnt, e2_ref[h], zero) * w1
            parts.append(g)
        return jnp.concatenate(parts, axis=0)

    def mm2(c0, c1):
        hid = jnp.concatenate(hids[c0:c1], axis=0)
        acc_ref[...] += jnp.dot(vt_ref[:, c0 * ck:c1 * ck], hid, preferred_element_type=F32)

    grp = max(1, DENSE_KDEPTH // ck)
    acts = [mm1(c) for c in range(min(DENSE_AHEAD, nck))]
    hids = []
    pending = None
    for c in range(nck):
        if c + DENSE_AHEAD < nck:
            acts.append(mm1(c + DENSE_AHEAD))
        y = acts[c]
        hids.append((y * (1.0 + lax.erf(y))).astype(BF16) * gate(c))
        if pending is not None:
            mm2(*pending)
            pending = None
        if c % grp == grp - 1:
            pending = (c + 1 - grp, c + 1)
    mm2(*pending)

    @pl.when(e == pl.num_programs(1) - 1)
    def _():
        base = acc_ref.shape[1] // len(gt_refs)
        for k, gt_ref in enumerate(gt_refs):
            rs = slice(k * base, (k + 1) * base)
            o_ref[rs] = x_ref[rs] + (gt_ref[0] * math.sqrt(0.5)) * acc_ref[:, rs].T


def _peer_dense(xs, hmod, u, vt, cnt1, e1, rk2, e2, mod, l, bn, t_len, l_len):
    n, d = xs.shape
    ne = u.shape[1]
    base = _pick(math.gcd(l_len, t_len), (256, 128))
    nsub = 2 if n % (2 * base) == 0 else 1
    tm = nsub * base
    te = _pick(ne, (2048, 1024, 512))
    tpb, lt = t_len // base, l_len // base
    gts = [pl.BlockSpec((1, 1, d), lambda i, e, k=k: (_row_sel(i * nsub + k, tpb, lt, bn) * 6 + 5, 0, 0))
           for k in range(nsub)]
    sel = pl.BlockSpec((PEER_HEADS, N_KEYS, tm), lambda i, e: (0, 0, i))
    tok = pl.BlockSpec((tm, d), lambda i, e: (i, 0))
    return pl.pallas_call(
        functools.partial(_dense_body, te=te),
        grid=(n // tm, ne // te),
        in_specs=[tok, tok,
                  pl.BlockSpec((None, te, d), lambda i, e: (l, e, 0)),
                  pl.BlockSpec((None, d, te), lambda i, e: (l, 0, e)),
                  sel, sel, sel, sel] + gts,
        out_specs=tok,
        out_shape=jax.ShapeDtypeStruct((n, d), F32),
        scratch_shapes=[pltpu.VMEM((d, tm), F32)],
        compiler_params=_params(("parallel", "arbitrary")),
        name="peer_dense",
    )(xs, hmod, u, vt, cnt1, e1, rk2, e2, *([mod] * nsub))


def _rope_tables(rows, l_len):
    row = jnp.repeat(jnp.arange(rows), GRID_W).astype(F32)
    col = jnp.tile(jnp.arange(GRID_W), rows).astype(F32)
    inv = jnp.power(ROPE_THETA, -jnp.arange(ROPE_PAIRS, dtype=F32) / ROPE_PAIRS)
    ar = row[:, None] * inv
    ac = col[:, None] * inv
    ang = jnp.concatenate([ar, ar, ac, ac], axis=-1)
    cos = jnp.concatenate([jnp.ones((l_len, HEAD_DIM), F32), jnp.cos(ang)], axis=0)
    sin = jnp.concatenate([jnp.zeros((l_len, HEAD_DIM), F32), jnp.sin(ang)], axis=0)
    cos = jnp.tile(cos, (1, 2))
    sin = jnp.tile(sin, (1, 2))
    first = (jnp.arange(LANES) % (2 * ROPE_PAIRS)) < ROPE_PAIRS
    return cos, jnp.where(first, -sin, 0.0), jnp.where(first, 0.0, sin)


def _reorder_w_in(w):
    nl, d = w.shape[:2]
    qa_ka_va, qb, kb_vb = w[..., :1536], w[..., 1536:2048], w[..., 2048:2304]
    qkvc, zc, ab, gl = w[..., 2304:3840], w[..., 3840:4352], w[..., 4352:4368], w[..., 4368:]
    qb = qb.reshape(nl, d, WG_KV_HEADS, WG_GROUP, HEAD_DIM).transpose(0, 1, 3, 2, 4).reshape(nl, d, 512)
    pad = jnp.zeros((nl, d, C_GL - C_AB - ab.shape[-1]), w.dtype)
    return jnp.concatenate([qa_ka_va, qkvc, zc, qb, kb_vb, ab, pad, gl], axis=-1).astype(BF16)


def _reorder_w_branch(wb):
    nl, _, _, d = wb.shape
    wb1 = wb[:, 1].reshape(nl, WG_KV_HEADS, WG_GROUP, HEAD_DIM, d).transpose(0, 2, 1, 3, 4)
    return jnp.stack([wb[:, 0], wb1.reshape(nl, BRANCH_W, d), wb[:, 2]], axis=1).astype(BF16)


def kernel(x, c, ctx, c_ctx, norm1_g, norm2_g, w_ada, b_ada, w_in, b_gate, qk_norm_g, diff_lam,
           diff_subln_g, wg_sink, gd_conv_w, gd_a_log, gd_dt_bias, gd_norm_g, w_branch, w_out,
           peer_wq, peer_keys, peer_u, peer_v):
    bn, s_len, d = x.shape
    l_len = ctx.shape[1]
    t_len = l_len + s_len
    n_tok = bn * t_len
    depth = w_in.shape[0]
    cos, sa, sb = _rope_tables(s_len // GRID_W, l_len)

    xs = jnp.concatenate([ctx, x], axis=1).reshape(n_tok, d)
    cvec = jnp.concatenate([c, c_ctx[None]], axis=0)
    cact = jnp.pad(jax.nn.silu(cvec), ((0, 16 - (bn + 1) % 16), (0, 0))).astype(BF16)
    lane_pad = lambda a: jnp.pad(a.reshape(1, -1), ((0, 0), (0, LANES - a.size)))

    w_ada_b = w_ada.astype(BF16)
    w_in_b = _reorder_w_in(w_in)
    w_branch_b = _reorder_w_branch(w_branch)
    w_out_b = w_out.astype(BF16)
    wq_b = peer_wq.astype(BF16)
    u_b = peer_u.astype(BF16)
    vt_b = jnp.swapaxes(peer_v.astype(BF16), 1, 2)

    for l in range(depth):
        lam_init = 0.8 - 0.6 * math.exp(-0.3 * l)
        mod = (_mm(cact, w_ada_b, l)[:bn + 1] + b_ada[l]).reshape((bn + 1) * 6, 1, d)

        proj = _inproj(xs, norm1_g[l].reshape(1, d), mod, w_in_b, l, bn, t_len, l_len)
        qa, ka, va, qb, kb, vb, gq, gk, gv, gb = _prep(
            proj, cos, sa, sb, jnp.tile(qk_norm_g[l], (1, 2)), gd_conv_w[l],
            lane_pad(gd_a_log[l]), lane_pad(gd_dt_bias[l]), t_len, l_len)
        b3 = lambda a: a.reshape(bn, t_len, a.shape[-1])

        lv = diff_lam[l]
        lam = (jnp.exp(jnp.sum(lv[0, 0] * lv[0, 1])) - jnp.exp(jnp.sum(lv[1, 0] * lv[1, 1])) + lam_init)
        oa = _diff_attn(b3(qa), b3(ka), b3(va), lam.reshape(1), diff_subln_g[l].reshape(1, DA_VDIM),
                        1.0 - lam_init, l_len)
        ob = _win_attn(b3(qb), b3(kb), b3(vb), wg_sink[l], l_len)
        og = _gdn(b3(gq), b3(gk), b3(gv), b3(gb), l_len)

        xs = _merge(xs, oa.reshape(n_tok, BRANCH_W), ob.reshape(n_tok, BRANCH_W), og, proj,
                    b_gate[l].reshape(1, -1), gd_norm_g[l].reshape(1, GD_DV), mod, w_branch_b, w_out_b,
                    l, bn, t_len, l_len)

        hmod, cnt1, e1, rk2, e2 = _peer_topk(xs, norm2_g[l].reshape(1, d), mod, wq_b, peer_keys,
                                             l, bn, t_len, l_len)
        xs = _peer_dense(xs, hmod, u_b, vt_b, cnt1, e1, rk2, e2, mod, l, bn, t_len, l_len)

    return xs.reshape(bn, t_len, d)[:, l_len:]
```
